```python
import jax
import jax.numpy as jnp
from jax import lax
import numpy as np

D_MODEL = 1024
BATCH = 4
SEQ = 4096
DEPTH = 4
DEC_BATCH = 32
DEC_SEQ = 8
PAST_LEN = 8192
PAGE_SIZE = 128

HEAD_DIM = 64
FOX_HEADS = 8
FOX_KV_HEADS = 4
SB_HEADS = 8
SB_KV_HEADS = 4
NSA_HEADS = 8
NSA_KV_HEADS = 2
FOX_WIDTH = FOX_HEADS * HEAD_DIM
SB_WIDTH = SB_HEADS * HEAD_DIM
NSA_WIDTH = NSA_HEADS * HEAD_DIM
NSA_BLOCK = 64
NSA_TOPK = 16
NSA_WINDOW = 512
Q_BLOCK = 128
N_BRANCH = 3
N_STATE = 11
RMS_EPS = 1e-6
FORGET_BIAS = 3.0

SEGMENTS = (
    ("fox_q", FOX_HEADS * HEAD_DIM), ("fox_k", FOX_KV_HEADS * HEAD_DIM),
    ("fox_v", FOX_KV_HEADS * HEAD_DIM), ("fox_f", FOX_HEADS),
    ("sb_q", SB_HEADS * HEAD_DIM), ("sb_k", SB_KV_HEADS * HEAD_DIM), ("sb_v", SB_KV_HEADS * HEAD_DIM),
    ("nsa_q", NSA_HEADS * HEAD_DIM),
    ("nsa_kc", NSA_KV_HEADS * HEAD_DIM), ("nsa_vc", NSA_KV_HEADS * HEAD_DIM),
    ("nsa_ks", NSA_KV_HEADS * HEAD_DIM), ("nsa_vs", NSA_KV_HEADS * HEAD_DIM),
    ("nsa_kw", NSA_KV_HEADS * HEAD_DIM), ("nsa_vw", NSA_KV_HEADS * HEAD_DIM),
    ("nsa_g", 3 * NSA_HEADS),
    ("z_fox", FOX_WIDTH), ("z_sb", SB_WIDTH), ("z_nsa", NSA_WIDTH),
    ("merge", N_BRANCH * D_MODEL),
)
D_IN = sum(n for _, n in SEGMENTS)

kernel_name = "gated_fox_nsa_stickbreaking_decode_step"


def seg_offset(name):
    off = 0
    for nm, n in SEGMENTS:
        if nm == name:
            return off
        off += n
    return off


def split_proj(p):
    out = {}
    off = 0
    for nm, n in SEGMENTS:
        out[nm] = p[..., off:off + n]
        off += n
    return out


def heads(a):
    return a.reshape(*a.shape[:-1], a.shape[-1] // HEAD_DIM, HEAD_DIM)


def rmsnorm(x, g):
    xf = x.astype(jnp.float32)
    y = xf * lax.rsqrt(jnp.mean(xf * xf, axis=-1, keepdims=True) + RMS_EPS)
    return (y * g.astype(jnp.float32)).astype(x.dtype)


def masked_softmax(s, mask):
    s = jnp.where(mask, s, -jnp.inf)
    m = jnp.max(s, axis=-1, keepdims=True)
    m = jnp.where(jnp.isfinite(m), m, 0.0)
    e = jnp.exp(s - m)
    return e / jnp.maximum(jnp.sum(e, axis=-1, keepdims=True), jnp.finfo(jnp.float32).tiny)


def alibi_slopes(n):
    return jnp.power(2.0, -8.0 * jnp.arange(1, n + 1, dtype=jnp.float32) / n)


def sweep(fn, seq_args):
    B, S = seq_args[0].shape[:2]
    nb = S // Q_BLOCK
    blocked = tuple(a.reshape(B, nb, Q_BLOCK, *a.shape[2:]).swapaxes(0, 1) for a in seq_args)
    q_pos = jnp.arange(S, dtype=jnp.int32).reshape(nb, Q_BLOCK)
    out = lax.map(lambda xs: fn(*xs), blocked + (q_pos,))
    return out.swapaxes(0, 1).reshape(B, S, out.shape[-1])


def fox_block(q, fq, q_pos, k, v, fk):
    B, nq, H, Dh = q.shape
    G = k.shape[2]
    R = H // G
    qg = q.reshape(B, nq, G, R, Dh)
    s = jnp.einsum("bqgrd,bkgd->bgrqk", qg, k).astype(jnp.float32) * (Dh ** -0.5)
    fqg = fq.astype(jnp.float32).reshape(B, nq, G, R).transpose(0, 2, 3, 1)
    fkg = fk.astype(jnp.float32).reshape(B, -1, G, R).transpose(0, 2, 3, 1)
    s = s + fqg[..., :, None] - fkg[..., None, :]
    k_pos = jnp.arange(k.shape[1], dtype=jnp.int32)
    p = masked_softmax(s, k_pos[None, :] <= q_pos[:, None])
    o = jnp.einsum("bgrqk,bkgd->bqgrd", p.astype(v.dtype), v)
    return o.reshape(B, nq, H * Dh)


def sb_block(q, q_pos, k, v):
    B, nq, H, Dh = q.shape
    G = k.shape[2]
    R = H // G
    qg = q.reshape(B, nq, G, R, Dh)
    z = jnp.einsum("bqgrd,bkgd->bgrqk", qg, k).astype(jnp.float32) * (Dh ** -0.5)
    k_pos = jnp.arange(k.shape[1], dtype=jnp.int32)
    causal = k_pos[None, :] < q_pos[:, None]
    log_keep = jnp.where(causal, jax.nn.log_sigmoid(-z), 0.0)
    after = lax.cumsum(log_keep, axis=log_keep.ndim - 1, reverse=True) - log_keep
    a = jnp.where(causal, jnp.exp(jax.nn.log_sigmoid(z) + after), 0.0)
    o = jnp.einsum("bgrqk,bkgd->bqgrd", a.astype(v.dtype), v)
    return o.reshape(B, nq, H * Dh)


def pad_blocks(a):
    T = a.shape[1]
    Tp = -(-T // NSA_BLOCK) * NSA_BLOCK
    return jnp.pad(a, ((0, 0), (0, Tp - T), (0, 0), (0, 0)))


def compress(k, w):
    B, Tp, G, Dh = k.shape
    kb = k.reshape(B, Tp // NSA_BLOCK, NSA_BLOCK, G, Dh)
    return jnp.einsum("bnlgd,lg->bngd", kb, w.astype(k.dtype))


def to_blocks(k):
    B, Tp, G, Dh = k.shape
    return k.reshape(B, Tp // NSA_BLOCK, NSA_BLOCK, G, Dh).transpose(0, 3, 1, 2, 4)


def gather_blocks(blocks, idx):
    return jax.vmap(jax.vmap(lambda kb, i: kb[i]))(blocks, idx)


def nsa_block(q, gates, q_pos, kc, vc, ks_b, vs_b, kw, vw, kw_pos):
    B, nq, H, Dh = q.shape
    G = kc.shape[2]
    R = H // G
    NB = kc.shape[1]
    scale = Dh ** -0.5
    slopes = alibi_slopes(H).reshape(G, R)
    qg = q.reshape(B, nq, G, R, Dh)
    t = q_pos.astype(jnp.float32)
    blk = jnp.arange(NB, dtype=jnp.int32)
    blk_end = (blk + 1) * NSA_BLOCK - 1
    sc = jnp.einsum("bqgrd,bngd->bgrqn", qg, kc).astype(jnp.float32) * scale
    sc = sc - slopes[:, :, None, None] * (t[:, None] - blk_end.astype(jnp.float32)[None, :])
    pc = masked_softmax(sc, blk_end[None, :] <= q_pos[:, None])
    o_c = jnp.einsum("bgrqn,bngd->bqgrd", pc.astype(vc.dtype), vc)
    cur = q_pos // NSA_BLOCK
    forced = (blk[None, :] == 0) | (blk[None, :] == cur[:, None]) | (blk[None, :] == cur[:, None] - 1)
    allowed = blk[None, :] <= cur[:, None]
    imp = jnp.sum(pc, axis=2)
    score = jnp.where(allowed, jnp.where(forced, jnp.inf, imp), -jnp.inf)
    top_s, top_i = lax.top_k(score, min(NSA_TOPK, NB))
    n_sel = top_i.shape[-1]
    kg = gather_blocks(ks_b, top_i)
    vg = gather_blocks(vs_b, top_i)
    pos = top_i[..., None] * NSA_BLOCK + jnp.arange(NSA_BLOCK, dtype=jnp.int32)
    ss = jnp.einsum("bqgrd,bgqkld->bgrqkl", qg, kg).astype(jnp.float32) * scale
    ss = ss - slopes[:, :, None, None, None] * (t[:, None, None] - pos.astype(jnp.float32))[:, :, None]
    ms = (top_s > -jnp.inf)[..., None] & (pos <= q_pos[:, None, None])
    ps = masked_softmax(ss.reshape(B, G, R, nq, n_sel * NSA_BLOCK),
                        ms.reshape(B, G, 1, nq, n_sel * NSA_BLOCK))
    o_s = jnp.einsum("bgrqn,bgqnd->bqgrd", ps.astype(vg.dtype),
                     vg.reshape(B, G, nq, n_sel * NSA_BLOCK, Dh))
    sw = jnp.einsum("bqgrd,bkgd->bgrqk", qg, kw).astype(jnp.float32) * scale
    dist = q_pos[:, None] - kw_pos[None, :]
    sw = sw - slopes[:, :, None, None] * dist.astype(jnp.float32)
    mw = (dist >= 0) & (dist < NSA_WINDOW) & (kw_pos[None, :] >= 0)
    pw = masked_softmax(sw, mw)
    o_w = jnp.einsum("bgrqk,bkgd->bqgrd", pw.astype(vw.dtype), vw)
    g = jax.nn.sigmoid(gates.astype(jnp.float32)).reshape(B, nq, G, R, 3)
    o = g[..., 0:1] * o_c + g[..., 1:2] * o_s + g[..., 2:3] * o_w
    return o.reshape(B, nq, H * Dh).astype(q.dtype)


def mix_prompt(h, w_in_l, b_in_l, wck, wcv):
    B, S, _ = h.shape
    seg = split_proj(jnp.einsum("bsd,de->bse", h, w_in_l) + b_in_l)
    fq, fk, fv = heads(seg["fox_q"]), heads(seg["fox_k"]), heads(seg["fox_v"])
    logf = jax.nn.log_sigmoid(seg["fox_f"].astype(jnp.float32))
    F = jnp.cumsum(logf, axis=1)
    o_fox = sweep(lambda q, fqb, qp: fox_block(q, fqb, qp, fk, fv, F), (fq, F))
    sq, sk, sv = heads(seg["sb_q"]), heads(seg["sb_k"]), heads(seg["sb_v"])
    o_sb = sweep(lambda q, qp: sb_block(q, qp, sk, sv), (sq,))
    nsa_q = heads(seg["nsa_q"])
    kc, vc = heads(seg["nsa_kc"]), heads(seg["nsa_vc"])
    ks, vs = heads(seg["nsa_ks"]), heads(seg["nsa_vs"])
    kw, vw = heads(seg["nsa_kw"]), heads(seg["nsa_vw"])
    kc_b = compress(pad_blocks(kc), wck)
    vc_b = compress(pad_blocks(vc), wcv)
    ks_b = to_blocks(pad_blocks(ks))
    vs_b = to_blocks(pad_blocks(vs))
    kw_pad = jnp.pad(kw, ((0, 0), (NSA_WINDOW, 0), (0, 0), (0, 0)))
    vw_pad = jnp.pad(vw, ((0, 0), (NSA_WINDOW, 0), (0, 0), (0, 0)))
    n_win = NSA_WINDOW + Q_BLOCK

    def nsa_fn(q, g, qp):
        q0 = qp[0]
        kw_pos = q0 - NSA_WINDOW + jnp.arange(n_win, dtype=jnp.int32)
        kwb = lax.dynamic_slice_in_dim(kw_pad, q0, n_win, axis=1)
        vwb = lax.dynamic_slice_in_dim(vw_pad, q0, n_win, axis=1)
        return nsa_block(q, g, qp, kc_b, vc_b, ks_b, vs_b, kwb, vwb, kw_pos)

    o_nsa = sweep(nsa_fn, (nsa_q, seg["nsa_g"]))
    keep = min(NSA_WINDOW, S)
    rows = (fk, fv, logf, sk, sv, kc, vc, ks, vs, kw[:, S - keep:], vw[:, S - keep:])
    return seg, o_fox, o_sb, o_nsa, rows


def mix_sample(h, past, w_in_l, b_in_l, wck, wcv):
    B, Sd, _ = h.shape
    (p_fk, p_fv, p_logf, p_sk, p_sv, p_kc, p_vc, p_ks, p_vs, p_kw, p_vw) = past
    P = p_fk.shape[1]
    q_pos = P + jnp.arange(Sd, dtype=jnp.int32)
    seg = split_proj(jnp.einsum("bsd,de->bse", h, w_in_l) + b_in_l)
    cat = lambda a, b: jnp.concatenate([a, b], axis=1)
    fq, fk, fv = heads(seg["fox_q"]), heads(seg["fox_k"]), heads(seg["fox_v"])
    logf = jax.nn.log_sigmoid(seg["fox_f"].astype(jnp.float32))
    F = jnp.cumsum(cat(p_logf.astype(jnp.float32), logf), axis=1)
    o_fox = fox_block(fq, F[:, P:], q_pos, cat(p_fk, fk), cat(p_fv, fv), F)
    sq, sk, sv = heads(seg["sb_q"]), heads(seg["sb_k"]), heads(seg["sb_v"])
    o_sb = sb_block(sq, q_pos, cat(p_sk, sk), cat(p_sv, sv))
    nsa_q = heads(seg["nsa_q"])
    kc, vc = heads(seg["nsa_kc"]), heads(seg["nsa_vc"])
    ks, vs = heads(seg["nsa_ks"]), heads(seg["nsa_vs"])
    kw, vw = heads(seg["nsa_kw"]), heads(seg["nsa_vw"])
    kc_b = compress(pad_blocks(cat(p_kc, kc)), wck)
    vc_b = compress(pad_blocks(cat(p_vc, vc)), wcv)
    ks_b = to_blocks(pad_blocks(cat(p_ks, ks)))
    vs_b = to_blocks(pad_blocks(cat(p_vs, vs)))
    wb = p_kw.shape[1]
    kw_all = cat(p_kw, kw)
    vw_all = cat(p_vw, vw)
    kw_pos = P - wb + jnp.arange(wb + Sd, dtype=jnp.int32)
    o_nsa = nsa_block(nsa_q, seg["nsa_g"], q_pos, kc_b, vc_b, ks_b, vs_b, kw_all, vw_all, kw_pos)
    rows = (fk, fv, logf, sk, sv, kc, vc, ks, vs, kw_all[:, Sd:], vw_all[:, Sd:])
    return seg, o_fox, o_sb, o_nsa, rows


def merge_out(seg, o_fox, o_sb, o_nsa, wbf, wbs, wbn, w_out_l):
    y_f = jnp.einsum("bsw,wd->bsd", o_fox * jax.nn.silu(seg["z_fox"]), wbf)
    y_s = jnp.einsum("bsw,wd->bsd", o_sb * jax.nn.silu(seg["z_sb"]), wbs)
    y_n = jnp.einsum("bsw,wd->bsd", o_nsa * jax.nn.silu(seg["z_nsa"]), wbn)
    gm = jax.nn.sigmoid(seg["merge"]).reshape(*y_f.shape[:-1], N_BRANCH, D_MODEL)
    y = gm[..., 0, :] * y_f + gm[..., 1, :] * y_s + gm[..., 2, :] * y_n
    return jnp.einsum("bsd,de->bse", y, w_out_l)


def gather_pages(pool_l, page_table):
    g = pool_l[page_table]
    return g.reshape(page_table.shape[0], page_table.shape[1] * g.shape[2], *g.shape[3:])


def setup_inputs(seed: int = 0) -> dict:
    key = jax.random.key(seed)
    ks = jax.random.split(key, 28)
    f32 = jnp.float32
    n_pages = PAST_LEN // PAGE_SIZE
    n_pool = (DEC_BATCH * n_pages * 5) // 4
    w_buf = min(NSA_WINDOW, PAST_LEN)

    def nrm(k, shape, s=1.0):
        return s * jax.random.normal(k, shape, f32)

    def pool(k, kvh):
        return nrm(k, (DEPTH, n_pool, PAGE_SIZE, kvh, HEAD_DIM))

    page_table = jax.random.permutation(ks[0], n_pool)[: DEC_BATCH * n_pages]
    page_table = page_table.reshape(DEC_BATCH, n_pages).astype(jnp.int32)
    off_f = seg_offset("fox_f")
    b_in = nrm(ks[1], (DEPTH, D_IN), 0.02).at[:, off_f:off_f + FOX_HEADS].add(FORGET_BIAS)
    return {
        "x_prompt": nrm(ks[2], (BATCH, SEQ, D_MODEL)),
        "x_sample": nrm(ks[3], (DEC_BATCH, DEC_SEQ, D_MODEL)),
        "cache_fox_k": pool(ks[4], FOX_KV_HEADS),
        "cache_fox_v": pool(ks[5], FOX_KV_HEADS),
        "cache_fox_logf": jax.nn.log_sigmoid(FORGET_BIAS + nrm(ks[6], (DEPTH, n_pool, PAGE_SIZE, FOX_HEADS), 0.5)),
        "cache_sb_k": pool(ks[7], SB_KV_HEADS),
        "cache_sb_v": pool(ks[8], SB_KV_HEADS),
        "cache_nsa_kc": pool(ks[9], NSA_KV_HEADS),
        "cache_nsa_vc": pool(ks[10], NSA_KV_HEADS),
        "cache_nsa_ks": pool(ks[11], NSA_KV_HEADS),
        "cache_nsa_vs": pool(ks[12], NSA_KV_HEADS),
        "state_nsa_kw": nrm(ks[13], (DEPTH, DEC_BATCH, w_buf, NSA_KV_HEADS, HEAD_DIM)),
        "state_nsa_vw": nrm(ks[14], (DEPTH, DEC_BATCH, w_buf, NSA_KV_HEADS, HEAD_DIM)),
        "page_table": page_table,
        "norm_g": 1.0 + nrm(ks[15], (DEPTH, D_MODEL), 0.02),
        "w_in": nrm(ks[16], (DEPTH, D_MODEL, D_IN), D_MODEL ** -0.5),
        "b_in": b_in,
        "w_cmp_k": (1.0 + nrm(ks[17], (DEPTH, NSA_BLOCK, NSA_KV_HEADS), 0.1)) / NSA_BLOCK,
        "w_cmp_v": (1.0 + nrm(ks[18], (DEPTH, NSA_BLOCK, NSA_KV_HEADS), 0.1)) / NSA_BLOCK,
        "w_br_fox": nrm(ks[19], (DEPTH, FOX_WIDTH, D_MODEL), FOX_WIDTH ** -0.5),
        "w_br_sb": nrm(ks[20], (DEPTH, SB_WIDTH, D_MODEL), SB_WIDTH ** -0.5),
        "w_br_nsa": nrm(ks[21], (DEPTH, NSA_WIDTH, D_MODEL), NSA_WIDTH ** -0.5),
        "w_out": nrm(ks[22], (DEPTH, D_MODEL, D_MODEL), D_MODEL ** -0.5),
        "final_norm_g": 1.0 + nrm(ks[23], (D_MODEL,), 0.02),
    }


def reference(x_prompt, x_sample, cache_fox_k, cache_fox_v, cache_fox_logf, cache_sb_k, cache_sb_v,
              cache_nsa_kc, cache_nsa_vc, cache_nsa_ks, cache_nsa_vs, state_nsa_kw, state_nsa_vw,
              page_table, norm_g, w_in, b_in, w_cmp_k, w_cmp_v, w_br_fox, w_br_sb, w_br_nsa,
              w_out, final_norm_g):
    pools = (cache_fox_k, cache_fox_v, cache_fox_logf, cache_sb_k, cache_sb_v,
             cache_nsa_kc, cache_nsa_vc, cache_nsa_ks, cache_nsa_vs)
    xp, xs = x_prompt, x_sample
    rows_p, rows_s = [], []
    for l in range(DEPTH):
        seg, a, b, c, rows = mix_prompt(rmsnorm(xp, norm_g[l]), w_in[l], b_in[l], w_cmp_k[l], w_cmp_v[l])
        xp = xp + merge_out(seg, a, b, c, w_br_fox[l], w_br_sb[l], w_br_nsa[l], w_out[l])
        rows_p.append(rows)
        past = tuple(gather_pages(pl[l], page_table) for pl in pools) + (state_nsa_kw[l], state_nsa_vw[l])
        seg, a, b, c, rows = mix_sample(rmsnorm(xs, norm_g[l]), past, w_in[l], b_in[l], w_cmp_k[l], w_cmp_v[l])
        xs = xs + merge_out(seg, a, b, c, w_br_fox[l], w_br_sb[l], w_br_nsa[l], w_out[l])
        rows_s.append(rows)
    y_prompt = rmsnorm(xp, final_norm_g)
    y_sample = rmsnorm(xs, final_norm_g)
    (fox_k_p, fox_v_p, fox_logf_p, sb_k_p, sb_v_p, kc_p, vc_p, ks_p, vs_p, kw_p, vw_p) = [
        jnp.stack([r[i] for r in rows_p]) for i in range(N_STATE)]
    (fox_k_s, fox_v_s, fox_logf_s, sb_k_s, sb_v_s, kc_s, vc_s, ks_s, vs_s, kw_s, vw_s) = [
        jnp.stack([r[i] for r in rows_s]) for i in range(N_STATE)]
    return (y_prompt, y_sample,
            fox_k_p, fox_v_p, fox_logf_p, sb_k_p, sb_v_p, kc_p, vc_p, ks_p, vs_p, kw_p, vw_p,
            fox_k_s, fox_v_s, fox_logf_s, sb_k_s, sb_v_s, kc_s, vc_s, ks_s, vs_s, kw_s, vw_s)
```

```python
import functools

import numpy as np
import jax
import jax.numpy as jnp
from jax import lax
from jax.experimental import pallas as pl
from jax.experimental.pallas import tpu as pltpu

F32 = jnp.float32
BF16 = jnp.bfloat16

HEAD_DIM = 64
FOX_HEADS, FOX_KV = 8, 4
SB_HEADS, SB_KV = 8, 4
NSA_HEADS, NSA_KV = 8, 2
NSA_BLOCK = 64
NSA_TOPK = 16
RMS_EPS = 1e-6
LANES = 128
NEG = -1e30
BIG_SCORE = 8.0
EXP_ZERO = -104.0

C_FOXQ, C_FOXK, C_FOXV = 0, 512, 768
C_SBQ, C_SBK, C_SBV = 1024, 1536, 1792
C_NSAQ = 2048
C_ZF, C_ZS, C_ZN = 2560, 3072, 3584
C_MERGE = 4096
C_NKV = 7168
C_SMALL = 7936
N_PAD = 8064
GATE_LANE0 = 8

NSA_SLOPES = tuple(float(2.0 ** (-8.0 * (i + 1) / NSA_HEADS)) for i in range(NSA_HEADS))


def _tile(n, target, mult):
    best = None
    for d in range(mult, min(n, target) + 1, mult):
        if n % d == 0:
            best = d
    return best if best is not None else n


def _cparams(sem, vmem_mb=48):
    return pltpu.CompilerParams(dimension_semantics=sem, vmem_limit_bytes=vmem_mb * 1024 * 1024)


def _dot(a, b):
    return jnp.dot(a, b, preferred_element_type=F32)


def _dot_nt(a, b):
    return lax.dot_general(a, b, (((1,), (1,)), ((), ())), preferred_element_type=F32)


def _split3(x):
    hi = x.astype(BF16)
    r = x - hi.astype(F32)
    mid = r.astype(BF16)
    lo = (r - mid.astype(F32)).astype(BF16)
    return hi, mid, lo


def _dot_exact(x, m01):
    hi, mid, lo = _split3(x)
    return _dot(hi, m01) + _dot(mid, m01) + _dot(lo, m01)


def _dot_nt_exact_rhs(m01, x):
    hi, mid, lo = _split3(x)
    return _dot_nt(m01, hi) + _dot_nt(m01, mid) + _dot_nt(m01, lo)


def _log_sigmoid(x):
    return jnp.minimum(x, 0.0) - jnp.log1p(jnp.exp(-jnp.abs(x)))


def _softplus(x):
    return jnp.maximum(x, 0.0) + jnp.log1p(jnp.exp(-jnp.abs(x)))


def _iota(shape, dim):
    return lax.broadcasted_iota(jnp.int32, shape, dim)


def _swap_halves(x):
    return pltpu.roll(x, 64, 1)


def _rmsnorm_body(x, g):
    return x * lax.rsqrt(jnp.mean(x * x, axis=-1, keepdims=True) + RMS_EPS) * g


def _rmsnorm_kernel(x_ref, g_ref, o_ref):
    o_ref[...] = _rmsnorm_body(x_ref[...], g_ref[...]).astype(o_ref.dtype)


def _rmsnorm(x, g, out_dtype):
    T, D = x.shape
    tm = _tile(T, 1024, 16)
    return pl.pallas_call(
        _rmsnorm_kernel,
        grid=(T // tm,),
        in_specs=[pl.BlockSpec((tm, D), lambda i: (i, 0)), pl.BlockSpec((1, D), lambda i: (0, 0))],
        out_specs=pl.BlockSpec((tm, D), lambda i: (i, 0)),
        out_shape=jax.ShapeDtypeStruct((T, D), out_dtype),
        compiler_params=_cparams(("parallel",)),
        name="rmsnorm",
    )(x, g.reshape(1, D))


def _inproj_kernel(h_ref, w_ref, b_ref, o32_ref, o16_ref):
    acc = _dot(h_ref[...], w_ref[...]) + b_ref[...]
    o32_ref[...] = acc
    o16_ref[...] = acc.astype(BF16)


def _inproj(h, w, b):
    T, D = h.shape
    N = w.shape[1]
    tm = _tile(T, 1280, 16)
    tn = _tile(N, 1152, 128)
    return pl.pallas_call(
        _inproj_kernel,
        grid=(N // tn, T // tm),
        in_specs=[pl.BlockSpec((tm, D), lambda j, i: (i, 0)),
                  pl.BlockSpec((D, tn), lambda j, i: (0, j)),
                  pl.BlockSpec((1, tn), lambda j, i: (0, j))],
        out_specs=[pl.BlockSpec((tm, tn), lambda j, i: (i, j)),
                   pl.BlockSpec((tm, tn), lambda j, i: (i, j))],
        out_shape=[jax.ShapeDtypeStruct((T, N), F32), jax.ShapeDtypeStruct((T, N), BF16)],
        compiler_params=_cparams(("parallel", "parallel")),
        name="inproj",
    )(h, w, b)


def _merge_kernel(of_ref, os_ref, on_ref, zf_ref, zs_ref, zn_ref, g0_ref, g1_ref, g2_ref,
                  x_ref, wbr_ref, wout_ref, gn_ref, xo_ref, ho_ref):
    def branch(o_ref, z_ref, g_ref, k):
        z = z_ref[...]
        u = (o_ref[...] * (z * jax.nn.sigmoid(z))).astype(BF16)
        return jax.nn.sigmoid(g_ref[...]) * _dot(u, wbr_ref[k])

    y = branch(of_ref, zf_ref, g0_ref, 0) + branch(os_ref, zs_ref, g1_ref, 1) + branch(on_ref, zn_ref, g2_ref, 2)
    xn = x_ref[...] + _dot(y.astype(BF16), wout_ref[...])
    xo_ref[...] = xn
    ho_ref[...] = _rmsnorm_body(xn, gn_ref[...]).astype(ho_ref.dtype)


def _merge(o_fox, o_sb, o_nsa, p32, x, wbr, wout, g_next, h_dtype):
    T, D = x.shape
    W = o_fox.shape[1]
    tm = _tile(T, 256, 16)
    row = lambda c: (lambda i: (i, c))
    return pl.pallas_call(
        _merge_kernel,
        grid=(T // tm,),
        in_specs=[pl.BlockSpec((tm, W), row(0)), pl.BlockSpec((tm, W), row(0)), pl.BlockSpec((tm, W), row(0)),
                  pl.BlockSpec((tm, W), row(C_ZF // W)), pl.BlockSpec((tm, W), row(C_ZS // W)),
                  pl.BlockSpec((tm, W), row(C_ZN // W)),
                  pl.BlockSpec((tm, D), row(C_MERGE // D)), pl.BlockSpec((tm, D), row(C_MERGE // D + 1)),
                  pl.BlockSpec((tm, D), row(C_MERGE // D + 2)),
                  pl.BlockSpec((tm, D), row(0)),
                  pl.BlockSpec((3, W, D), lambda i: (0, 0, 0)),
                  pl.BlockSpec((D, D), lambda i: (0, 0)),
                  pl.BlockSpec((1, D), lambda i: (0, 0))],
        out_specs=[pl.BlockSpec((tm, D), row(0)), pl.BlockSpec((tm, D), row(0))],
        out_shape=[jax.ShapeDtypeStruct((T, D), F32), jax.ShapeDtypeStruct((T, D), h_dtype)],
        compiler_params=_cparams(("parallel",)),
        name="merge_out",
    )(o_fox, o_sb, o_nsa, p32, p32, p32, p32, p32, p32, x, wbr, wout, g_next.reshape(1, D))


def _foxgate_kernel(sm_ref, logf_ref, ft_ref, carry_ref, *, ts):
    @pl.when(pl.program_id(1) == 0)
    def _():
        carry_ref[...] = jnp.zeros_like(carry_ref)

    lf = _log_sigmoid(sm_ref[...])
    logf_ref[...] = lf[:, :FOX_HEADS]
    lft = lf.T[:FOX_HEADS, :]
    upper = (_iota((ts, ts), 0) <= _iota((ts, ts), 1)).astype(BF16)
    c = _dot_exact(lft, upper) + carry_ref[:, 0:1]
    ft_ref[0] = c
    carry_ref[...] = jnp.broadcast_to(c[:, ts - 1:ts], carry_ref.shape)


def _foxgate(p32, B, S):
    ts = _tile(S, 512, 128)
    nst = S // ts
    return pl.pallas_call(
        functools.partial(_foxgate_kernel, ts=ts),
        grid=(B, nst),
        in_specs=[pl.BlockSpec((ts, LANES), lambda b, s: (b * nst + s, C_SMALL // LANES))],
        out_specs=[pl.BlockSpec((ts, FOX_HEADS), lambda b, s: (b * nst + s, 0)),
                   pl.BlockSpec((1, FOX_HEADS, ts), lambda b, s: (b, 0, s))],
        out_shape=[jax.ShapeDtypeStruct((B * S, FOX_HEADS), F32), jax.ShapeDtypeStruct((B, FOX_HEADS, S), F32)],
        scratch_shapes=[pltpu.VMEM((FOX_HEADS, LANES), F32)],
        compiler_params=_cparams(("parallel", "arbitrary")),
        name="fox_gate",
    )(p32)


def _pair_queries(q_ref, qz_ref):
    tq = q_ref.shape[0]
    low = _iota((tq, LANES), 1) < HEAD_DIM
    qa, qb = q_ref[:, :LANES], q_ref[:, LANES:]
    qz_ref[0] = jnp.where(low, qa, 0.0).astype(BF16)
    qz_ref[1] = jnp.where(low, _swap_halves(qa), 0.0).astype(BF16)
    qz_ref[2] = jnp.where(low, 0.0, _swap_halves(qb)).astype(BF16)
    qz_ref[3] = jnp.where(low, 0.0, qb).astype(BF16)


def _pair_output(a0, a1, a2, a3, o_ref):
    low = _iota(a0.shape, 1) < HEAD_DIM
    o_ref[:, :LANES] = jnp.where(low, a0, _swap_halves(a1))
    o_ref[:, LANES:] = jnp.where(low, _swap_halves(a2), a3)


def _softmax_update(s, vt, m_ref, l_ref, acc_ref, idx):
    m_prev = m_ref[idx]
    m_new = jnp.maximum(m_prev, jnp.max(s, axis=1, keepdims=True))
    alpha = jnp.exp(m_prev - m_new)
    p = jnp.exp(s - m_new)
    l_ref[idx] = alpha * l_ref[idx] + jnp.sum(p, axis=1, keepdims=True)
    acc_ref[idx] = alpha * acc_ref[idx] + _dot(p.astype(BF16), vt)
    m_ref[idx] = m_new


def _fox_prompt_kernel(q_ref, k_ref, v_ref, ft_ref, o_ref, qz_ref, m_ref, l_ref, acc_ref, *, tq):
    qi = pl.program_id(2)
    _pair_queries(q_ref, qz_ref)
    m_ref[...] = jnp.full(m_ref.shape, NEG, F32)
    l_ref[...] = jnp.zeros_like(l_ref)
    acc_ref[...] = jnp.zeros_like(acc_ref)
    causal = _iota((tq, tq), 1) <= _iota((tq, tq), 0)

    def tile(kj, diag):
        k0 = pl.multiple_of(kj * tq, tq)
        kt = k_ref[pl.ds(k0, tq), :]
        vt = v_ref[pl.ds(k0, tq), :]
        fk = ft_ref[0, 0, kj]
        for hh in range(4):
            s = _dot_nt(qz_ref[hh], kt) - fk[hh:hh + 1, :]
            if diag:
                s = jnp.where(causal, s, NEG)
            _softmax_update(s, vt, m_ref, l_ref, acc_ref, hh)

    def body(kj, carry):
        tile(kj, False)
        return carry

    lax.fori_loop(0, qi, body, 0)
    tile(qi, True)
    outs = [acc_ref[hh] / l_ref[hh] for hh in range(4)]
    _pair_output(*outs, o_ref)


def _fox_prompt(p32, p16, ft4, B, S, tq):
    nq = S // tq
    n_pairs = FOX_KV // 2
    W = FOX_HEADS * HEAD_DIM
    return pl.pallas_call(
        functools.partial(_fox_prompt_kernel, tq=tq),
        grid=(B, n_pairs, nq),
        in_specs=[pl.BlockSpec((tq, 2 * LANES), lambda b, g, i: (b * nq + i, C_FOXQ // (2 * LANES) + g)),
                  pl.BlockSpec((S, LANES), lambda b, g, i: (b, C_FOXK // LANES + g)),
                  pl.BlockSpec((S, LANES), lambda b, g, i: (b, C_FOXV // LANES + g)),
                  pl.BlockSpec((1, 1, nq, 4, tq), lambda b, g, i: (b, g, 0, 0, 0))],
        out_specs=pl.BlockSpec((tq, 2 * LANES), lambda b, g, i: (b * nq + i, g)),
        out_shape=jax.ShapeDtypeStruct((B * S, W), F32),
        scratch_shapes=[pltpu.VMEM((4, tq, LANES), BF16), pltpu.VMEM((4, tq, LANES), F32),
                        pltpu.VMEM((4, tq, LANES), F32), pltpu.VMEM((4, tq, LANES), F32)],
        compiler_params=_cparams(("parallel", "parallel", "arbitrary")),
        name="fox_prompt",
    )(p32, p16, p16, ft4)


def _suffix_and_total(lk, uo):
    hi = lk.astype(BF16)
    lo = (lk - hi.astype(F32)).astype(BF16)
    r = _dot(hi, uo) + _dot(lo, uo)
    return r[:, :LANES], r[:, LANES:]


def _suffix_matrix():
    j = _iota((LANES, LANES), 0)
    s = _iota((LANES, LANES), 1)
    return jnp.concatenate([(j > s).astype(BF16), jnp.ones((LANES, LANES), BF16)], axis=1)


def _sb_tile(z, vt, carry, uo, mask):
    sp = _softplus(z)
    lk = -sp
    if mask is not None:
        lk = jnp.where(mask, lk, 0.0)
    after, tot = _suffix_and_total(lk, uo)
    a = jnp.exp(z - sp + after + carry)
    if mask is not None:
        a = jnp.where(mask, a, 0.0)
    return _dot(a.astype(BF16), vt), carry + tot


def _sb_prompt_kernel(q_ref, k_ref, v_ref, o_ref, qz_ref, carry_ref, acc_ref, *, tq):
    qi = pl.program_id(2)
    _pair_queries(q_ref, qz_ref)
    carry_ref[...] = jnp.zeros_like(carry_ref)
    acc_ref[...] = jnp.zeros_like(acc_ref)
    uo = _suffix_matrix()
    strict = _iota((tq, tq), 1) < _iota((tq, tq), 0)

    def tile(kj, diag):
        k0 = pl.multiple_of(kj * tq, tq)
        kt = k_ref[pl.ds(k0, tq), :]
        vt = v_ref[pl.ds(k0, tq), :]
        top = jnp.float32(NEG)
        for hh in range(4):
            z = _dot_nt(qz_ref[hh], kt)
            pv, c = _sb_tile(z, vt, carry_ref[hh], uo, strict if diag else None)
            acc_ref[hh] = acc_ref[hh] + pv
            carry_ref[hh] = c
            top = jnp.maximum(top, jnp.max(c))
        return top

    top0 = tile(qi, True)

    def cond(st):
        return jnp.logical_and(st[0] >= 0, st[1] > EXP_ZERO)

    def body(st):
        return st[0] - 1, tile(st[0], False)

    lax.while_loop(cond, body, (qi - 1, top0))
    _pair_output(acc_ref[0], acc_ref[1], acc_ref[2], acc_ref[3], o_ref)


def _sb_prompt(p32, p16, B, S, tq):
    nq = S // tq
    n_pairs = SB_KV // 2
    W = SB_HEADS * HEAD_DIM
    return pl.pallas_call(
        functools.partial(_sb_prompt_kernel, tq=tq),
        grid=(B, n_pairs, nq),
        in_specs=[pl.BlockSpec((tq, 2 * LANES), lambda b, g, i: (b * nq + i, C_SBQ // (2 * LANES) + g)),
                  pl.BlockSpec((S, LANES), lambda b, g, i: (b, C_SBK // LANES + g)),
                  pl.BlockSpec((S, LANES), lambda b, g, i: (b, C_SBV // LANES + g))],
        out_specs=pl.BlockSpec((tq, 2 * LANES), lambda b, g, i: (b * nq + i, g)),
        out_shape=jax.ShapeDtypeStruct((B * S, W), F32),
        scratch_shapes=[pltpu.VMEM((4, tq, LANES), BF16), pltpu.VMEM((4, tq, LANES), F32),
                        pltpu.VMEM((4, tq, LANES), F32)],
        compiler_params=_cparams(("parallel", "parallel", "arbitrary")),
        name="sb_prompt",
    )(p32, p16, p16)


def _nsa_group_queries(q_ref, qg_ref, rows):
    low = _iota((rows, LANES), 1) < HEAD_DIM
    rper = NSA_HEADS // NSA_KV
    for g in range(NSA_KV):
        for r in range(rper):
            h = rper * g + r
            blk = q_ref[:, (h // 2) * LANES:(h // 2 + 1) * LANES]
            if h % 2 != g:
                blk = _swap_halves(blk)
            keep = low if g == 0 else jnp.logical_not(low)
            qg_ref[g, r] = jnp.where(keep, blk, 0.0).astype(qg_ref.dtype)


def _nsa_output(tot, o_ref):
    low = _iota(tot[0][0].shape, 1) < HEAD_DIM
    for j in range(NSA_HEADS // 2):
        g = j // 2
        a, b = tot[g][2 * (j % 2)], tot[g][2 * (j % 2) + 1]
        if g == 0:
            blk = jnp.where(low, a, _swap_halves(b))
        else:
            blk = jnp.where(low, _swap_halves(a), b)
        o_ref[:, j * LANES:(j + 1) * LANES] = blk


def _topk_select_t(score_t, allowed_t, n_real, k):
    nbp = score_t.shape[0]
    sc = score_t[:n_real]
    n_idx = _iota(sc.shape, 0)
    cnt = jnp.zeros(sc.shape, F32)
    for m in range(n_real):
        row = sc[m:m + 1, :]
        cnt = cnt + jnp.where(n_idx > m, jnp.where(row >= sc, 1.0, 0.0), jnp.where(row > sc, 1.0, 0.0))
    sel = jnp.where(jnp.logical_and(cnt < k, allowed_t[:n_real]), 1.0, 0.0)
    if nbp > n_real:
        sel = jnp.concatenate([sel, jnp.zeros((nbp - n_real, sc.shape[1]), F32)], axis=0)
    return sel


def _nsa_prompt_kernel(q_ref, kc_ref, vc_ref, ks_ref, vs_ref, kw_ref, vw_ref, sm_ref, wk_ref, wv_ref,
                       e_ref, o_ref, kcb_ref, vcb_ref, qg_ref, selb_ref, m_ref, l_ref, acc_ref, tot_ref,
                       *, tq, S, win):
    qi = pl.program_id(1)
    nb = S // NSA_BLOCK
    nbp = kcb_ref.shape[0]
    nq = S // tq
    rper = NSA_HEADS // NSA_KV

    @pl.when(qi == 0)
    def _():
        def pool(x_ref, w_ref):
            x = x_ref[...].astype(F32).reshape(nb, NSA_BLOCK, LANES)
            c = jnp.sum(x * w_ref[...][None], axis=1)
            if nbp > nb:
                c = jnp.concatenate([c, jnp.zeros((nbp - nb, LANES), F32)], axis=0)
            return c.astype(BF16)
        kcb_ref[...] = pool(kc_ref, wk_ref)
        vcb_ref[...] = pool(vc_ref, wv_ref)

    _nsa_group_queries(q_ref, qg_ref, tq)
    sig = jax.nn.sigmoid(sm_ref[...])

    def gate(h, j):
        c = GATE_LANE0 + 3 * h + j
        return sig[:, c:c + 1]

    n_i = _iota((tq, nbp), 1)
    t_i = qi * tq + _iota((tq, nbp), 0)
    blk_end = (n_i + 1) * NSA_BLOCK - 1
    dist_c = (t_i - blk_end).astype(F32)
    valid_c = blk_end <= t_i
    n_t = _iota((nbp, tq), 0)
    cur_t = lax.shift_right_logical(qi * tq + _iota((nbp, tq), 1), 6)
    allowed_t = n_t <= cur_t
    forced_t = jnp.logical_or(n_t == 0, jnp.logical_or(n_t == cur_t, n_t == cur_t - 1))
    for g in range(NSA_KV):
        imp = jnp.zeros((tq, nbp), F32)
        for r in range(rper):
            h = rper * g + r
            sc = _dot_nt(qg_ref[g, r], kcb_ref[...]) - NSA_SLOPES[h] * dist_c
            sc = jnp.where(valid_c, sc, NEG)
            mx = jnp.max(sc, axis=1, keepdims=True)
            e = jnp.where(valid_c, jnp.exp(sc - mx), 0.0)
            pc = e / jnp.maximum(jnp.sum(e, axis=1, keepdims=True), jnp.finfo(F32).tiny)
            imp = imp + pc
            tot_ref[g, r] = gate(h, 0) * _dot(pc.astype(BF16), vcb_ref[...])
        score_t = jnp.where(allowed_t, jnp.where(forced_t, BIG_SCORE, imp.T), -1.0)
        sel_t = _topk_select_t(score_t, allowed_t, nb, NSA_TOPK)
        selb = (_dot(sel_t.T.astype(BF16), e_ref[...]) - 1.0) * (-NEG)
        for j in range(nq):
            selb_ref[g, j] = selb[:, j * tq:(j + 1) * tq]

    rc = _iota((tq, tq), 0) - _iota((tq, tq), 1)
    causal = rc >= 0
    upper = rc < 0

    def reset():
        m_ref[...] = jnp.full(m_ref.shape, NEG, F32)
        l_ref[...] = jnp.zeros_like(l_ref)
        acc_ref[...] = jnp.zeros_like(acc_ref)

    def tile(k_ref, v_ref, kj, use_sel, mask):
        k0 = pl.multiple_of(kj * tq, tq)
        kt = k_ref[pl.ds(k0, tq), :]
        vt = v_ref[pl.ds(k0, tq), :]
        dist = ((qi - kj) * tq + rc).astype(F32)
        for g in range(NSA_KV):
            sb = selb_ref[g, kj] if use_sel else None
            for r in range(rper):
                s = _dot_nt(qg_ref[g, r], kt) - NSA_SLOPES[rper * g + r] * dist
                if use_sel:
                    s = s + sb
                if mask is not None:
                    s = jnp.where(mask, s, NEG)
                _softmax_update(s, vt, m_ref, l_ref, acc_ref, (g, r))

    def fold(branch):
        for g in range(NSA_KV):
            for r in range(rper):
                o = acc_ref[g, r] / l_ref[g, r]
                tot_ref[g, r] = tot_ref[g, r] + gate(rper * g + r, branch) * o

    reset()

    def sel_body(kj, c):
        tile(ks_ref, vs_ref, kj, True, None)
        return c

    lax.fori_loop(0, qi, sel_body, 0)
    tile(ks_ref, vs_ref, qi, True, causal)
    fold(1)

    reset()
    n_back = win // tq

    @pl.when(qi >= n_back)
    def _():
        tile(kw_ref, vw_ref, qi - n_back, False, upper)

    def win_body(kj, c):
        tile(kw_ref, vw_ref, kj, False, None)
        return c

    lax.fori_loop(jnp.maximum(qi - n_back + 1, 0), qi, win_body, 0)
    tile(kw_ref, vw_ref, qi, False, causal)
    fold(2)

    _nsa_output([[tot_ref[g, r] for r in range(rper)] for g in range(NSA_KV)], o_ref)


def _nsa_prompt(p32, p16, wk_exp, wv_exp, e_mat, B, S, tq, win):
    nq = S // tq
    nb = S // NSA_BLOCK
    nbp = e_mat.shape[0]
    W = NSA_HEADS * HEAD_DIM
    rper = NSA_HEADS // NSA_KV
    kv = lambda c: pl.BlockSpec((S, LANES), lambda b, i: (b, C_NKV // LANES + c))
    full = lambda a: pl.BlockSpec(a.shape, lambda b, i: (0,) * a.ndim)
    del nb
    return pl.pallas_call(
        functools.partial(_nsa_prompt_kernel, tq=tq, S=S, win=win),
        grid=(B, nq),
        in_specs=[pl.BlockSpec((tq, W), lambda b, i: (b * nq + i, C_NSAQ // W)),
                  kv(0), kv(1), kv(2), kv(3), kv(4), kv(5),
                  pl.BlockSpec((tq, LANES), lambda b, i: (b * nq + i, C_SMALL // LANES)),
                  full(wk_exp), full(wv_exp), full(e_mat)],
        out_specs=pl.BlockSpec((tq, W), lambda b, i: (b * nq + i, 0)),
        out_shape=jax.ShapeDtypeStruct((B * S, W), F32),
        scratch_shapes=[pltpu.VMEM((nbp, LANES), BF16), pltpu.VMEM((nbp, LANES), BF16),
                        pltpu.VMEM((NSA_KV, rper, tq, LANES), BF16),
                        pltpu.VMEM((NSA_KV, nq, tq, tq), F32),
                        pltpu.VMEM((NSA_KV, rper, tq, LANES), F32),
                        pltpu.VMEM((NSA_KV, rper, tq, LANES), F32),
                        pltpu.VMEM((NSA_KV, rper, tq, LANES), F32),
                        pltpu.VMEM((NSA_KV, rper, tq, LANES), F32)],
        compiler_params=_cparams(("parallel", "arbitrary"), 56),
        name="nsa_prompt",
    )(p32, p16, p16, p16, p16, p16, p16, p32, wk_exp, wv_exp, e_mat)


def _blockdiag_queries(q_ref, qbd_ref, n_heads, n_kv):
    sd = q_ref.shape[0]
    rper = n_heads // n_kv
    width = n_kv * HEAD_DIM
    low = _iota((sd, LANES), 1) < HEAD_DIM
    for h in range(n_heads):
        g = h // rper
        blk = q_ref[:, (h // 2) * LANES:(h // 2 + 1) * LANES]
        if h % 2 != g % 2:
            blk = _swap_halves(blk)
        keep = low if g % 2 == 0 else jnp.logical_not(low)
        blk = jnp.where(keep, blk, 0.0)
        parts = [blk if c == g // 2 else jnp.zeros((sd, LANES), F32) for c in range(width // LANES)]
        row = parts[0] if len(parts) == 1 else jnp.concatenate(parts, axis=1)
        qbd_ref[h * sd:(h + 1) * sd, :] = row.astype(BF16)


def _blockdiag_output(acc, o_ref, n_heads, n_kv, sd):
    rper = n_heads // n_kv
    low = _iota((sd, LANES), 1) < HEAD_DIM
    for j in range(n_heads // 2):
        pieces = []
        for h in (2 * j, 2 * j + 1):
            g = h // rper
            p = acc[h * sd:(h + 1) * sd, (g // 2) * LANES:(g // 2 + 1) * LANES]
            if g % 2 != h % 2:
                p = _swap_halves(p)
            pieces.append(p)
        o_ref[:, j * LANES:(j + 1) * LANES] = jnp.where(low, pieces[0], pieces[1])


def _pad_rows(x, rows):
    return jnp.concatenate([x, jnp.zeros((rows - x.shape[0], x.shape[1]), x.dtype)], axis=0)


def _fox_decode_kernel(pt_ref, q_ref, kn_ref, vn_ref, sm_ref, kp_ref, vp_ref, lfp_ref, mc_ref,
                       o_ref, lfo_ref, qbd_ref, m_ref, l_ref, acc_ref, cd_ref, *, sd, n_pages):
    del pt_ref
    p = pl.program_id(1)
    rows = FOX_HEADS * sd
    head_of_row = lax.shift_right_logical(_iota((rows, LANES), 0), 3)
    i_of_row = jnp.bitwise_and(_iota((rows, LANES), 0), sd - 1)
    col = _iota((rows, LANES), 1)

    @pl.when(p == 0)
    def _():
        _blockdiag_queries(q_ref, qbd_ref, FOX_HEADS, FOX_KV)
        lf = _log_sigmoid(sm_ref[...])
        lfo_ref[...] = lf[:, :FOX_HEADS]
        sel_h = jnp.where(col == head_of_row, 1.0, 0.0).astype(BF16)
        a = _dot_nt_exact_rhs(sel_h, _pad_rows(lf, LANES))
        incl = (_iota((LANES, LANES), 0) <= _iota((LANES, LANES), 1)).astype(BF16)
        bias_new = -_dot_exact(a, incl)
        kn = _pad_rows(kn_ref[...], LANES).astype(BF16)
        vn = _pad_rows(vn_ref[...], LANES).astype(BF16)
        s = _dot_nt(qbd_ref[...], kn) + bias_new
        s = jnp.where(col <= i_of_row, s, NEG)
        mx = jnp.max(s, axis=1, keepdims=True)
        pr = jnp.exp(s - mx)
        m_ref[...] = jnp.broadcast_to(mx, m_ref.shape)
        l_ref[...] = jnp.broadcast_to(jnp.sum(pr, axis=1, keepdims=True), l_ref.shape)
        acc_ref[...] = _dot(pr.astype(BF16), vn)
        cd_ref[...] = jnp.zeros_like(cd_ref)

    y = lfp_ref[0, 0]
    head_mask = jnp.bitwise_and(col, FOX_HEADS - 1) == head_of_row
    yexp = jnp.concatenate([jnp.where(head_mask, y[r:r + 1, :], 0.0) for r in range(8)], axis=1)
    res = _dot_exact(yexp, mc_ref[...])
    bias = res[:, :LANES] + cd_ref[...]
    s = _dot_nt(qbd_ref[...], kp_ref[0, 0].astype(BF16)) + bias
    m_prev = m_ref[...]
    m_new = jnp.maximum(m_prev, jnp.max(s, axis=1, keepdims=True))
    alpha = jnp.exp(m_prev - m_new)
    pr = jnp.exp(s - m_new)
    l_ref[...] = alpha * l_ref[...] + jnp.sum(pr, axis=1, keepdims=True)
    acc_ref[...] = jnp.concatenate([alpha, alpha], axis=1) * acc_ref[...] + _dot(pr.astype(BF16), vp_ref[0, 0].astype(BF16))
    m_ref[...] = m_new
    cd_ref[...] = cd_ref[...] + res[:, LANES:]

    @pl.when(p == n_pages - 1)
    def _():
        inv = 1.0 / l_ref[...]
        acc = acc_ref[...] * jnp.concatenate([inv, inv], axis=1)
        _blockdiag_output(acc, o_ref, FOX_HEADS, FOX_KV, sd)


def _fox_decode(p32, page_table, pool_k, pool_v, pool_lf, mc, layer, row0, DB, sd):
    n_pages = page_table.shape[1]
    page = pool_k.shape[2]
    Wq, Wk = FOX_HEADS * HEAD_DIM, FOX_KV * HEAD_DIM
    rows = FOX_HEADS * sd
    rb = row0 // sd
    pk = pool_k.reshape(pool_k.shape[0], pool_k.shape[1], page, Wk)
    pv = pool_v.reshape(pool_v.shape[0], pool_v.shape[1], page, Wk)
    plf = pool_lf.reshape(pool_lf.shape[0], pool_lf.shape[1], page * FOX_HEADS // LANES, LANES)
    pg = lambda b, p, pt: (layer, pt[b, n_pages - 1 - p], 0, 0)
    grid_spec = pltpu.PrefetchScalarGridSpec(
        num_scalar_prefetch=1,
        grid=(DB, n_pages),
        in_specs=[pl.BlockSpec((sd, Wq), lambda b, p, pt: (rb + b, C_FOXQ // Wq)),
                  pl.BlockSpec((sd, Wk), lambda b, p, pt: (rb + b, C_FOXK // Wk)),
                  pl.BlockSpec((sd, Wk), lambda b, p, pt: (rb + b, C_FOXV // Wk)),
                  pl.BlockSpec((sd, LANES), lambda b, p, pt: (rb + b, C_SMALL // LANES)),
                  pl.BlockSpec((1, 1, page, Wk), pg),
                  pl.BlockSpec((1, 1, page, Wk), pg),
                  pl.BlockSpec((1, 1, page * FOX_HEADS // LANES, LANES), pg),
                  pl.BlockSpec(mc.shape, lambda b, p, pt: (0, 0))],
        out_specs=[pl.BlockSpec((sd, Wq), lambda b, p, pt: (b, 0)),
                   pl.BlockSpec((sd, FOX_HEADS), lambda b, p, pt: (b, 0))],
        scratch_shapes=[pltpu.VMEM((rows, Wk), BF16), pltpu.VMEM((rows, LANES), F32),
                        pltpu.VMEM((rows, LANES), F32), pltpu.VMEM((rows, Wk), F32),
                        pltpu.VMEM((rows, LANES), F32)])
    return pl.pallas_call(
        functools.partial(_fox_decode_kernel, sd=sd, n_pages=n_pages),
        grid_spec=grid_spec,
        out_shape=[jax.ShapeDtypeStruct((DB * sd, Wq), F32), jax.ShapeDtypeStruct((DB * sd, FOX_HEADS), F32)],
        compiler_params=_cparams(("parallel", "arbitrary")),
        name="fox_decode",
    )(page_table, p32, p32, p32, p32, pk, pv, plf, mc)


def _sb_decode_kernel(pt_ref, q_ref, kn_ref, vn_ref, kp_ref, vp_ref, o_ref,
                      qbd_ref, carry_ref, acc_ref, top_ref, *, sd, n_pages):
    del pt_ref
    p = pl.program_id(1)
    rows = SB_HEADS * sd
    uo = _suffix_matrix()

    @pl.when(p == 0)
    def _():
        _blockdiag_queries(q_ref, qbd_ref, SB_HEADS, SB_KV)
        i_of_row = jnp.bitwise_and(_iota((rows, LANES), 0), sd - 1)
        strict = _iota((rows, LANES), 1) < i_of_row
        kn = _pad_rows(kn_ref[...], LANES).astype(BF16)
        vn = _pad_rows(vn_ref[...], LANES).astype(BF16)
        z = _dot_nt(qbd_ref[...], kn)
        pv, c = _sb_tile(z, vn, jnp.zeros((rows, LANES), F32), uo, strict)
        acc_ref[...] = pv
        carry_ref[...] = c
        top_ref[0] = jnp.max(c)

    @pl.when(top_ref[0] > EXP_ZERO)
    def _():
        z = _dot_nt(qbd_ref[...], kp_ref[0, 0].astype(BF16))
        pv, c = _sb_tile(z, vp_ref[0, 0].astype(BF16), carry_ref[...], uo, None)
        acc_ref[...] = acc_ref[...] + pv
        carry_ref[...] = c
        top_ref[0] = jnp.max(c)

    @pl.when(p == n_pages - 1)
    def _():
        _blockdiag_output(acc_ref[...], o_ref, SB_HEADS, SB_KV, sd)


def _sb_decode(p32, page_table, pool_k, pool_v, layer, row0, DB, sd):
    n_pages = page_table.shape[1]
    page = pool_k.shape[2]
    Wq, Wk = SB_HEADS * HEAD_DIM, SB_KV * HEAD_DIM
    rows = SB_HEADS * sd
    rb = row0 // sd
    pk = pool_k.reshape(pool_k.shape[0], pool_k.shape[1], page, Wk)
    pv = pool_v.reshape(pool_v.shape[0], pool_v.shape[1], page, Wk)
    pg = lambda b, p, pt: (layer, pt[b, n_pages - 1 - p], 0, 0)
    grid_spec = pltpu.PrefetchScalarGridSpec(
        num_scalar_prefetch=1,
        grid=(DB, n_pages),
        in_specs=[pl.BlockSpec((sd, Wq), lambda b, p, pt: (rb + b, C_SBQ // Wq)),
                  pl.BlockSpec((sd, Wk), lambda b, p, pt: (rb + b, C_SBK // Wk)),
                  pl.BlockSpec((sd, Wk), lambda b, p, pt: (rb + b, C_SBV // Wk)),
                  pl.BlockSpec((1, 1, page, Wk), pg),
                  pl.BlockSpec((1, 1, page, Wk), pg)],
        out_specs=pl.BlockSpec((sd, Wq), lambda b, p, pt: (b, 0)),
        scratch_shapes=[pltpu.VMEM((rows, Wk), BF16), pltpu.VMEM((rows, LANES), F32),
                        pltpu.VMEM((rows, Wk), F32), pltpu.SMEM((1,), F32)])
    return pl.pallas_call(
        functools.partial(_sb_decode_kernel, sd=sd, n_pages=n_pages),
        grid_spec=grid_spec,
        out_shape=jax.ShapeDtypeStruct((DB * sd, Wq), F32),
        compiler_params=_cparams(("parallel", "arbitrary")),
        name="sb_decode",
    )(page_table, p32, p32, p32, pk, pv)


def _nsa_decode_kernel(pt_ref, q_ref, ksn_ref, vsn_ref, kwn_ref, vwn_ref, sm_ref,
                       kcp_ref, vcp_ref, ksp_ref, vsp_ref, kws_ref, vws_ref, wk_ref, wv_ref,
                       o_ref, kcb_ref, vcb_ref, qg_ref, selr_ref, m_ref, l_ref, acc_ref, tot_ref,
                       *, sd, n_pages, past):
    del pt_ref
    j = pl.program_id(1)
    rper = NSA_HEADS // NSA_KV
    rows = rper * sd
    nbp = kcb_ref.shape[0]
    nbc = past // NSA_BLOCK
    page = kcp_ref.shape[2]
    bpp = page // NSA_BLOCK
    win = kws_ref.shape[2]

    i_row = jnp.bitwise_and(_iota((rows, LANES), 0), sd - 1)
    r_row = lax.shift_right_logical(_iota((rows, LANES), 0), 3)
    col = _iota((rows, LANES), 1)
    t_row = past + i_row

    def slopes(g):
        s = jnp.full((rows, LANES), NSA_SLOPES[rper * g], F32)
        for r in range(1, rper):
            s = jnp.where(r_row == r, NSA_SLOPES[rper * g + r], s)
        return s

    def group_q(g):
        return jnp.concatenate([qg_ref[g, r] for r in range(rper)], axis=0).astype(BF16)

    @pl.when(j == 0)
    def _():
        kcb_ref[...] = jnp.zeros_like(kcb_ref)
        vcb_ref[...] = jnp.zeros_like(vcb_ref)

    @pl.when(j < n_pages)
    def _():
        rown = _iota((nbp, LANES), 0)

        def pool(x_ref, w_ref, dst_ref):
            c = jnp.sum((x_ref[0, 0] * w_ref[...]).reshape(bpp, NSA_BLOCK, LANES), axis=1)
            cur = dst_ref[...]
            for u in range(bpp):
                cur = jnp.where(rown == j * bpp + u, c[u:u + 1, :], cur)
            dst_ref[...] = cur
        pool(kcp_ref, wk_ref, kcb_ref)
        pool(vcp_ref, wv_ref, vcb_ref)

    @pl.when(j == n_pages - 1)
    def _():
        _nsa_group_queries(q_ref, qg_ref, sd)
        sig = jax.nn.sigmoid(sm_ref[...])
        blk_end = (col + 1) * NSA_BLOCK - 1
        dist_c = (t_row - blk_end).astype(F32)
        valid_c = jnp.logical_and(blk_end <= t_row, col < nbc)
        kcb = kcb_ref[...].astype(BF16)
        vcb = vcb_ref[...].astype(BF16)
        n_i = _iota((sd, LANES), 1)
        allowed = n_i < nbc
        forced = jnp.logical_or(n_i == 0, n_i == nbc - 1)
        m_idx = _iota((LANES, LANES), 0)
        n_idx = _iota((LANES, LANES), 1)
        for g in range(NSA_KV):
            sc = _dot_nt(group_q(g), kcb) - slopes(g) * dist_c
            sc = jnp.where(valid_c, sc, NEG)
            mx = jnp.max(sc, axis=1, keepdims=True)
            e = jnp.where(valid_c, jnp.exp(sc - mx), 0.0)
            pc = e / jnp.maximum(jnp.sum(e, axis=1, keepdims=True), jnp.finfo(F32).tiny)
            gate_c = jnp.concatenate(
                [jnp.broadcast_to(sig[:, GATE_LANE0 + 3 * (rper * g + r):GATE_LANE0 + 3 * (rper * g + r) + 1],
                                  (sd, LANES)) for r in range(rper)], axis=0)
            tot_ref[g] = gate_c * _dot(pc.astype(BF16), vcb)
            imp = pc[0:sd]
            for r in range(1, rper):
                imp = imp + pc[r * sd:(r + 1) * sd]
            score = jnp.where(allowed, jnp.where(forced, BIG_SCORE, imp), -1.0)
            score_t = _pad_rows(score, LANES).T
            sel_rows = []
            for i in range(sd):
                cm = score_t[:, i:i + 1]
                rn = score[i:i + 1, :]
                beats = jnp.where(n_idx > m_idx, jnp.where(cm >= rn, 1.0, 0.0), jnp.where(cm > rn, 1.0, 0.0))
                cnt = jnp.sum(beats, axis=0, keepdims=True)
                sel_rows.append(jnp.where(jnp.logical_and(cnt < NSA_TOPK - 1, allowed[i:i + 1, :]), 1.0, 0.0))
            sel = jnp.concatenate(sel_rows, axis=0)
            selr_ref[g] = jnp.concatenate([sel] * rper, axis=0).astype(BF16)
        m_ref[...] = jnp.full(m_ref.shape, NEG, F32)
        l_ref[...] = jnp.zeros_like(l_ref)
        acc_ref[...] = jnp.zeros_like(acc_ref)

    def sm_update(g, s, vt):
        m_prev = m_ref[g]
        m_new = jnp.maximum(m_prev, jnp.max(s, axis=1, keepdims=True))
        alpha = jnp.exp(m_prev - m_new)
        pr = jnp.exp(s - m_new)
        l_ref[g] = alpha * l_ref[g] + jnp.sum(pr, axis=1, keepdims=True)
        acc_ref[g] = alpha * acc_ref[g] + _dot(pr.astype(BF16), vt)
        m_ref[g] = m_new

    @pl.when(j >= n_pages)
    def _():
        pidx = j - n_pages
        kt = ksp_ref[0, 0].astype(BF16)
        vt = vsp_ref[0, 0].astype(BF16)
        e_p = (_iota((LANES, LANES), 0) ==
               pidx * bpp + lax.shift_right_logical(_iota((LANES, LANES), 1), 6)).astype(BF16)
        dist = (t_row - (pidx * page + col)).astype(F32)
        for g in range(NSA_KV):
            selb = (_dot(selr_ref[g], e_p) - 1.0) * (-NEG)
            s = _dot_nt(group_q(g), kt) - slopes(g) * dist + selb
            sm_update(g, s, vt)

    @pl.when(j == 2 * n_pages - 1)
    def _():
        sig = jax.nn.sigmoid(sm_ref[...])

        def gate(g, branch):
            return jnp.concatenate(
                [jnp.broadcast_to(sig[:, GATE_LANE0 + 3 * (rper * g + r) + branch:
                                      GATE_LANE0 + 3 * (rper * g + r) + branch + 1], (sd, LANES))
                 for r in range(rper)], axis=0)

        new_ok = col <= i_row
        dist_new = (i_row - col).astype(F32)
        ksn = _pad_rows(ksn_ref[...], LANES).astype(BF16)
        vsn = _pad_rows(vsn_ref[...], LANES).astype(BF16)
        kwn = _pad_rows(kwn_ref[...], LANES).astype(BF16)
        vwn = _pad_rows(vwn_ref[...], LANES).astype(BF16)
        kws = kws_ref[0, 0].astype(BF16)
        vws = vws_ref[0, 0].astype(BF16)
        colw = _iota((rows, win), 1)
        iw = jnp.bitwise_and(_iota((rows, win), 0), sd - 1)
        rw = lax.shift_right_logical(_iota((rows, win), 0), 3)
        dist_w = (win + iw - colw).astype(F32)
        ok_w = colw > iw
        tots = []
        for g in range(NSA_KV):
            qg = group_q(g)
            sl = slopes(g)
            s = jnp.where(new_ok, _dot_nt(qg, ksn) - sl * dist_new, NEG)
            sm_update(g, s, vsn)
            o_sel = acc_ref[g] / l_ref[g]
            slw = jnp.full((rows, win), NSA_SLOPES[rper * g], F32)
            for r in range(1, rper):
                slw = jnp.where(rw == r, NSA_SLOPES[rper * g + r], slw)
            s1 = jnp.where(ok_w, _dot_nt(qg, kws) - slw * dist_w, NEG)
            s2 = jnp.where(new_ok, _dot_nt(qg, kwn) - sl * dist_new, NEG)
            mx = jnp.maximum(jnp.max(s1, axis=1, keepdims=True), jnp.max(s2, axis=1, keepdims=True))
            p1 = jnp.exp(s1 - mx)
            p2 = jnp.exp(s2 - mx)
            den = jnp.sum(p1, axis=1, keepdims=True) + jnp.sum(p2, axis=1, keepdims=True)
            o_win = (_dot(p1.astype(BF16), vws) + _dot(p2.astype(BF16), vwn)) / den
            t = tot_ref[g] + gate(g, 1) * o_sel + gate(g, 2) * o_win
            tots.append([t[r * sd:(r + 1) * sd] for r in range(rper)])
        _nsa_output(tots, o_ref)


def _nsa_decode(p32, page_table, pool_kc, pool_vc, pool_ks, pool_vs, st_kw, st_vw, wk2, wv2,
                layer, row0, DB, sd):
    n_pages = page_table.shape[1]
    page = pool_kc.shape[2]
    past = n_pages * page
    W = NSA_HEADS * HEAD_DIM
    rper = NSA_HEADS // NSA_KV
    rows = rper * sd
    nbp = LANES
    assert past // NSA_BLOCK <= nbp and past % NSA_BLOCK == 0 and sd <= NSA_BLOCK
    rb = row0 // sd
    win = st_kw.shape[2]
    flat = lambda a: a.reshape(a.shape[0], a.shape[1], a.shape[2], LANES)
    new = lambda c: pl.BlockSpec((sd, LANES), lambda b, j, pt: (rb + b, C_NKV // LANES + c))
    pg1 = lambda b, j, pt: (layer, pt[b, jnp.minimum(j, n_pages - 1)], 0, 0)
    pg2 = lambda b, j, pt: (layer, pt[b, jnp.maximum(j - n_pages, 0)], 0, 0)
    grid_spec = pltpu.PrefetchScalarGridSpec(
        num_scalar_prefetch=1,
        grid=(DB, 2 * n_pages),
        in_specs=[pl.BlockSpec((sd, W), lambda b, j, pt: (rb + b, C_NSAQ // W)),
                  new(2), new(3), new(4), new(5),
                  pl.BlockSpec((sd, LANES), lambda b, j, pt: (rb + b, C_SMALL // LANES)),
                  pl.BlockSpec((1, 1, page, LANES), pg1), pl.BlockSpec((1, 1, page, LANES), pg1),
                  pl.BlockSpec((1, 1, page, LANES), pg2), pl.BlockSpec((1, 1, page, LANES), pg2),
                  pl.BlockSpec((1, 1, win, LANES), lambda b, j, pt: (layer, b, 0, 0)),
                  pl.BlockSpec((1, 1, win, LANES), lambda b, j, pt: (layer, b, 0, 0)),
                  pl.BlockSpec(wk2.shape, lambda b, j, pt: (0, 0)),
                  pl.BlockSpec(wv2.shape, lambda b, j, pt: (0, 0))],
        out_specs=pl.BlockSpec((sd, W), lambda b, j, pt: (b, 0)),
        scratch_shapes=[pltpu.VMEM((nbp, LANES), F32), pltpu.VMEM((nbp, LANES), F32),
                        pltpu.VMEM((NSA_KV, rper, sd, LANES), F32),
                        pltpu.VMEM((NSA_KV, rows, LANES), BF16),
                        pltpu.VMEM((NSA_KV, rows, LANES), F32), pltpu.VMEM((NSA_KV, rows, LANES), F32),
                        pltpu.VMEM((NSA_KV, rows, LANES), F32), pltpu.VMEM((NSA_KV, rows, LANES), F32)])
    return pl.pallas_call(
        functools.partial(_nsa_decode_kernel, sd=sd, n_pages=n_pages, past=past),
        grid_spec=grid_spec,
        out_shape=jax.ShapeDtypeStruct((DB * sd, W), F32),
        compiler_params=_cparams(("parallel", "arbitrary")),
        name="nsa_decode",
    )(page_table, p32, p32, p32, p32, p32, p32, flat(pool_kc), flat(pool_vc), flat(pool_ks), flat(pool_vs),
      flat(st_kw), flat(st_vw), wk2, wv2)


def _prep_in_weights(w_in, b_in):
    segs = {}
    off = 0
    for name, n in (("fox_q", 512), ("fox_k", 256), ("fox_v", 256), ("fox_f", 8), ("sb_q", 512), ("sb_k", 256),
                    ("sb_v", 256), ("nsa_q", 512), ("nsa_kv", 768), ("nsa_g", 24), ("z", 1536), ("merge", 3072)):
        segs[name] = (off, off + n)
        off += n
    assert off == w_in.shape[-1]
    scale = HEAD_DIM ** -0.5

    def build(a):
        sl = lambda nm: a[..., segs[nm][0]:segs[nm][1]]
        pad = jnp.zeros(a.shape[:-1] + (LANES - 32,), a.dtype)
        return jnp.concatenate([sl("fox_q") * scale, sl("fox_k"), sl("fox_v"), sl("sb_q") * scale, sl("sb_k"),
                                sl("sb_v"), sl("nsa_q") * scale, sl("z"), sl("merge"), sl("nsa_kv"),
                                sl("fox_f"), sl("nsa_g"), pad], axis=-1)
    wp = build(w_in).astype(BF16)
    bp = build(b_in)[:, None, :]
    assert wp.shape[-1] == N_PAD
    return wp, bp


def _page_cumsum_matrix(page):
    rc = np.arange(page * FOX_HEADS)
    pos = (rc // LANES) * (LANES // FOX_HEADS) + (rc % LANES) // FOX_HEADS
    m = (pos[:, None] > np.arange(page)[None, :]).astype(np.float32)
    return jnp.asarray(np.concatenate([m, np.ones((page * FOX_HEADS, LANES), np.float32)], axis=1), BF16)


def kernel(x_prompt, x_sample, cache_fox_k, cache_fox_v, cache_fox_logf, cache_sb_k, cache_sb_v, cache_nsa_kc, cache_nsa_vc, cache_nsa_ks, cache_nsa_vs, state_nsa_kw, state_nsa_vw, page_table, norm_g, w_in, b_in, w_cmp_k, w_cmp_v, w_br_fox, w_br_sb, w_br_nsa, w_out, final_norm_g):
    B, S, D = x_prompt.shape
    DB, SD, _ = x_sample.shape
    depth = w_in.shape[0]
    page = cache_fox_k.shape[2]
    n_pages = page_table.shape[1]
    win = state_nsa_kw.shape[2]
    tq = LANES
    assert cache_fox_k.shape[3:] == (FOX_KV, HEAD_DIM) and cache_fox_logf.shape[3] == FOX_HEADS
    assert cache_sb_k.shape[3:] == (SB_KV, HEAD_DIM) and cache_nsa_kc.shape[3:] == (NSA_KV, HEAD_DIM)
    assert w_br_fox.shape[1] == FOX_HEADS * HEAD_DIM and w_br_sb.shape[1] == SB_HEADS * HEAD_DIM
    assert w_br_nsa.shape[1] == NSA_HEADS * HEAD_DIM and w_cmp_k.shape[1:] == (NSA_BLOCK, NSA_KV)
    assert SD == 8 and page == LANES and S % tq == 0 and win % tq == 0 and S >= win and n_pages * page >= win
    TP, TS = B * S, DB * SD
    assert TP % SD == 0

    x = jnp.concatenate([x_prompt.reshape(TP, D), x_sample.reshape(TS, D)], axis=0)
    wp, bp = _prep_in_weights(w_in, b_in)
    wbr = jnp.stack([w_br_fox, w_br_sb, w_br_nsa], axis=1).astype(BF16)
    wout = w_out.astype(BF16)
    wk_exp = jnp.repeat(w_cmp_k, HEAD_DIM, axis=2)
    wv_exp = jnp.repeat(w_cmp_v, HEAD_DIM, axis=2)
    bpp = page // NSA_BLOCK
    nbp = -(-(S // NSA_BLOCK) // LANES) * LANES
    e_mat = jnp.asarray((np.arange(nbp)[:, None] == (np.arange(S) // NSA_BLOCK)[None, :]).astype(np.float32), BF16)
    mc = _page_cumsum_matrix(page)
    nq = S // tq

    h = _rmsnorm(x, norm_g[0], BF16)
    rows_p, rows_s = [], []
    for l in range(depth):
        p32, p16 = _inproj(h, wp[l], bp[l])
        logf_p, ft = _foxgate(p32, B, S)
        ft4 = ft.reshape(B, FOX_KV // 2, 4, nq, tq).transpose(0, 1, 3, 2, 4)
        of_p = _fox_prompt(p32, p16, ft4, B, S, tq)
        os_p = _sb_prompt(p32, p16, B, S, tq)
        on_p = _nsa_prompt(p32, p16, wk_exp[l], wv_exp[l], e_mat, B, S, tq, win)
        of_s, logf_s = _fox_decode(p32, page_table, cache_fox_k, cache_fox_v, cache_fox_logf, mc, l, TP, DB, SD)
        os_s = _sb_decode(p32, page_table, cache_sb_k, cache_sb_v, l, TP, DB, SD)
        on_s = _nsa_decode(p32, page_table, cache_nsa_kc, cache_nsa_vc, cache_nsa_ks, cache_nsa_vs,
                           state_nsa_kw, state_nsa_vw, jnp.tile(wk_exp[l], (bpp, 1)), jnp.tile(wv_exp[l], (bpp, 1)),
                           l, TP, DB, SD)
        o_fox = jnp.concatenate([of_p, of_s], axis=0)
        o_sb = jnp.concatenate([os_p, os_s], axis=0)
        o_nsa = jnp.concatenate([on_p, on_s], axis=0)
        last = l + 1 == depth
        g_next = final_norm_g if last else norm_g[l + 1]
        x, h = _merge(o_fox, o_sb, o_nsa, p32, x, wbr[l], wout[l], g_next, F32 if last else BF16)

        def seg(c0, n_kv, lo, hi, lead):
            return p32[lo:hi, c0:c0 + n_kv * HEAD_DIM].reshape(*lead, n_kv, HEAD_DIM)

        def rows(lo, hi, lead, logf):
            nk = [seg(C_NKV + c * LANES, NSA_KV, lo, hi, lead) for c in range(6)]
            return [seg(C_FOXK, FOX_KV, lo, hi, lead), seg(C_FOXV, FOX_KV, lo, hi, lead),
                    logf.reshape(*lead, FOX_HEADS),
                    seg(C_SBK, SB_KV, lo, hi, lead), seg(C_SBV, SB_KV, lo, hi, lead)] + nk

        rp = rows(0, TP, (B, S), logf_p)
        rp[9], rp[10] = rp[9][:, S - win:], rp[10][:, S - win:]
        rs = rows(TP, TP + TS, (DB, SD), logf_s)
        rs[9] = jnp.concatenate([state_nsa_kw[l], rs[9]], axis=1)[:, SD:]
        rs[10] = jnp.concatenate([state_nsa_vw[l], rs[10]], axis=1)[:, SD:]
        rows_p.append(rp)
        rows_s.append(rs)

    y_prompt = h[:TP].reshape(B, S, D)
    y_sample = h[TP:].reshape(DB, SD, D)
    outs_p = [jnp.stack([r[i] for r in rows_p]) for i in range(11)]
    outs_s = [jnp.stack([r[i] for r in rows_s]) for i in range(11)]
    return (y_prompt, y_sample, *outs_p, *outs_s)
```

```python
import functools

import numpy as np
import jax
import jax.numpy as jnp
from jax import lax
from jax.experimental import pallas as pl
from jax.experimental.pallas import tpu as pltpu

F32 = jnp.float32
BF16 = jnp.bfloat16

HEAD_DIM = 64
FOX_HEADS, FOX_KV = 8, 4
SB_HEADS, SB_KV = 8, 4
NSA_HEADS, NSA_KV = 8, 2
NSA_BLOCK = 64
NSA_TOPK = 16
RMS_EPS = 1e-6
LANES = 128
NEG = -1e30
BIG_SCORE = 8.0
EXP_ZERO = -104.0

C_FOXQ, C_FOXK, C_FOXV = 0, 512, 768
C_SBQ, C_SBK, C_SBV = 1024, 1536, 1792
C_NSAQ = 2048
C_ZF, C_ZS, C_ZN = 2560, 3072, 3584
C_MERGE = 4096
C_NKV = 7168
C_SMALL = 7936
N_PAD = 8064
GATE_LANE0 = 8

NSA_SLOPES = tuple(float(2.0 ** (-8.0 * (i + 1) / NSA_HEADS)) for i in range(NSA_HEADS))


def _tile(n, target, mult):
    best = None
    for d in range(mult, min(n, target) + 1, mult):
        if n % d == 0:
            best = d
    return best if best is not None else n


def _cparams(sem, vmem_mb=48):
    return pltpu.CompilerParams(dimension_semantics=sem, vmem_limit_bytes=vmem_mb * 1024 * 1024)


def _dot(a, b):
    return jnp.dot(a, b, preferred_element_type=F32)


def _dot_nt(a, b):
    return lax.dot_general(a, b, (((1,), (1,)), ((), ())), preferred_element_type=F32)


def _split3(x):
    hi = x.astype(BF16)
    r = x - hi.astype(F32)
    mid = r.astype(BF16)
    lo = (r - mid.astype(F32)).astype(BF16)
    return hi, mid, lo


def _dot_exact(x, m01):
    hi, mid, lo = _split3(x)
    return _dot(hi, m01) + _dot(mid, m01) + _dot(lo, m01)


def _dot_nt_exact_rhs(m01, x):
    hi, mid, lo = _split3(x)
    return _dot_nt(m01, hi) + _dot_nt(m01, mid) + _dot_nt(m01, lo)


def _log_sigmoid(x):
    return jnp.minimum(x, 0.0) - jnp.log1p(jnp.exp(-jnp.abs(x)))


def _softplus(x):
    return jnp.maximum(x, 0.0) + jnp.log1p(jnp.exp(-jnp.abs(x)))


def _iota(shape, dim):
    return lax.broadcasted_iota(jnp.int32, shape, dim)


def _swap_halves(x):
    return pltpu.roll(x, 64, 1)


def _rmsnorm_body(x, g):
    return x * lax.rsqrt(jnp.mean(x * x, axis=-1, keepdims=True) + RMS_EPS) * g


def _rmsnorm_kernel(x_ref, g_ref, o_ref):
    o_ref[...] = _rmsnorm_body(x_ref[...], g_ref[...]).astype(o_ref.dtype)


def _rmsnorm(x, g, out_dtype):
    T, D = x.shape
    tm = _tile(T, 1024, 16)
    return pl.pallas_call(
        _rmsnorm_kernel,
        grid=(T // tm,),
        in_specs=[pl.BlockSpec((tm, D), lambda i: (i, 0)), pl.BlockSpec((1, D), lambda i: (0, 0))],
        out_specs=pl.BlockSpec((tm, D), lambda i: (i, 0)),
        out_shape=jax.ShapeDtypeStruct((T, D), out_dtype),
        compiler_params=_cparams(("parallel",)),
        name="rmsnorm",
    )(x, g.reshape(1, D))


def _inproj_kernel(h_ref, w_ref, b_ref, o32_ref, o16_ref):
    acc = _dot(h_ref[...], w_ref[...]) + b_ref[...]
    o32_ref[...] = acc
    o16_ref[...] = acc.astype(BF16)


def _inproj(h, w, b):
    T, D = h.shape
    N = w.shape[1]
    tm = _tile(T, 1280, 16)
    tn = _tile(N, 1152, 128)
    return pl.pallas_call(
        _inproj_kernel,
        grid=(N // tn, T // tm),
        in_specs=[pl.BlockSpec((tm, D), lambda j, i: (i, 0)),
                  pl.BlockSpec((D, tn), lambda j, i: (0, j)),
                  pl.BlockSpec((1, tn), lambda j, i: (0, j))],
        out_specs=[pl.BlockSpec((tm, tn), lambda j, i: (i, j)),
                   pl.BlockSpec((tm, tn), lambda j, i: (i, j))],
        out_shape=[jax.ShapeDtypeStruct((T, N), F32), jax.ShapeDtypeStruct((T, N), BF16)],
        compiler_params=_cparams(("parallel", "parallel")),
        name="inproj",
    )(h, w, b)


def _merge_kernel(of_ref, os_ref, on_ref, zf_ref, zs_ref, zn_ref, g0_ref, g1_ref, g2_ref,
                  x_ref, wbr_ref, wout_ref, gn_ref, xo_ref, ho_ref):
    def branch(o_ref, z_ref, g_ref, k):
        z = z_ref[...]
        u = (o_ref[...] * (z * jax.nn.sigmoid(z))).astype(BF16)
        return jax.nn.sigmoid(g_ref[...]) * _dot(u, wbr_ref[k])

    y = branch(of_ref, zf_ref, g0_ref, 0) + branch(os_ref, zs_ref, g1_ref, 1) + branch(on_ref, zn_ref, g2_ref, 2)
    xn = x_ref[...] + _dot(y.astype(BF16), wout_ref[...])
    xo_ref[...] = xn
    ho_ref[...] = _rmsnorm_body(xn, gn_ref[...]).astype(ho_ref.dtype)


def _merge(o_fox, o_sb, o_nsa, p32, x, wbr, wout, g_next, h_dtype):
    T, D = x.shape
    W = o_fox.shape[1]
    tm = _tile(T, 256, 16)
    row = lambda c: (lambda i: (i, c))
    return pl.pallas_call(
        _merge_kernel,
        grid=(T // tm,),
        in_specs=[pl.BlockSpec((tm, W), row(0)), pl.BlockSpec((tm, W), row(0)), pl.BlockSpec((tm, W), row(0)),
                  pl.BlockSpec((tm, W), row(C_ZF // W)), pl.BlockSpec((tm, W), row(C_ZS // W)),
                  pl.BlockSpec((tm, W), row(C_ZN // W)),
                  pl.BlockSpec((tm, D), row(C_MERGE // D)), pl.BlockSpec((tm, D), row(C_MERGE // D + 1)),
                  pl.BlockSpec((tm, D), row(C_MERGE // D + 2)),
                  pl.BlockSpec((tm, D), row(0)),
                  pl.BlockSpec((3, W, D), lambda i: (0, 0, 0)),
                  pl.BlockSpec((D, D), lambda i: (0, 0)),
                  pl.BlockSpec((1, D), lambda i: (0, 0))],
        out_specs=[pl.BlockSpec((tm, D), row(0)), pl.BlockSpec((tm, D), row(0))],
        out_shape=[jax.ShapeDtypeStruct((T, D), F32), jax.ShapeDtypeStruct((T, D), h_dtype)],
        compiler_params=_cparams(("parallel",)),
        name="merge_out",
    )(o_fox, o_sb, o_nsa, p32, p32, p32, p32, p32, p32, x, wbr, wout, g_next.reshape(1, D))


def _foxgate_kernel(sm_ref, logf_ref, ft_ref, carry_ref, *, ts):
    @pl.when(pl.program_id(1) == 0)
    def _():
        carry_ref[...] = jnp.zeros_like(carry_ref)

    lf = _log_sigmoid(sm_ref[...])
    logf_ref[...] = lf[:, :FOX_HEADS]
    lft = lf.T[:FOX_HEADS, :]
    upper = (_iota((ts, ts), 0) <= _iota((ts, ts), 1)).astype(BF16)
    c = _dot_exact(lft, upper) + carry_ref[:, 0:1]
    ft_ref[0] = c
    carry_ref[...] = jnp.broadcast_to(c[:, ts - 1:ts], carry_ref.shape)


def _foxgate(p32, B, S):
    ts = _tile(S, 512, 128)
    nst = S // ts
    return pl.pallas_call(
        functools.partial(_foxgate_kernel, ts=ts),
        grid=(B, nst),
        in_specs=[pl.BlockSpec((ts, LANES), lambda b, s: (b * nst + s, C_SMALL // LANES))],
        out_specs=[pl.BlockSpec((ts, FOX_HEADS), lambda b, s: (b * nst + s, 0)),
                   pl.BlockSpec((1, FOX_HEADS, ts), lambda b, s: (b, 0, s))],
        out_shape=[jax.ShapeDtypeStruct((B * S, FOX_HEADS), F32), jax.ShapeDtypeStruct((B, FOX_HEADS, S), F32)],
        scratch_shapes=[pltpu.VMEM((FOX_HEADS, LANES), F32)],
        compiler_params=_cparams(("parallel", "arbitrary")),
        name="fox_gate",
    )(p32)


def _pair_queries(q_ref, qz_ref):
    tq = q_ref.shape[0]
    low = _iota((tq, LANES), 1) < HEAD_DIM
    qa, qb = q_ref[:, :LANES], q_ref[:, LANES:]
    qz_ref[0] = jnp.where(low, qa, 0.0).astype(BF16)
    qz_ref[1] = jnp.where(low, _swap_halves(qa), 0.0).astype(BF16)
    qz_ref[2] = jnp.where(low, 0.0, _swap_halves(qb)).astype(BF16)
    qz_ref[3] = jnp.where(low, 0.0, qb).astype(BF16)


def _pair_output(a0, a1, a2, a3, o_ref):
    low = _iota(a0.shape, 1) < HEAD_DIM
    o_ref[:, :LANES] = jnp.where(low, a0, _swap_halves(a1))
    o_ref[:, LANES:] = jnp.where(low, _swap_halves(a2), a3)


def _rep(x, k):
    return x if k == 1 else jnp.concatenate([x] * k, axis=1)


def _stack_rows(row_vectors, rows):
    n = row_vectors[0].shape[1]
    return jnp.concatenate([jnp.broadcast_to(v, (rows, n)) for v in row_vectors], axis=0)


def _softmax_update(s, pv_fn, m_ref, l_ref, acc_ref, idx=Ellipsis):
    m_prev = m_ref[idx]
    m_new = jnp.maximum(m_prev, jnp.max(s, axis=1, keepdims=True))
    alpha = jnp.exp(m_prev - m_new)
    p = jnp.exp(s - _rep(m_new, s.shape[1] // LANES))
    l_ref[idx] = alpha * l_ref[idx] + jnp.sum(p, axis=1, keepdims=True)
    acc = acc_ref[idx]
    acc_ref[idx] = _rep(alpha, acc.shape[1] // LANES) * acc + pv_fn(p.astype(BF16))
    m_ref[idx] = m_new


def _fox_prompt_kernel(q_ref, k_ref, v_ref, ft_ref, o_ref, qz_ref, m_ref, l_ref, acc_ref, *, tq, tk):
    qi = pl.program_id(2)
    q0 = qi * tq
    _pair_queries(q_ref, qz_ref)
    m_ref[...] = jnp.full(m_ref.shape, NEG, F32)
    l_ref[...] = jnp.zeros_like(l_ref)
    acc_ref[...] = jnp.zeros_like(acc_ref)
    qz = qz_ref[...].reshape(4 * tq, LANES)

    def tile(kj, diag):
        k0 = pl.multiple_of(kj * tk, tk)
        kt = k_ref[pl.ds(k0, tk), :]
        vt = v_ref[pl.ds(k0, tk), :]
        fk = ft_ref[0, 0, kj]
        s = _dot_nt(qz, kt) - _stack_rows([fk[hh:hh + 1, :] for hh in range(4)], tq)
        if diag:
            t_idx = jnp.bitwise_and(_iota((4 * tq, tk), 0), tq - 1)
            s = jnp.where(k0 + _iota((4 * tq, tk), 1) <= q0 + t_idx, s, NEG)
        _softmax_update(s, lambda p: _dot(p, vt), m_ref, l_ref, acc_ref)

    def body(kj, carry):
        tile(kj, False)
        return carry

    n_full = q0 // tk
    lax.fori_loop(0, n_full, body, 0)
    tile(n_full, True)
    out = acc_ref[...] / l_ref[...]
    _pair_output(*[out[hh * tq:(hh + 1) * tq] for hh in range(4)], o_ref)


def _fox_prompt(p32, p16, ft4, B, S, tq, tk):
    nq = S // tq
    n_pairs = FOX_KV // 2
    W = FOX_HEADS * HEAD_DIM
    return pl.pallas_call(
        functools.partial(_fox_prompt_kernel, tq=tq, tk=tk),
        grid=(B, n_pairs, nq),
        in_specs=[pl.BlockSpec((tq, 2 * LANES), lambda b, g, i: (b * nq + i, C_FOXQ // (2 * LANES) + g)),
                  pl.BlockSpec((S, LANES), lambda b, g, i: (b, C_FOXK // LANES + g)),
                  pl.BlockSpec((S, LANES), lambda b, g, i: (b, C_FOXV // LANES + g)),
                  pl.BlockSpec((1, 1, S // tk, 4, tk), lambda b, g, i: (b, g, 0, 0, 0))],
        out_specs=pl.BlockSpec((tq, 2 * LANES), lambda b, g, i: (b * nq + i, g)),
        out_shape=jax.ShapeDtypeStruct((B * S, W), F32),
        scratch_shapes=[pltpu.VMEM((4, tq, LANES), BF16), pltpu.VMEM((4 * tq, LANES), F32),
                        pltpu.VMEM((4 * tq, LANES), F32), pltpu.VMEM((4 * tq, LANES), F32)],
        compiler_params=_cparams(("parallel", "parallel", "arbitrary")),
        name="fox_prompt",
    )(p32, p16, p16, ft4)


def _suffix_and_total(lk, uo):
    hi = lk.astype(BF16)
    lo = (lk - hi.astype(F32)).astype(BF16)
    r = _dot(hi, uo) + _dot(lo, uo)
    return r[:, :LANES], r[:, LANES:]


def _suffix_matrix():
    j = _iota((LANES, LANES), 0)
    s = _iota((LANES, LANES), 1)
    return jnp.concatenate([(j > s).astype(BF16), jnp.ones((LANES, LANES), BF16)], axis=1)


def _sb_blocks(zs, carry, uo, masks):
    running = carry
    out = []
    for z, mask in zip(zs, masks):
        sp = _softplus(z)
        lk = -sp
        if mask is not None:
            lk = jnp.where(mask, lk, 0.0)
        after, tot = _suffix_and_total(lk, uo)
        a = jnp.exp(z - sp + after + running)
        if mask is not None:
            a = jnp.where(mask, a, 0.0)
        out.append(a)
        running = running + tot
    return out, running


def _sb_prompt_kernel(q_ref, k_ref, v_ref, o_ref, qz_ref, carry_ref, acc_ref, *, tq, tk):
    qi = pl.program_id(2)
    q0 = qi * tq
    _pair_queries(q_ref, qz_ref)
    carry_ref[...] = jnp.zeros_like(carry_ref)
    acc_ref[...] = jnp.zeros_like(acc_ref)
    qz = qz_ref[...].reshape(4 * tq, LANES)
    uo = _suffix_matrix()
    nsub = tk // LANES

    def tile(kj, diag):
        k0 = pl.multiple_of(kj * tk, tk)
        kt = k_ref[pl.ds(k0, tk), :]
        vt = v_ref[pl.ds(k0, tk), :]
        z = _dot_nt(qz, kt)
        order = list(reversed(range(nsub)))
        masks = [None] * nsub
        if diag:
            t_idx = jnp.bitwise_and(_iota((4 * tq, LANES), 0), tq - 1)
            col = _iota((4 * tq, LANES), 1)
            masks = [k0 + u * LANES + col < q0 + t_idx for u in order]
        ws, running = _sb_blocks([z[:, u * LANES:(u + 1) * LANES] for u in order], carry_ref[...], uo, masks)
        a = ws[0] if nsub == 1 else jnp.concatenate(list(reversed(ws)), axis=1)
        acc_ref[...] = acc_ref[...] + _dot(a.astype(BF16), vt)
        carry_ref[...] = running
        return jnp.max(running)

    kd = q0 // tk
    top0 = tile(kd, True)

    def cond(st):
        return jnp.logical_and(st[0] >= 0, st[1] > EXP_ZERO)

    def body(st):
        return st[0] - 1, tile(st[0], False)

    lax.while_loop(cond, body, (kd - 1, top0))
    acc = acc_ref[...]
    _pair_output(*[acc[hh * tq:(hh + 1) * tq] for hh in range(4)], o_ref)


def _sb_prompt(p32, p16, B, S, tq, tk):
    nq = S // tq
    n_pairs = SB_KV // 2
    W = SB_HEADS * HEAD_DIM
    return pl.pallas_call(
        functools.partial(_sb_prompt_kernel, tq=tq, tk=tk),
        grid=(B, n_pairs, nq),
        in_specs=[pl.BlockSpec((tq, 2 * LANES), lambda b, g, i: (b * nq + i, C_SBQ // (2 * LANES) + g)),
                  pl.BlockSpec((S, LANES), lambda b, g, i: (b, C_SBK // LANES + g)),
                  pl.BlockSpec((S, LANES), lambda b, g, i: (b, C_SBV // LANES + g))],
        out_specs=pl.BlockSpec((tq, 2 * LANES), lambda b, g, i: (b * nq + i, g)),
        out_shape=jax.ShapeDtypeStruct((B * S, W), F32),
        scratch_shapes=[pltpu.VMEM((4, tq, LANES), BF16), pltpu.VMEM((4 * tq, LANES), F32),
                        pltpu.VMEM((4 * tq, LANES), F32)],
        compiler_params=_cparams(("parallel", "parallel", "arbitrary")),
        name="sb_prompt",
    )(p32, p16, p16)


def _nsa_group_queries(q_ref, qg_ref, rows):
    low = _iota((rows, LANES), 1) < HEAD_DIM
    rper = NSA_HEADS // NSA_KV
    for g in range(NSA_KV):
        for r in range(rper):
            h = rper * g + r
            blk = q_ref[:, (h // 2) * LANES:(h // 2 + 1) * LANES]
            if h % 2 != g:
                blk = _swap_halves(blk)
            keep = low if g == 0 else jnp.logical_not(low)
            qg_ref[g, r] = jnp.where(keep, blk, 0.0).astype(qg_ref.dtype)


def _nsa_output(tot, o_ref):
    low = _iota(tot[0][0].shape, 1) < HEAD_DIM
    for j in range(NSA_HEADS // 2):
        g = j // 2
        a, b = tot[g][2 * (j % 2)], tot[g][2 * (j % 2) + 1]
        if g == 0:
            blk = jnp.where(low, a, _swap_halves(b))
        else:
            blk = jnp.where(low, _swap_halves(a), b)
        o_ref[:, j * LANES:(j + 1) * LANES] = blk


def _topk_select_t(score_t, allowed_t, n_real, k):
    nbp = score_t.shape[0]
    sc = score_t[:n_real]
    n_idx = _iota(sc.shape, 0)
    cnt = jnp.zeros(sc.shape, F32)
    for m in range(n_real):
        row = sc[m:m + 1, :]
        cnt = cnt + jnp.where(n_idx > m, jnp.where(row >= sc, 1.0, 0.0), jnp.where(row > sc, 1.0, 0.0))
    sel = jnp.where(jnp.logical_and(cnt < k, allowed_t[:n_real]), 1.0, 0.0)
    if nbp > n_real:
        sel = jnp.concatenate([sel, jnp.zeros((nbp - n_real, sc.shape[1]), F32)], axis=0)
    return sel


def _nsa_prompt_kernel(q_ref, kc_ref, vc_ref, ks_ref, vs_ref, kw_ref, vw_ref, sm_ref, wk_ref, wv_ref,
                       e_ref, o_ref, kcb_ref, vcb_ref, qg_ref, selb_ref, m_ref, l_ref, acc_ref, tot_ref,
                       *, tq, tk, S, win):
    qi = pl.program_id(1)
    q0 = qi * tq
    nb = S // NSA_BLOCK
    nbp = kcb_ref.shape[0]
    rper = NSA_HEADS // NSA_KV
    rows = rper * tq

    @pl.when(qi == 0)
    def _():
        def pool(x_ref, w_ref):
            x = x_ref[...].astype(F32).reshape(nb, NSA_BLOCK, LANES)
            c = jnp.sum(x * w_ref[...][None], axis=1)
            if nbp > nb:
                c = jnp.concatenate([c, jnp.zeros((nbp - nb, LANES), F32)], axis=0)
            return c.astype(BF16)
        kcb_ref[...] = pool(kc_ref, wk_ref)
        vcb_ref[...] = pool(vc_ref, wv_ref)

    _nsa_group_queries(q_ref, qg_ref, tq)
    sig = jax.nn.sigmoid(sm_ref[...])

    def gate(g, branch):
        cols = [GATE_LANE0 + 3 * (rper * g + r) + branch for r in range(rper)]
        return jnp.concatenate([jnp.broadcast_to(sig[:, c:c + 1], (tq, LANES)) for c in cols], axis=0)

    def group_q(g):
        return qg_ref[g].reshape(rows, LANES)

    n_i = _iota((rows, nbp), 1)
    t_i = q0 + jnp.bitwise_and(_iota((rows, nbp), 0), tq - 1)
    blk_end = (n_i + 1) * NSA_BLOCK - 1
    dist_c = (t_i - blk_end).astype(F32)
    valid_c = blk_end <= t_i
    n_t = _iota((nbp, tq), 0)
    cur_t = lax.shift_right_logical(q0 + _iota((nbp, tq), 1), 6)
    allowed_t = n_t <= cur_t
    forced_t = jnp.logical_or(n_t == 0, jnp.logical_or(n_t == cur_t, n_t == cur_t - 1))
    for g in range(NSA_KV):
        slope_rows = _stack_rows([jnp.full((1, nbp), NSA_SLOPES[rper * g + r], F32) for r in range(rper)], tq)
        sc = _dot_nt(group_q(g), kcb_ref[...]) - slope_rows * dist_c
        sc = jnp.where(valid_c, sc, NEG)
        mx = jnp.max(sc, axis=1, keepdims=True)
        e = jnp.where(valid_c, jnp.exp(sc - mx), 0.0)
        pc = e / jnp.maximum(jnp.sum(e, axis=1, keepdims=True), jnp.finfo(F32).tiny)
        tot_ref[g] = gate(g, 0) * _dot(pc.astype(BF16), vcb_ref[...])
        imp = pc[0:tq]
        for r in range(1, rper):
            imp = imp + pc[r * tq:(r + 1) * tq]
        score_t = jnp.where(allowed_t, jnp.where(forced_t, BIG_SCORE, imp.T), -1.0)
        sel_t = _topk_select_t(score_t, allowed_t, nb, NSA_TOPK)
        selb = (_dot(sel_t.T.astype(BF16), e_ref[...]) - 1.0) * (-NEG)
        for j in range(S // tk):
            selb_ref[g, j] = selb[:, j * tk:(j + 1) * tk]

    def reset():
        m_ref[...] = jnp.full(m_ref.shape, NEG, F32)
        l_ref[...] = jnp.zeros_like(l_ref)
        acc_ref[...] = jnp.zeros_like(acc_ref)

    def tile(k_ref, v_ref, kj, use_sel, mask_kind):
        k0 = pl.multiple_of(kj * tk, tk)
        kt = k_ref[pl.ds(k0, tk), :]
        vt = v_ref[pl.ds(k0, tk), :]
        pos_rel = (k0 - q0 + _iota((1, tk), 1)).astype(F32)
        if mask_kind is not None:
            dist = (q0 - k0) + jnp.bitwise_and(_iota((rows, tk), 0), tq - 1) - _iota((rows, tk), 1)
            if mask_kind == "causal":
                mask = dist >= 0
            else:
                mask = jnp.logical_and(dist >= 0, dist < win)
        for g in range(NSA_KV):
            s = _dot_nt(group_q(g), kt) + _stack_rows([NSA_SLOPES[rper * g + r] * pos_rel for r in range(rper)], tq)
            if use_sel:
                s = s + jnp.concatenate([selb_ref[g, kj]] * rper, axis=0)
            if mask_kind is not None:
                s = jnp.where(mask, s, NEG)
            _softmax_update(s, lambda p: _dot(p, vt), m_ref, l_ref, acc_ref, g)

    def fold(branch):
        for g in range(NSA_KV):
            tot_ref[g] = tot_ref[g] + gate(g, branch) * (acc_ref[g] / l_ref[g])

    reset()

    def sel_body(kj, c):
        tile(ks_ref, vs_ref, kj, True, None)
        return c

    n_full = q0 // tk
    lax.fori_loop(0, n_full, sel_body, 0)
    tile(ks_ref, vs_ref, n_full, True, "causal")
    fold(1)

    reset()

    def win_body(kj, c):
        tile(kw_ref, vw_ref, kj, False, "window")
        return c

    lax.fori_loop(jnp.maximum(q0 - win + 1, 0) // tk, n_full + 1, win_body, 0)
    fold(2)

    _nsa_output([[tot_ref[g, r * tq:(r + 1) * tq, :] for r in range(rper)] for g in range(NSA_KV)], o_ref)


def _nsa_prompt(p32, p16, wk_exp, wv_exp, e_mat, B, S, tq, tk, win):
    nq = S // tq
    nbp = e_mat.shape[0]
    W = NSA_HEADS * HEAD_DIM
    rper = NSA_HEADS // NSA_KV
    kv = lambda c: pl.BlockSpec((S, LANES), lambda b, i: (b, C_NKV // LANES + c))
    full = lambda a: pl.BlockSpec(a.shape, lambda b, i: (0,) * a.ndim)
    return pl.pallas_call(
        functools.partial(_nsa_prompt_kernel, tq=tq, tk=tk, S=S, win=win),
        grid=(B, nq),
        in_specs=[pl.BlockSpec((tq, W), lambda b, i: (b * nq + i, C_NSAQ // W)),
                  kv(0), kv(1), kv(2), kv(3), kv(4), kv(5),
                  pl.BlockSpec((tq, LANES), lambda b, i: (b * nq + i, C_SMALL // LANES)),
                  full(wk_exp), full(wv_exp), full(e_mat)],
        out_specs=pl.BlockSpec((tq, W), lambda b, i: (b * nq + i, 0)),
        out_shape=jax.ShapeDtypeStruct((B * S, W), F32),
        scratch_shapes=[pltpu.VMEM((nbp, LANES), BF16), pltpu.VMEM((nbp, LANES), BF16),
                        pltpu.VMEM((NSA_KV, rper, tq, LANES), BF16),
                        pltpu.VMEM((NSA_KV, S // tk, tq, tk), F32),
                        pltpu.VMEM((NSA_KV, rper * tq, LANES), F32),
                        pltpu.VMEM((NSA_KV, rper * tq, LANES), F32),
                        pltpu.VMEM((NSA_KV, rper * tq, LANES), F32),
                        pltpu.VMEM((NSA_KV, rper * tq, LANES), F32)],
        compiler_params=_cparams(("parallel", "arbitrary"), 56),
        name="nsa_prompt",
    )(p32, p16, p16, p16, p16, p16, p16, p32, wk_exp, wv_exp, e_mat)


def _blockdiag_queries(q_ref, qbd_ref, n_heads, n_kv):
    sd = q_ref.shape[0]
    rper = n_heads // n_kv
    width = n_kv * HEAD_DIM
    low = _iota((sd, LANES), 1) < HEAD_DIM
    for h in range(n_heads):
        g = h // rper
        blk = q_ref[:, (h // 2) * LANES:(h // 2 + 1) * LANES]
        if h % 2 != g % 2:
            blk = _swap_halves(blk)
        keep = low if g % 2 == 0 else jnp.logical_not(low)
        blk = jnp.where(keep, blk, 0.0)
        parts = [blk if c == g // 2 else jnp.zeros((sd, LANES), F32) for c in range(width // LANES)]
        row = parts[0] if len(parts) == 1 else jnp.concatenate(parts, axis=1)
        qbd_ref[h * sd:(h + 1) * sd, :] = row.astype(BF16)


def _blockdiag_output(acc, o_ref, n_heads, n_kv, sd):
    rper = n_heads // n_kv
    low = _iota((sd, LANES), 1) < HEAD_DIM
    for j in range(n_heads // 2):
        pieces = []
        for h in (2 * j, 2 * j + 1):
            g = h // rper
            p = acc[h * sd:(h + 1) * sd, (g // 2) * LANES:(g // 2 + 1) * LANES]
            if g % 2 != h % 2:
                p = _swap_halves(p)
            pieces.append(p)
        o_ref[:, j * LANES:(j + 1) * LANES] = jnp.where(low, pieces[0], pieces[1])


def _pad_rows(x, rows):
    return jnp.concatenate([x, jnp.zeros((rows - x.shape[0], x.shape[1]), x.dtype)], axis=0)


def _page_specs(n, shape, index_of):
    return [pl.BlockSpec(shape, index_of(u)) for u in range(n)]


def _expand_heads(x8, sd):
    return jnp.concatenate([jnp.broadcast_to(x8[h:h + 1, :], (sd, x8.shape[1])) for h in range(x8.shape[0])], axis=0)


def _transposed_pool(pool):
    d, n, page, kv, hd = pool.shape
    return pool.transpose(0, 1, 3, 4, 2).reshape(d, n, kv * hd, page)


def _fox_decode_kernel(pt_ref, q_ref, kn_ref, vn_ref, sm_ref, *rest, sd, n_steps, pp):
    del pt_ref
    kt_refs, vt_refs, lf_refs = rest[:pp], rest[pp:2 * pp], rest[2 * pp:3 * pp]
    o_ref, lfo_ref, qbd_ref, m_ref, l_ref, acc_ref, cd_ref = rest[3 * pp:]
    p = pl.program_id(1)
    rows = FOX_HEADS * sd
    uo = _suffix_matrix()

    @pl.when(p == 0)
    def _():
        _blockdiag_queries(q_ref, qbd_ref, FOX_HEADS, FOX_KV)
        head_of_row = lax.shift_right_logical(_iota((rows, LANES), 0), 3)
        i_of_row = jnp.bitwise_and(_iota((rows, LANES), 0), sd - 1)
        col = _iota((rows, LANES), 1)
        lf = _log_sigmoid(sm_ref[...])
        lfo_ref[...] = lf[:, :FOX_HEADS]
        sel_h = jnp.where(col == head_of_row, 1.0, 0.0).astype(BF16)
        a = _dot_nt_exact_rhs(sel_h, _pad_rows(lf, LANES))
        incl = (_iota((LANES, LANES), 0) <= _iota((LANES, LANES), 1)).astype(BF16)
        bias_new = -_dot_exact(a, incl)
        kn = _pad_rows(kn_ref[...], LANES).astype(BF16)
        vn = _pad_rows(vn_ref[...], LANES).astype(BF16)
        s = _dot_nt(qbd_ref[...], kn) + bias_new
        s = jnp.where(col <= i_of_row, s, NEG)
        mx = jnp.max(s, axis=1, keepdims=True)
        pr = jnp.exp(s - mx)
        m_ref[...] = jnp.broadcast_to(mx, m_ref.shape)
        l_ref[...] = jnp.broadcast_to(jnp.sum(pr, axis=1, keepdims=True), l_ref.shape)
        acc_ref[...] = _dot(pr.astype(BF16), vn)
        cd_ref[...] = jnp.zeros_like(cd_ref)

    later = cd_ref[...]
    qbd = qbd_ref[...]
    parts = []
    for u in range(pp):
        r = _dot_exact(lf_refs[u][0, 0], uo)
        bias = _expand_heads(r[:, :LANES] + later, sd)
        later = later + r[:, LANES:]
        parts.append(_dot(qbd, kt_refs[u][0, 0].astype(BF16)) + bias)
    cd_ref[...] = later
    s = parts[0] if pp == 1 else jnp.concatenate(parts, axis=1)

    def pv(pr):
        out = _dot_nt(pr[:, :LANES], vt_refs[0][0, 0].astype(BF16))
        for u in range(1, pp):
            out = out + _dot_nt(pr[:, u * LANES:(u + 1) * LANES], vt_refs[u][0, 0].astype(BF16))
        return out

    _softmax_update(s, pv, m_ref, l_ref, acc_ref)

    @pl.when(p == n_steps - 1)
    def _():
        acc = acc_ref[...] / _rep(l_ref[...], acc_ref.shape[1] // LANES)
        _blockdiag_output(acc, o_ref, FOX_HEADS, FOX_KV, sd)


def _fox_decode(p32, page_table, pool_k, pool_v, pool_lf, layer, row0, DB, sd, pp):
    n_pages = page_table.shape[1]
    n_steps = n_pages // pp
    page = pool_k.shape[2]
    Wq, Wk = FOX_HEADS * HEAD_DIM, FOX_KV * HEAD_DIM
    rows = FOX_HEADS * sd
    rb = row0 // sd
    pkt, pvt = _transposed_pool(pool_k), _transposed_pool(pool_v)
    plf = pool_lf.transpose(0, 1, 3, 2)
    pg = lambda u: (lambda b, p, pt: (layer, pt[b, n_pages - 1 - (p * pp + u)], 0, 0))
    grid_spec = pltpu.PrefetchScalarGridSpec(
        num_scalar_prefetch=1,
        grid=(DB, n_steps),
        in_specs=[pl.BlockSpec((sd, Wq), lambda b, p, pt: (rb + b, C_FOXQ // Wq)),
                  pl.BlockSpec((sd, Wk), lambda b, p, pt: (rb + b, C_FOXK // Wk)),
                  pl.BlockSpec((sd, Wk), lambda b, p, pt: (rb + b, C_FOXV // Wk)),
                  pl.BlockSpec((sd, LANES), lambda b, p, pt: (rb + b, C_SMALL // LANES))]
        + _page_specs(pp, (1, 1, Wk, page), pg) + _page_specs(pp, (1, 1, Wk, page), pg)
        + _page_specs(pp, (1, 1, FOX_HEADS, page), pg),
        out_specs=[pl.BlockSpec((sd, Wq), lambda b, p, pt: (b, 0)),
                   pl.BlockSpec((sd, FOX_HEADS), lambda b, p, pt: (b, 0))],
        scratch_shapes=[pltpu.VMEM((rows, Wk), BF16), pltpu.VMEM((rows, LANES), F32),
                        pltpu.VMEM((rows, LANES), F32), pltpu.VMEM((rows, Wk), F32),
                        pltpu.VMEM((FOX_HEADS, LANES), F32)])
    return pl.pallas_call(
        functools.partial(_fox_decode_kernel, sd=sd, n_steps=n_steps, pp=pp),
        grid_spec=grid_spec,
        out_shape=[jax.ShapeDtypeStruct((DB * sd, Wq), F32), jax.ShapeDtypeStruct((DB * sd, FOX_HEADS), F32)],
        compiler_params=_cparams(("parallel", "arbitrary")),
        name="fox_decode",
    )(page_table, p32, p32, p32, p32, *([pkt] * pp), *([pvt] * pp), *([plf] * pp))


def _sb_decode_kernel(pt_ref, q_ref, kn_ref, vn_ref, *rest, sd, n_steps, pp):
    del pt_ref
    kt_refs, vt_refs = rest[:pp], rest[pp:2 * pp]
    o_ref, qbd_ref, carry_ref, acc_ref, top_ref = rest[2 * pp:]
    p = pl.program_id(1)
    rows = SB_HEADS * sd
    uo = _suffix_matrix()

    @pl.when(p == 0)
    def _():
        _blockdiag_queries(q_ref, qbd_ref, SB_HEADS, SB_KV)
        i_of_row = jnp.bitwise_and(_iota((rows, LANES), 0), sd - 1)
        strict = _iota((rows, LANES), 1) < i_of_row
        kn = _pad_rows(kn_ref[...], LANES).astype(BF16)
        vn = _pad_rows(vn_ref[...], LANES).astype(BF16)
        ws, c = _sb_blocks([_dot_nt(qbd_ref[...], kn)], jnp.zeros((rows, LANES), F32), uo, [strict])
        acc_ref[...] = _dot(ws[0].astype(BF16), vn)
        carry_ref[...] = c
        top_ref[0] = jnp.max(c)

    @pl.when(top_ref[0] > EXP_ZERO)
    def _():
        qbd = qbd_ref[...]
        zs = [_dot(qbd, kt_refs[u][0, 0].astype(BF16)) for u in range(pp)]
        ws, c = _sb_blocks(zs, carry_ref[...], uo, [None] * pp)
        acc = acc_ref[...]
        for u in range(pp):
            acc = acc + _dot_nt(ws[u].astype(BF16), vt_refs[u][0, 0].astype(BF16))
        acc_ref[...] = acc
        carry_ref[...] = c
        top_ref[0] = jnp.max(c)

    @pl.when(p == n_steps - 1)
    def _():
        _blockdiag_output(acc_ref[...], o_ref, SB_HEADS, SB_KV, sd)


def _sb_decode(p32, page_table, pool_k, pool_v, layer, row0, DB, sd, pp):
    n_pages = page_table.shape[1]
    n_steps = n_pages // pp
    page = pool_k.shape[2]
    Wq, Wk = SB_HEADS * HEAD_DIM, SB_KV * HEAD_DIM
    rows = SB_HEADS * sd
    rb = row0 // sd
    pkt, pvt = _transposed_pool(pool_k), _transposed_pool(pool_v)
    pg = lambda u: (lambda b, p, pt: (layer, pt[b, n_pages - 1 - (p * pp + u)], 0, 0))
    grid_spec = pltpu.PrefetchScalarGridSpec(
        num_scalar_prefetch=1,
        grid=(DB, n_steps),
        in_specs=[pl.BlockSpec((sd, Wq), lambda b, p, pt: (rb + b, C_SBQ // Wq)),
                  pl.BlockSpec((sd, Wk), lambda b, p, pt: (rb + b, C_SBK // Wk)),
                  pl.BlockSpec((sd, Wk), lambda b, p, pt: (rb + b, C_SBV // Wk))]
        + _page_specs(pp, (1, 1, Wk, page), pg) + _page_specs(pp, (1, 1, Wk, page), pg),
        out_specs=pl.BlockSpec((sd, Wq), lambda b, p, pt: (b, 0)),
        scratch_shapes=[pltpu.VMEM((rows, Wk), BF16), pltpu.VMEM((rows, LANES), F32),
                        pltpu.VMEM((rows, Wk), F32), pltpu.SMEM((1,), F32)])
    return pl.pallas_call(
        functools.partial(_sb_decode_kernel, sd=sd, n_steps=n_steps, pp=pp),
        grid_spec=grid_spec,
        out_shape=jax.ShapeDtypeStruct((DB * sd, Wq), F32),
        compiler_params=_cparams(("parallel", "arbitrary")),
        name="sb_decode",
    )(page_table, p32, p32, p32, *([pkt] * pp), *([pvt] * pp))


def _nsa_decode_kernel(pt_ref, q_ref, ksn_ref, vsn_ref, kwn_ref, vwn_ref, sm_ref, kws_ref, vws_ref,
                       wk_ref, wv_ref, *rest, sd, n_steps, pp, past):
    del pt_ref
    kc_refs, vc_refs = rest[:pp], rest[pp:2 * pp]
    ks_refs, vs_refs = rest[2 * pp:3 * pp], rest[3 * pp:4 * pp]
    o_ref, kcb_ref, vcb_ref, qg_ref, selr_ref, m_ref, l_ref, acc_ref, tot_ref = rest[4 * pp:]
    j = pl.program_id(1)
    rper = NSA_HEADS // NSA_KV
    rows = rper * sd
    nbc = past // NSA_BLOCK
    page = kc_refs[0].shape[3]
    bpp = page // NSA_BLOCK
    win = kws_ref.shape[3]

    i_row = jnp.bitwise_and(_iota((rows, LANES), 0), sd - 1)
    col = _iota((rows, LANES), 1)
    t_row = past + i_row

    def slopes(g, width=LANES):
        rr = lax.shift_right_logical(_iota((rows, width), 0), 3)
        s = jnp.full((rows, width), NSA_SLOPES[rper * g], F32)
        for r in range(1, rper):
            s = jnp.where(rr == r, NSA_SLOPES[rper * g + r], s)
        return s

    def group_q(g):
        return jnp.concatenate([qg_ref[g, r] for r in range(rper)], axis=0).astype(BF16)

    @pl.when(j == 0)
    def _():
        kcb_ref[...] = jnp.zeros_like(kcb_ref)
        vcb_ref[...] = jnp.zeros_like(vcb_ref)

    @pl.when(j < n_steps)
    def _():
        kacc = kcb_ref[...]
        vacc = vcb_ref[...]
        pos_blk = lax.shift_right_logical(_iota((LANES, LANES), 0), 6)
        n_col = _iota((LANES, LANES), 1)
        for u in range(pp):
            ind = (n_col == (j * pp + u) * bpp + pos_blk).astype(BF16)
            pk = kc_refs[u][0, 0] * wk_ref[...]
            pv_ = vc_refs[u][0, 0] * wv_ref[...]
            kh = pk.astype(BF16)
            vh = pv_.astype(BF16)
            stack = jnp.concatenate([kh, (pk - kh.astype(F32)).astype(BF16),
                                     vh, (pv_ - vh.astype(F32)).astype(BF16)], axis=0)
            r = _dot(stack, ind)
            kacc = kacc + r[0:LANES] + r[LANES:2 * LANES]
            vacc = vacc + r[2 * LANES:3 * LANES] + r[3 * LANES:4 * LANES]
        kcb_ref[...] = kacc
        vcb_ref[...] = vacc

    @pl.when(j == n_steps - 1)
    def _():
        _nsa_group_queries(q_ref, qg_ref, sd)
        sig = jax.nn.sigmoid(sm_ref[...])
        blk_end = (col + 1) * NSA_BLOCK - 1
        dist_c = (t_row - blk_end).astype(F32)
        valid_c = jnp.logical_and(blk_end <= t_row, col < nbc)
        kcb = kcb_ref[...].astype(BF16)
        vcb = vcb_ref[...].astype(BF16)
        n_i = _iota((sd, LANES), 1)
        allowed = n_i < nbc
        forced = jnp.logical_or(n_i == 0, n_i == nbc - 1)
        m_idx = _iota((LANES, LANES), 0)
        n_idx = _iota((LANES, LANES), 1)
        for g in range(NSA_KV):
            sc = _dot(group_q(g), kcb) - slopes(g) * dist_c
            sc = jnp.where(valid_c, sc, NEG)
            mx = jnp.max(sc, axis=1, keepdims=True)
            e = jnp.where(valid_c, jnp.exp(sc - mx), 0.0)
            pc = e / jnp.maximum(jnp.sum(e, axis=1, keepdims=True), jnp.finfo(F32).tiny)
            gate_c = jnp.concatenate(
                [jnp.broadcast_to(sig[:, GATE_LANE0 + 3 * (rper * g + r):GATE_LANE0 + 3 * (rper * g + r) + 1],
                                  (sd, LANES)) for r in range(rper)], axis=0)
            tot_ref[g] = gate_c * _dot_nt(pc.astype(BF16), vcb)
            imp = pc[0:sd]
            for r in range(1, rper):
                imp = imp + pc[r * sd:(r + 1) * sd]
            score = jnp.where(allowed, jnp.where(forced, BIG_SCORE, imp), -1.0)
            score_t = _pad_rows(score, LANES).T
            sel_rows = []
            for i in range(sd):
                cm = score_t[:, i:i + 1]
                rn = score[i:i + 1, :]
                beats = jnp.where(n_idx > m_idx, jnp.where(cm >= rn, 1.0, 0.0), jnp.where(cm > rn, 1.0, 0.0))
                cnt = jnp.sum(beats, axis=0, keepdims=True)
                sel_rows.append(jnp.where(jnp.logical_and(cnt < NSA_TOPK - 1, allowed[i:i + 1, :]), 1.0, 0.0))
            sel = jnp.concatenate(sel_rows, axis=0)
            selr_ref[g] = jnp.concatenate([sel] * rper, axis=0).astype(BF16)
        m_ref[...] = jnp.full(m_ref.shape, NEG, F32)
        l_ref[...] = jnp.zeros_like(l_ref)
        acc_ref[...] = jnp.zeros_like(acc_ref)

    @pl.when(j >= n_steps)
    def _():
        p0 = (j - n_steps) * pp
        wide = pp * LANES
        colw = _iota((rows, wide), 1)
        iw = jnp.bitwise_and(_iota((rows, wide), 0), sd - 1)
        dist = (past + iw - (p0 * page + colw)).astype(F32)
        e_p = (_iota((LANES, wide), 0) ==
               p0 * bpp + lax.shift_right_logical(_iota((LANES, wide), 1), 6)).astype(BF16)
        for g in range(NSA_KV):
            qg = group_q(g)
            qk = [_dot(qg, ks_refs[u][0, 0].astype(BF16)) for u in range(pp)]
            s = qk[0] if pp == 1 else jnp.concatenate(qk, axis=1)
            selb = (_dot(selr_ref[g], e_p) - 1.0) * (-NEG)
            s = s - slopes(g, wide) * dist + selb

            def pv(pr):
                out = _dot_nt(pr[:, :LANES], vs_refs[0][0, 0].astype(BF16))
                for u in range(1, pp):
                    out = out + _dot_nt(pr[:, u * LANES:(u + 1) * LANES], vs_refs[u][0, 0].astype(BF16))
                return out

            _softmax_update(s, pv, m_ref, l_ref, acc_ref, g)

    @pl.when(j == 2 * n_steps - 1)
    def _():
        sig = jax.nn.sigmoid(sm_ref[...])

        def gate(g, branch):
            return jnp.concatenate(
                [jnp.broadcast_to(sig[:, GATE_LANE0 + 3 * (rper * g + r) + branch:
                                      GATE_LANE0 + 3 * (rper * g + r) + branch + 1], (sd, LANES))
                 for r in range(rper)], axis=0)

        new_ok = col <= i_row
        dist_new = (i_row - col).astype(F32)
        ksn = _pad_rows(ksn_ref[...], LANES).astype(BF16)
        vsn = _pad_rows(vsn_ref[...], LANES).astype(BF16)
        kwn = _pad_rows(kwn_ref[...], LANES).astype(BF16)
        vwn = _pad_rows(vwn_ref[...], LANES).astype(BF16)
        kws = kws_ref[0, 0].astype(BF16)
        vws = vws_ref[0, 0].astype(BF16)
        colw = _iota((rows, win), 1)
        iw = jnp.bitwise_and(_iota((rows, win), 0), sd - 1)
        dist_w = (win + iw - colw).astype(F32)
        ok_w = colw > iw
        tots = []
        for g in range(NSA_KV):
            qg = group_q(g)
            sl = slopes(g)
            s = jnp.where(new_ok, _dot_nt(qg, ksn) - sl * dist_new, NEG)
            _softmax_update(s, lambda pr: _dot(pr, vsn), m_ref, l_ref, acc_ref, g)
            o_sel = acc_ref[g] / l_ref[g]
            s1 = jnp.where(ok_w, _dot(qg, kws) - slopes(g, win) * dist_w, NEG)
            s2 = jnp.where(new_ok, _dot_nt(qg, kwn) - sl * dist_new, NEG)
            mx = jnp.maximum(jnp.max(s1, axis=1, keepdims=True), jnp.max(s2, axis=1, keepdims=True))
            p1 = jnp.exp(s1 - mx)
            p2 = jnp.exp(s2 - mx)
            den = jnp.sum(p1, axis=1, keepdims=True) + jnp.sum(p2, axis=1, keepdims=True)
            o_win = (_dot_nt(p1.astype(BF16), vws) + _dot(p2.astype(BF16), vwn)) / den
            t = tot_ref[g] + gate(g, 1) * o_sel + gate(g, 2) * o_win
            tots.append([t[r * sd:(r + 1) * sd] for r in range(rper)])
        _nsa_output(tots, o_ref)


def _nsa_decode(p32, page_table, pool_kc, pool_vc, pool_ks, pool_vs, st_kw, st_vw, wk_t, wv_t,
                layer, row0, DB, sd, pp):
    n_pages = page_table.shape[1]
    n_steps = n_pages // pp
    page = pool_kc.shape[2]
    past = n_pages * page
    W = NSA_HEADS * HEAD_DIM
    rper = NSA_HEADS // NSA_KV
    rows = rper * sd
    assert past // NSA_BLOCK <= LANES and past % NSA_BLOCK == 0 and sd <= NSA_BLOCK
    rb = row0 // sd
    win = st_kw.shape[2]
    new = lambda c: pl.BlockSpec((sd, LANES), lambda b, j, pt: (rb + b, C_NKV // LANES + c))
    pg1 = lambda u: (lambda b, j, pt: (layer, pt[b, jnp.minimum(j, n_steps - 1) * pp + u], 0, 0))
    pg2 = lambda u: (lambda b, j, pt: (layer, pt[b, jnp.maximum(j - n_steps, 0) * pp + u], 0, 0))
    pages = lambda index_of: _page_specs(pp, (1, 1, LANES, page), index_of)
    state = pl.BlockSpec((1, 1, LANES, win), lambda b, j, pt: (layer, b, 0, 0))
    grid_spec = pltpu.PrefetchScalarGridSpec(
        num_scalar_prefetch=1,
        grid=(DB, 2 * n_steps),
        in_specs=[pl.BlockSpec((sd, W), lambda b, j, pt: (rb + b, C_NSAQ // W)),
                  new(2), new(3), new(4), new(5),
                  pl.BlockSpec((sd, LANES), lambda b, j, pt: (rb + b, C_SMALL // LANES)),
                  state, state,
                  pl.BlockSpec(wk_t.shape, lambda b, j, pt: (0, 0)),
                  pl.BlockSpec(wv_t.shape, lambda b, j, pt: (0, 0))]
        + pages(pg1) + pages(pg1) + pages(pg2) + pages(pg2),
        out_specs=pl.BlockSpec((sd, W), lambda b, j, pt: (b, 0)),
        scratch_shapes=[pltpu.VMEM((LANES, LANES), F32), pltpu.VMEM((LANES, LANES), F32),
                        pltpu.VMEM((NSA_KV, rper, sd, LANES), F32),
                        pltpu.VMEM((NSA_KV, rows, LANES), BF16),
                        pltpu.VMEM((NSA_KV, rows, LANES), F32), pltpu.VMEM((NSA_KV, rows, LANES), F32),
                        pltpu.VMEM((NSA_KV, rows, LANES), F32), pltpu.VMEM((NSA_KV, rows, LANES), F32)])
    tp = _transposed_pool
    return pl.pallas_call(
        functools.partial(_nsa_decode_kernel, sd=sd, n_steps=n_steps, pp=pp, past=past),
        grid_spec=grid_spec,
        out_shape=jax.ShapeDtypeStruct((DB * sd, W), F32),
        compiler_params=_cparams(("parallel", "arbitrary")),
        name="nsa_decode",
    )(page_table, p32, p32, p32, p32, p32, p32, tp(st_kw), tp(st_vw), wk_t, wv_t,
      *([tp(pool_kc)] * pp), *([tp(pool_vc)] * pp), *([tp(pool_ks)] * pp), *([tp(pool_vs)] * pp))


def _prep_in_weights(w_in, b_in):
    segs = {}
    off = 0
    for name, n in (("fox_q", 512), ("fox_k", 256), ("fox_v", 256), ("fox_f", 8), ("sb_q", 512), ("sb_k", 256),
                    ("sb_v", 256), ("nsa_q", 512), ("nsa_kv", 768), ("nsa_g", 24), ("z", 1536), ("merge", 3072)):
        segs[name] = (off, off + n)
        off += n
    assert off == w_in.shape[-1]
    scale = HEAD_DIM ** -0.5

    def build(a):
        sl = lambda nm: a[..., segs[nm][0]:segs[nm][1]]
        pad = jnp.zeros(a.shape[:-1] + (LANES - 32,), a.dtype)
        return jnp.concatenate([sl("fox_q") * scale, sl("fox_k"), sl("fox_v"), sl("sb_q") * scale, sl("sb_k"),
                                sl("sb_v"), sl("nsa_q") * scale, sl("z"), sl("merge"), sl("nsa_kv"),
                                sl("fox_f"), sl("nsa_g"), pad], axis=-1)
    wp = build(w_in).astype(BF16)
    bp = build(b_in)[:, None, :]
    assert wp.shape[-1] == N_PAD
    return wp, bp


def kernel(x_prompt, x_sample, cache_fox_k, cache_fox_v, cache_fox_logf, cache_sb_k, cache_sb_v, cache_nsa_kc, cache_nsa_vc, cache_nsa_ks, cache_nsa_vs, state_nsa_kw, state_nsa_vw, page_table, norm_g, w_in, b_in, w_cmp_k, w_cmp_v, w_br_fox, w_br_sb, w_br_nsa, w_out, final_norm_g):
    B, S, D = x_prompt.shape
    DB, SD, _ = x_sample.shape
    depth = w_in.shape[0]
    page = cache_fox_k.shape[2]
    n_pages = page_table.shape[1]
    win = state_nsa_kw.shape[2]
    tq = LANES
    tk_fox = _tile(S, 512, LANES)
    tk_sb = _tile(S, 256, LANES)
    pp = _tile(n_pages, 8, 1)
    assert cache_fox_k.shape[3:] == (FOX_KV, HEAD_DIM) and cache_fox_logf.shape[3] == FOX_HEADS
    assert cache_sb_k.shape[3:] == (SB_KV, HEAD_DIM) and cache_nsa_kc.shape[3:] == (NSA_KV, HEAD_DIM)
    assert w_br_fox.shape[1] == FOX_HEADS * HEAD_DIM and w_br_sb.shape[1] == SB_HEADS * HEAD_DIM
    assert w_br_nsa.shape[1] == NSA_HEADS * HEAD_DIM and w_cmp_k.shape[1:] == (NSA_BLOCK, NSA_KV)
    assert SD == 8 and page == LANES and S % tq == 0 and S >= win and n_pages * page >= win
    TP, TS = B * S, DB * SD
    assert TP % SD == 0

    x = jnp.concatenate([x_prompt.reshape(TP, D), x_sample.reshape(TS, D)], axis=0)
    wp, bp = _prep_in_weights(w_in, b_in)
    wbr = jnp.stack([w_br_fox, w_br_sb, w_br_nsa], axis=1).astype(BF16)
    wout = w_out.astype(BF16)
    wk_exp = jnp.repeat(w_cmp_k, HEAD_DIM, axis=2)
    wv_exp = jnp.repeat(w_cmp_v, HEAD_DIM, axis=2)
    bpp = page // NSA_BLOCK
    nbp = -(-(S // NSA_BLOCK) // LANES) * LANES
    e_mat = jnp.asarray((np.arange(nbp)[:, None] == (np.arange(S) // NSA_BLOCK)[None, :]).astype(np.float32), BF16)

    h = _rmsnorm(x, norm_g[0], BF16)
    rows_p, rows_s = [], []
    for l in range(depth):
        p32, p16 = _inproj(h, wp[l], bp[l])
        logf_p, ft = _foxgate(p32, B, S)
        ft4 = ft.reshape(B, FOX_KV // 2, 4, S // tk_fox, tk_fox).transpose(0, 1, 3, 2, 4)
        of_p = _fox_prompt(p32, p16, ft4, B, S, tq, tk_fox)
        os_p = _sb_prompt(p32, p16, B, S, tq, tk_sb)
        on_p = _nsa_prompt(p32, p16, wk_exp[l], wv_exp[l], e_mat, B, S, tq, tk_fox, win)
        of_s, logf_s = _fox_decode(p32, page_table, cache_fox_k, cache_fox_v, cache_fox_logf, l, TP, DB, SD, pp)
        os_s = _sb_decode(p32, page_table, cache_sb_k, cache_sb_v, l, TP, DB, SD, pp)
        wk_t = jnp.tile(wk_exp[l].T, (1, bpp))
        wv_t = jnp.tile(wv_exp[l].T, (1, bpp))
        on_s = _nsa_decode(p32, page_table, cache_nsa_kc, cache_nsa_vc, cache_nsa_ks, cache_nsa_vs,
                           state_nsa_kw, state_nsa_vw, wk_t, wv_t, l, TP, DB, SD, pp)
        o_fox = jnp.concatenate([of_p, of_s], axis=0)
        o_sb = jnp.concatenate([os_p, os_s], axis=0)
        o_nsa = jnp.concatenate([on_p, on_s], axis=0)
        last = l + 1 == depth
        g_next = final_norm_g if last else norm_g[l + 1]
        x, h = _merge(o_fox, o_sb, o_nsa, p32, x, wbr[l], wout[l], g_next, F32 if last else BF16)

        def seg(c0, n_kv, lo, hi, lead):
            return p32[lo:hi, c0:c0 + n_kv * HEAD_DIM].reshape(*lead, n_kv, HEAD_DIM)

        def rows(lo, hi, lead, logf):
            nk = [seg(C_NKV + c * LANES, NSA_KV, lo, hi, lead) for c in range(6)]
            return [seg(C_FOXK, FOX_KV, lo, hi, lead), seg(C_FOXV, FOX_KV, lo, hi, lead),
                    logf.reshape(*lead, FOX_HEADS),
                    seg(C_SBK, SB_KV, lo, hi, lead), seg(C_SBV, SB_KV, lo, hi, lead)] + nk

        rp = rows(0, TP, (B, S), logf_p)
        rp[9], rp[10] = rp[9][:, S - win:], rp[10][:, S - win:]
        rs = rows(TP, TP + TS, (DB, SD), logf_s)
        rs[9] = jnp.concatenate([state_nsa_kw[l], rs[9]], axis=1)[:, SD:]
        rs[10] = jnp.concatenate([state_nsa_vw[l], rs[10]], axis=1)[:, SD:]
        rows_p.append(rp)
        rows_s.append(rs)

    y_prompt = h[:TP].reshape(B, S, D)
    y_sample = h[TP:].reshape(DB, SD, D)
    outs_p = [jnp.stack([r[i] for r in rows_p]) for i in range(11)]
    outs_s = [jnp.stack([r[i] for r in rows_s]) for i in range(11)]
    return (y_prompt, y_sample, *outs_p, *outs_s)
```

```python
import functools

import numpy as np
import jax
import jax.numpy as jnp
from jax import lax
from jax.experimental import pallas as pl
from jax.experimental.pallas import tpu as pltpu

F32 = jnp.float32
BF16 = jnp.bfloat16

HEAD_DIM = 64
FOX_HEADS, FOX_KV = 8, 4
SB_HEADS, SB_KV = 8, 4
NSA_HEADS, NSA_KV = 8, 2
NSA_BLOCK = 64
NSA_TOPK = 16
RMS_EPS = 1e-6
LANES = 128
NEG = -1e30
BIG_SCORE = 8.0
EXP_ZERO = -104.0

C_FOXQ, C_FOXK, C_FOXV = 0, 512, 768
C_SBQ, C_SBK, C_SBV = 1024, 1536, 1792
C_NSAQ = 2048
C_ZF, C_ZS, C_ZN = 2560, 3072, 3584
C_MERGE = 4096
C_NKV = 7168
C_SMALL = 7936
N_PAD = 8064
GATE_LANE0 = 8
ALIBI_LANE = 64

NSA_SLOPES = tuple(float(2.0 ** (-8.0 * (i + 1) / NSA_HEADS)) for i in range(NSA_HEADS))


def _tile(n, target, mult):
    best = None
    for d in range(mult, min(n, target) + 1, mult):
        if n % d == 0:
            best = d
    return best if best is not None else n


def _cparams(sem, vmem_mb=48):
    return pltpu.CompilerParams(dimension_semantics=sem, vmem_limit_bytes=vmem_mb * 1024 * 1024)


def _dot(a, b):
    return jnp.dot(a, b, preferred_element_type=F32)


def _dot_nt(a, b):
    return lax.dot_general(a, b, (((1,), (1,)), ((), ())), preferred_element_type=F32)


def _split3(x):
    hi = x.astype(BF16)
    r = x - hi.astype(F32)
    mid = r.astype(BF16)
    lo = (r - mid.astype(F32)).astype(BF16)
    return hi, mid, lo


def _dot_exact(x, m01):
    hi, mid, lo = _split3(x)
    return _dot(hi, m01) + _dot(mid, m01) + _dot(lo, m01)


def _dot_nt_exact_rhs(m01, x):
    hi, mid, lo = _split3(x)
    return _dot_nt(m01, hi) + _dot_nt(m01, mid) + _dot_nt(m01, lo)


def _log_sigmoid(x):
    return jnp.minimum(x, 0.0) - jnp.log1p(jnp.exp(-jnp.abs(x)))


def _softplus(x):
    return jnp.maximum(x, 0.0) + jnp.log1p(jnp.exp(-jnp.abs(x)))


def _iota(shape, dim):
    return lax.broadcasted_iota(jnp.int32, shape, dim)


def _swap_halves(x):
    return pltpu.roll(x, 64, 1)


def _rmsnorm_body(x, g):
    return x * lax.rsqrt(jnp.mean(x * x, axis=-1, keepdims=True) + RMS_EPS) * g


def _rmsnorm_kernel(x_ref, g_ref, o_ref):
    o_ref[...] = _rmsnorm_body(x_ref[...], g_ref[...]).astype(o_ref.dtype)


def _rmsnorm(x, g, out_dtype):
    T, D = x.shape
    tm = _tile(T, 1024, 16)
    return pl.pallas_call(
        _rmsnorm_kernel,
        grid=(T // tm,),
        in_specs=[pl.BlockSpec((tm, D), lambda i: (i, 0)), pl.BlockSpec((1, D), lambda i: (0, 0))],
        out_specs=pl.BlockSpec((tm, D), lambda i: (i, 0)),
        out_shape=jax.ShapeDtypeStruct((T, D), out_dtype),
        compiler_params=_cparams(("parallel",)),
        name="rmsnorm",
    )(x, g.reshape(1, D))


def _inproj_kernel(h_ref, w_ref, b_ref, o32_ref, o16_ref):
    acc = _dot(h_ref[...], w_ref[...]) + b_ref[...]
    o32_ref[...] = acc
    o16_ref[...] = acc.astype(BF16)


def _inproj(h, w, b):
    T, D = h.shape
    N = w.shape[1]
    tm = _tile(T, 1280, 16)
    tn = _tile(N, 1152, 128)
    return pl.pallas_call(
        _inproj_kernel,
        grid=(N // tn, T // tm),
        in_specs=[pl.BlockSpec((tm, D), lambda j, i: (i, 0)),
                  pl.BlockSpec((D, tn), lambda j, i: (0, j)),
                  pl.BlockSpec((1, tn), lambda j, i: (0, j))],
        out_specs=[pl.BlockSpec((tm, tn), lambda j, i: (i, j)),
                   pl.BlockSpec((tm, tn), lambda j, i: (i, j))],
        out_shape=[jax.ShapeDtypeStruct((T, N), F32), jax.ShapeDtypeStruct((T, N), BF16)],
        compiler_params=_cparams(("parallel", "parallel")),
        name="inproj",
    )(h, w, b)


def _merge_kernel(of_ref, os_ref, on_ref, zf_ref, zs_ref, zn_ref, g0_ref, g1_ref, g2_ref,
                  x_ref, wbr_ref, wout_ref, gn_ref, xo_ref, ho_ref):
    def branch(o_ref, z_ref, g_ref, k):
        z = z_ref[...]
        u = (o_ref[...] * (z * jax.nn.sigmoid(z))).astype(BF16)
        return jax.nn.sigmoid(g_ref[...]) * _dot(u, wbr_ref[k])

    y = branch(of_ref, zf_ref, g0_ref, 0) + branch(os_ref, zs_ref, g1_ref, 1) + branch(on_ref, zn_ref, g2_ref, 2)
    xn = x_ref[...] + _dot(y.astype(BF16), wout_ref[...])
    xo_ref[...] = xn
    ho_ref[...] = _rmsnorm_body(xn, gn_ref[...]).astype(ho_ref.dtype)


def _merge(o_fox, o_sb, o_nsa, p32, x, wbr, wout, g_next, h_dtype):
    T, D = x.shape
    W = o_fox.shape[1]
    tm = _tile(T, 256, 16)
    row = lambda c: (lambda i: (i, c))
    return pl.pallas_call(
        _merge_kernel,
        grid=(T // tm,),
        in_specs=[pl.BlockSpec((tm, W), row(0)), pl.BlockSpec((tm, W), row(0)), pl.BlockSpec((tm, W), row(0)),
                  pl.BlockSpec((tm, W), row(C_ZF // W)), pl.BlockSpec((tm, W), row(C_ZS // W)),
                  pl.BlockSpec((tm, W), row(C_ZN // W)),
                  pl.BlockSpec((tm, D), row(C_MERGE // D)), pl.BlockSpec((tm, D), row(C_MERGE // D + 1)),
                  pl.BlockSpec((tm, D), row(C_MERGE // D + 2)),
                  pl.BlockSpec((tm, D), row(0)),
                  pl.BlockSpec((3, W, D), lambda i: (0, 0, 0)),
                  pl.BlockSpec((D, D), lambda i: (0, 0)),
                  pl.BlockSpec((1, D), lambda i: (0, 0))],
        out_specs=[pl.BlockSpec((tm, D), row(0)), pl.BlockSpec((tm, D), row(0))],
        out_shape=[jax.ShapeDtypeStruct((T, D), F32), jax.ShapeDtypeStruct((T, D), h_dtype)],
        compiler_params=_cparams(("parallel",)),
        name="merge_out",
    )(o_fox, o_sb, o_nsa, p32, p32, p32, p32, p32, p32, x, wbr, wout, g_next.reshape(1, D))


KV_LEAVES = ((C_FOXK, FOX_KV), (C_FOXV, FOX_KV), (C_SBK, SB_KV), (C_SBV, SB_KV)) + tuple(
    (C_NKV + c * LANES, NSA_KV) for c in range(6))


def _kv_rows_kernel(*refs):
    n = len(KV_LEAVES)
    for i in range(n):
        refs[2 * n + i][0, 0] = refs[i][...].T


def _kv_rows(p32, bufs, layer, B, S):
    n = len(KV_LEAVES)
    ts = _tile(S, 512, LANES)
    nst = S // ts
    widths = [kv * HEAD_DIM for _, kv in KV_LEAVES]
    return pl.pallas_call(
        _kv_rows_kernel,
        grid=(B, nst),
        in_specs=[pl.BlockSpec((ts, w), lambda b, s, c=c0 // w: (b * nst + s, c)) for (c0, _), w in zip(KV_LEAVES, widths)]
        + [pl.BlockSpec(memory_space=pl.ANY)] * n,
        out_specs=[pl.BlockSpec((1, 1, w, ts), lambda b, s: (layer, b, 0, s)) for w in widths],
        out_shape=[jax.ShapeDtypeStruct(buf.shape, F32) for buf in bufs],
        input_output_aliases={n + i: i for i in range(n)},
        compiler_params=_cparams(("parallel", "parallel")),
        name="kv_rows",
    )(*([p32] * n), *bufs)


def _foxgate_kernel(sm_ref, logf_ref, ft_ref, carry_ref, *, ts):
    @pl.when(pl.program_id(1) == 0)
    def _():
        carry_ref[...] = jnp.zeros_like(carry_ref)

    lf = _log_sigmoid(sm_ref[...])
    logf_ref[...] = lf[:, :FOX_HEADS]
    lft = lf.T[:FOX_HEADS, :]
    upper = (_iota((ts, ts), 0) <= _iota((ts, ts), 1)).astype(BF16)
    c = _dot_exact(lft, upper) + carry_ref[:, 0:1]
    ft_ref[0] = c
    carry_ref[...] = jnp.broadcast_to(c[:, ts - 1:ts], carry_ref.shape)


def _foxgate(p32, B, S):
    ts = _tile(S, 512, 128)
    nst = S // ts
    return pl.pallas_call(
        functools.partial(_foxgate_kernel, ts=ts),
        grid=(B, nst),
        in_specs=[pl.BlockSpec((ts, LANES), lambda b, s: (b * nst + s, C_SMALL // LANES))],
        out_specs=[pl.BlockSpec((ts, FOX_HEADS), lambda b, s: (b * nst + s, 0)),
                   pl.BlockSpec((1, FOX_HEADS, ts), lambda b, s: (b, 0, s))],
        out_shape=[jax.ShapeDtypeStruct((B * S, FOX_HEADS), F32), jax.ShapeDtypeStruct((B, FOX_HEADS, S), F32)],
        scratch_shapes=[pltpu.VMEM((FOX_HEADS, LANES), F32)],
        compiler_params=_cparams(("parallel", "arbitrary")),
        name="fox_gate",
    )(p32)


def _pair_queries(q_ref, qz_ref):
    tq = q_ref.shape[0]
    low = _iota((tq, LANES), 1) < HEAD_DIM
    qa, qb = q_ref[:, :LANES], q_ref[:, LANES:]
    qz_ref[0] = jnp.where(low, qa, 0.0).astype(BF16)
    qz_ref[1] = jnp.where(low, _swap_halves(qa), 0.0).astype(BF16)
    qz_ref[2] = jnp.where(low, 0.0, _swap_halves(qb)).astype(BF16)
    qz_ref[3] = jnp.where(low, 0.0, qb).astype(BF16)


def _pair_output(a0, a1, a2, a3, o_ref):
    low = _iota(a0.shape, 1) < HEAD_DIM
    o_ref[:, :LANES] = jnp.where(low, a0, _swap_halves(a1))
    o_ref[:, LANES:] = jnp.where(low, _swap_halves(a2), a3)


def _rep(x, k):
    return x if k == 1 else jnp.concatenate([x] * k, axis=1)


def _stack_rows(row_vectors, rows):
    n = row_vectors[0].shape[1]
    return jnp.concatenate([jnp.broadcast_to(v, (rows, n)) for v in row_vectors], axis=0)


def _softmax_update(s, pv_fn, m_ref, l_ref, acc_ref, idx=Ellipsis):
    m_prev = m_ref[idx]
    m_new = jnp.maximum(m_prev, jnp.max(s, axis=1, keepdims=True))
    alpha = jnp.exp(m_prev - m_new)
    p = jnp.exp(s - _rep(m_new, s.shape[1] // LANES))
    l_ref[idx] = alpha * l_ref[idx] + jnp.sum(p, axis=1, keepdims=True)
    acc = acc_ref[idx]
    acc_ref[idx] = _rep(alpha, acc.shape[1] // LANES) * acc + pv_fn(p.astype(BF16))
    m_ref[idx] = m_new


def _fox_prompt_kernel(q_ref, k_ref, v_ref, ft_ref, o_ref, qz_ref, m_ref, l_ref, acc_ref, s_ref, *, tq, tk):
    qi = pl.program_id(2)
    q0 = qi * tq
    _pair_queries(q_ref, qz_ref)
    m_ref[...] = jnp.full(m_ref.shape, NEG, F32)
    l_ref[...] = jnp.zeros_like(l_ref)
    acc_ref[...] = jnp.zeros_like(acc_ref)
    qz = qz_ref[...].reshape(4 * tq, LANES)

    def scores(kj):
        k0 = pl.multiple_of(kj * tk, tk)
        fk = ft_ref[0, 0, kj]
        return _dot_nt(qz, k_ref[pl.ds(k0, tk), :]) - _stack_rows([fk[hh:hh + 1, :] for hh in range(4)], tq)

    def consume(kj, s, diag):
        k0 = pl.multiple_of(kj * tk, tk)
        vt = v_ref[pl.ds(k0, tk), :]
        if diag:
            t_idx = jnp.bitwise_and(_iota((4 * tq, tk), 0), tq - 1)
            s = jnp.where(k0 + _iota((4 * tq, tk), 1) <= q0 + t_idx, s, NEG)
        _softmax_update(s, lambda p: _dot(p, vt), m_ref, l_ref, acc_ref)

    n_full = q0 // tk
    s_ref[...] = scores(0)

    def body(kj, carry):
        s = s_ref[...]
        s_ref[...] = scores(kj + 1)
        consume(kj, s, False)
        return carry

    lax.fori_loop(0, n_full, body, 0)
    consume(n_full, s_ref[...], True)
    out = acc_ref[...] / l_ref[...]
    _pair_output(*[out[hh * tq:(hh + 1) * tq] for hh in range(4)], o_ref)


def _fox_prompt(p32, p16, ft4, B, S, tq, tk):
    nq = S // tq
    n_pairs = FOX_KV // 2
    W = FOX_HEADS * HEAD_DIM
    return pl.pallas_call(
        functools.partial(_fox_prompt_kernel, tq=tq, tk=tk),
        grid=(B, n_pairs, nq),
        in_specs=[pl.BlockSpec((tq, 2 * LANES), lambda b, g, i: (b * nq + i, C_FOXQ // (2 * LANES) + g)),
                  pl.BlockSpec((S, LANES), lambda b, g, i: (b, C_FOXK // LANES + g)),
                  pl.BlockSpec((S, LANES), lambda b, g, i: (b, C_FOXV // LANES + g)),
                  pl.BlockSpec((1, 1, S // tk, 4, tk), lambda b, g, i: (b, g, 0, 0, 0))],
        out_specs=pl.BlockSpec((tq, 2 * LANES), lambda b, g, i: (b * nq + i, g)),
        out_shape=jax.ShapeDtypeStruct((B * S, W), F32),
        scratch_shapes=[pltpu.VMEM((4, tq, LANES), BF16), pltpu.VMEM((4 * tq, LANES), F32),
                        pltpu.VMEM((4 * tq, LANES), F32), pltpu.VMEM((4 * tq, LANES), F32),
                        pltpu.VMEM((4 * tq, tk), F32)],
        compiler_params=_cparams(("parallel", "parallel", "arbitrary")),
        name="fox_prompt",
    )(p32, p16, p16, ft4)


def _suffix_and_total(lk, uo):
    hi = lk.astype(BF16)
    lo = (lk - hi.astype(F32)).astype(BF16)
    r = _dot(hi, uo) + _dot(lo, uo)
    return r[:, :LANES], r[:, LANES:]


def _suffix_matrix():
    j = _iota((LANES, LANES), 0)
    s = _iota((LANES, LANES), 1)
    return jnp.concatenate([(j > s).astype(BF16), jnp.ones((LANES, LANES), BF16)], axis=1)


def _sb_blocks(zs, carry, uo, masks):
    running = carry
    out = []
    for z, mask in zip(zs, masks):
        sp = _softplus(z)
        lk = -sp
        if mask is not None:
            lk = jnp.where(mask, lk, 0.0)
        after, tot = _suffix_and_total(lk, uo)
        a = jnp.exp(z - sp + after + running)
        if mask is not None:
            a = jnp.where(mask, a, 0.0)
        out.append(a)
        running = running + tot
    return out, running


def _sb_prompt_kernel(q_ref, k_ref, v_ref, o_ref, qz_ref, carry_ref, acc_ref, *, tq, tk):
    qi = pl.program_id(2)
    q0 = qi * tq
    _pair_queries(q_ref, qz_ref)
    carry_ref[...] = jnp.zeros_like(carry_ref)
    acc_ref[...] = jnp.zeros_like(acc_ref)
    qz = qz_ref[...].reshape(4 * tq, LANES)
    uo = _suffix_matrix()
    nsub = tk // LANES

    def tile(kj, diag):
        k0 = pl.multiple_of(kj * tk, tk)
        kt = k_ref[pl.ds(k0, tk), :]
        vt = v_ref[pl.ds(k0, tk), :]
        z = _dot_nt(qz, kt)
        order = list(reversed(range(nsub)))
        masks = [None] * nsub
        if diag:
            t_idx = jnp.bitwise_and(_iota((4 * tq, LANES), 0), tq - 1)
            col = _iota((4 * tq, LANES), 1)
            masks = [k0 + u * LANES + col < q0 + t_idx for u in order]
        ws, running = _sb_blocks([z[:, u * LANES:(u + 1) * LANES] for u in order], carry_ref[...], uo, masks)
        a = ws[0] if nsub == 1 else jnp.concatenate(list(reversed(ws)), axis=1)
        acc_ref[...] = acc_ref[...] + _dot(a.astype(BF16), vt)
        carry_ref[...] = running
        return jnp.max(running)

    kd = q0 // tk
    top0 = tile(kd, True)

    def cond(st):
        return jnp.logical_and(st[0] >= 0, st[1] > EXP_ZERO)

    def body(st):
        return st[0] - 1, tile(st[0], False)

    lax.while_loop(cond, body, (kd - 1, top0))
    acc = acc_ref[...]
    _pair_output(*[acc[hh * tq:(hh + 1) * tq] for hh in range(4)], o_ref)


def _sb_prompt(p32, p16, B, S, tq, tk):
    nq = S // tq
    n_pairs = SB_KV // 2
    W = SB_HEADS * HEAD_DIM
    return pl.pallas_call(
        functools.partial(_sb_prompt_kernel, tq=tq, tk=tk),
        grid=(B, n_pairs, nq),
        in_specs=[pl.BlockSpec((tq, 2 * LANES), lambda b, g, i: (b * nq + i, C_SBQ // (2 * LANES) + g)),
                  pl.BlockSpec((S, LANES), lambda b, g, i: (b, C_SBK // LANES + g)),
                  pl.BlockSpec((S, LANES), lambda b, g, i: (b, C_SBV // LANES + g))],
        out_specs=pl.BlockSpec((tq, 2 * LANES), lambda b, g, i: (b * nq + i, g)),
        out_shape=jax.ShapeDtypeStruct((B * S, W), F32),
        scratch_shapes=[pltpu.VMEM((4, tq, LANES), BF16), pltpu.VMEM((4 * tq, LANES), F32),
                        pltpu.VMEM((4 * tq, LANES), F32)],
        compiler_params=_cparams(("parallel", "parallel", "arbitrary")),
        name="sb_prompt",
    )(p32, p16, p16)


def _nsa_group_queries(q_ref, qg_ref, rows):
    low = _iota((rows, LANES), 1) < HEAD_DIM
    rper = NSA_HEADS // NSA_KV
    for g in range(NSA_KV):
        for r in range(rper):
            h = rper * g + r
            blk = q_ref[:, (h // 2) * LANES:(h // 2 + 1) * LANES]
            if h % 2 != g:
                blk = _swap_halves(blk)
            keep = low if g == 0 else jnp.logical_not(low)
            qg_ref[g, r] = jnp.where(keep, blk, 0.0).astype(qg_ref.dtype)


def _nsa_output(tot, o_ref):
    low = _iota(tot[0][0].shape, 1) < HEAD_DIM
    for j in range(NSA_HEADS // 2):
        g = j // 2
        a, b = tot[g][2 * (j % 2)], tot[g][2 * (j % 2) + 1]
        if g == 0:
            blk = jnp.where(low, a, _swap_halves(b))
        else:
            blk = jnp.where(low, _swap_halves(a), b)
        o_ref[:, j * LANES:(j + 1) * LANES] = blk


def _topk_select_t(score_t, allowed_t, n_real, k):
    nbp = score_t.shape[0]
    sc = score_t[:n_real]
    n_idx = _iota(sc.shape, 0)
    cnt = jnp.zeros(sc.shape, F32)
    for m in range(n_real):
        row = sc[m:m + 1, :]
        cnt = cnt + jnp.where(n_idx > m, jnp.where(row >= sc, 1.0, 0.0), jnp.where(row > sc, 1.0, 0.0))
    sel = jnp.where(jnp.logical_and(cnt < k, allowed_t[:n_real]), 1.0, 0.0)
    if nbp > n_real:
        sel = jnp.concatenate([sel, jnp.zeros((nbp - n_real, sc.shape[1]), F32)], axis=0)
    return sel


def _nsa_key_table(S):
    assert S // NSA_BLOCK <= ALIBI_LANE
    pos = np.arange(S)
    t = np.zeros((S, LANES), np.float32)
    t[pos, pos // NSA_BLOCK] = NEG
    t[:, ALIBI_LANE] = pos // NSA_BLOCK
    t[:, ALIBI_LANE + 1] = pos % NSA_BLOCK
    return jnp.asarray(t, BF16)


def _nsa_prompt_kernel(q_ref, kc_ref, vc_ref, ks_ref, vs_ref, kw_ref, vw_ref, sm_ref, wk_ref, wv_ref,
                       et_ref, o_ref, kcb_ref, vcb_ref, qg_ref, lhs_ref, m_ref, l_ref, acc_ref, tot_ref,
                       *, tq, tk, S, win):
    qi = pl.program_id(1)
    q0 = qi * tq
    nb = S // NSA_BLOCK
    nbp = kcb_ref.shape[0]
    rper = NSA_HEADS // NSA_KV
    rows = rper * tq

    @pl.when(qi == 0)
    def _():
        def pool(x_ref, w_ref):
            x = x_ref[...].astype(F32).reshape(nb, NSA_BLOCK, LANES)
            c = jnp.sum(x * w_ref[...][None], axis=1)
            if nbp > nb:
                c = jnp.concatenate([c, jnp.zeros((nbp - nb, LANES), F32)], axis=0)
            return c.astype(BF16)
        kcb_ref[...] = pool(kc_ref, wk_ref)
        vcb_ref[...] = pool(vc_ref, wv_ref)

    _nsa_group_queries(q_ref, qg_ref, tq)
    sig = jax.nn.sigmoid(sm_ref[...])

    def gate(g, branch):
        cols = [GATE_LANE0 + 3 * (rper * g + r) + branch for r in range(rper)]
        return jnp.concatenate([jnp.broadcast_to(sig[:, c:c + 1], (tq, LANES)) for c in cols], axis=0)

    def group_q(g):
        return qg_ref[g].reshape(rows, LANES)

    lane = _iota((tq, LANES), 1)
    n_i = _iota((rows, nbp), 1)
    t_i = q0 + jnp.bitwise_and(_iota((rows, nbp), 0), tq - 1)
    blk_end = (n_i + 1) * NSA_BLOCK - 1
    dist_c = (t_i - blk_end).astype(F32)
    valid_c = blk_end <= t_i
    n_t = _iota((nbp, tq), 0)
    cur_t = lax.shift_right_logical(q0 + _iota((nbp, tq), 1), 6)
    allowed_t = n_t <= cur_t
    forced_t = jnp.logical_or(n_t == 0, jnp.logical_or(n_t == cur_t, n_t == cur_t - 1))
    for g in range(NSA_KV):
        slope_rows = _stack_rows([jnp.full((1, nbp), NSA_SLOPES[rper * g + r], F32) for r in range(rper)], tq)
        sc = _dot_nt(group_q(g), kcb_ref[...]) - slope_rows * dist_c
        sc = jnp.where(valid_c, sc, NEG)
        mx = jnp.max(sc, axis=1, keepdims=True)
        e = jnp.where(valid_c, jnp.exp(sc - mx), 0.0)
        pc = e / jnp.maximum(jnp.sum(e, axis=1, keepdims=True), jnp.finfo(F32).tiny)
        tot_ref[g] = gate(g, 0) * _dot(pc.astype(BF16), vcb_ref[...])
        imp = pc[0:tq]
        for r in range(1, rper):
            imp = imp + pc[r * tq:(r + 1) * tq]
        score_t = jnp.where(allowed_t, jnp.where(forced_t, BIG_SCORE, imp.T), -1.0)
        not_sel = 1.0 - _topk_select_t(score_t, allowed_t, nb, NSA_TOPK).T
        for r in range(rper):
            slope = NSA_SLOPES[rper * g + r]
            alibi = jnp.where(lane == ALIBI_LANE, NSA_BLOCK * slope, jnp.where(lane == ALIBI_LANE + 1, slope, 0.0))
            lhs_ref[0, g, r * tq:(r + 1) * tq, LANES:] = jnp.where(lane < ALIBI_LANE, not_sel, alibi).astype(BF16)
            lhs_ref[1, g, r * tq:(r + 1) * tq, LANES:] = alibi.astype(BF16)
        lhs_ref[0, g, :, :LANES] = group_q(g)
        lhs_ref[1, g, :, :LANES] = group_q(g)

    m_ref[...] = jnp.full(m_ref.shape, NEG, F32)
    l_ref[...] = jnp.zeros_like(l_ref)
    acc_ref[...] = jnp.zeros_like(acc_ref)

    def sel_tile(kj, diag):
        k0 = pl.multiple_of(kj * tk, tk)
        rhs = jnp.concatenate([ks_ref[pl.ds(k0, tk), :], et_ref[pl.ds(k0, tk), :]], axis=1)
        vt = vs_ref[pl.ds(k0, tk), :]
        if diag:
            causal = (q0 - k0) + jnp.bitwise_and(_iota((rows, tk), 0), tq - 1) - _iota((rows, tk), 1) >= 0
        ss = [_dot_nt(lhs_ref[0, g], rhs) for g in range(NSA_KV)]
        for g in range(NSA_KV):
            s = ss[g]
            if diag:
                s = jnp.where(causal, s, NEG)
            _softmax_update(s, lambda p: _dot(p, vt), m_ref, l_ref, acc_ref, g)

    def sel_body(kj, c):
        sel_tile(kj, False)
        return c

    n_full = q0 // tk
    lax.fori_loop(0, n_full, sel_body, 0)
    sel_tile(n_full, True)

    span = win + tq
    w0 = pl.multiple_of(jnp.maximum(q0 - win, 0), tq)
    rhs_w = jnp.concatenate([kw_ref[pl.ds(w0, span), :], et_ref[pl.ds(w0, span), :]], axis=1)
    vw = vw_ref[pl.ds(w0, span), :]
    dist = (q0 - w0) + jnp.bitwise_and(_iota((rows, span), 0), tq - 1) - _iota((rows, span), 1)
    in_win = jnp.logical_and(dist >= 0, dist < win)
    for g in range(NSA_KV):
        s = jnp.where(in_win, _dot_nt(lhs_ref[1, g], rhs_w), NEG)
        p = jnp.exp(s - jnp.max(s, axis=1, keepdims=True))
        o_win = _dot(p.astype(BF16), vw) / jnp.sum(p, axis=1, keepdims=True)
        tot_ref[g] = tot_ref[g] + gate(g, 1) * (acc_ref[g] / l_ref[g]) + gate(g, 2) * o_win

    _nsa_output([[tot_ref[g, r * tq:(r + 1) * tq, :] for r in range(rper)] for g in range(NSA_KV)], o_ref)


def _nsa_prompt(p32, p16, wk_exp, wv_exp, key_table, B, S, tq, tk, win):
    nq = S // tq
    nbp = LANES
    W = NSA_HEADS * HEAD_DIM
    rper = NSA_HEADS // NSA_KV
    assert S >= win + tq
    kv = lambda c: pl.BlockSpec((S, LANES), lambda b, i: (b, C_NKV // LANES + c))
    full = lambda a: pl.BlockSpec(a.shape, lambda b, i: (0,) * a.ndim)
    return pl.pallas_call(
        functools.partial(_nsa_prompt_kernel, tq=tq, tk=tk, S=S, win=win),
        grid=(B, nq),
        in_specs=[pl.BlockSpec((tq, W), lambda b, i: (b * nq + i, C_NSAQ // W)),
                  kv(0), kv(1), kv(2), kv(3), kv(4), kv(5),
                  pl.BlockSpec((tq, LANES), lambda b, i: (b * nq + i, C_SMALL // LANES)),
                  full(wk_exp), full(wv_exp), full(key_table)],
        out_specs=pl.BlockSpec((tq, W), lambda b, i: (b * nq + i, 0)),
        out_shape=jax.ShapeDtypeStruct((B * S, W), F32),
        scratch_shapes=[pltpu.VMEM((nbp, LANES), BF16), pltpu.VMEM((nbp, LANES), BF16),
                        pltpu.VMEM((NSA_KV, rper, tq, LANES), BF16),
                        pltpu.VMEM((2, NSA_KV, rper * tq, 2 * LANES), BF16),
                        pltpu.VMEM((NSA_KV, rper * tq, LANES), F32),
                        pltpu.VMEM((NSA_KV, rper * tq, LANES), F32),
                        pltpu.VMEM((NSA_KV, rper * tq, LANES), F32),
                        pltpu.VMEM((NSA_KV, rper * tq, LANES), F32)],
        compiler_params=_cparams(("parallel", "arbitrary"), 56),
        name="nsa_prompt",
    )(p32, p16, p16, p16, p16, p16, p16, p32, wk_exp, wv_exp, key_table)


def _blockdiag_queries(q_ref, qbd_ref, n_heads, n_kv):
    sd = q_ref.shape[0]
    rper = n_heads // n_kv
    width = n_kv * HEAD_DIM
    low = _iota((sd, LANES), 1) < HEAD_DIM
    for h in range(n_heads):
        g = h // rper
        blk = q_ref[:, (h // 2) * LANES:(h // 2 + 1) * LANES]
        if h % 2 != g % 2:
            blk = _swap_halves(blk)
        keep = low if g % 2 == 0 else jnp.logical_not(low)
        blk = jnp.where(keep, blk, 0.0)
        parts = [blk if c == g // 2 else jnp.zeros((sd, LANES), F32) for c in range(width // LANES)]
        row = parts[0] if len(parts) == 1 else jnp.concatenate(parts, axis=1)
        qbd_ref[h * sd:(h + 1) * sd, :] = row.astype(BF16)


def _blockdiag_output(acc, o_ref, n_heads, n_kv, sd):
    rper = n_heads // n_kv
    low = _iota((sd, LANES), 1) < HEAD_DIM
    for j in range(n_heads // 2):
        pieces = []
        for h in (2 * j, 2 * j + 1):
            g = h // rper
            p = acc[h * sd:(h + 1) * sd, (g // 2) * LANES:(g // 2 + 1) * LANES]
            if g % 2 != h % 2:
                p = _swap_halves(p)
            pieces.append(p)
        o_ref[:, j * LANES:(j + 1) * LANES] = jnp.where(low, pieces[0], pieces[1])


def _pad_rows(x, rows):
    return jnp.concatenate([x, jnp.zeros((rows - x.shape[0], x.shape[1]), x.dtype)], axis=0)


def _page_specs(n, shape, index_of):
    return [pl.BlockSpec(shape, index_of(u)) for u in range(n)]


def _expand_heads(x8, sd):
    return jnp.concatenate([jnp.broadcast_to(x8[h:h + 1, :], (sd, x8.shape[1])) for h in range(x8.shape[0])], axis=0)


def _transposed_pool(pool):
    d, n, page, kv, hd = pool.shape
    return pool.transpose(0, 1, 3, 4, 2).reshape(d, n, kv * hd, page)


def _fox_decode_kernel(pt_ref, q_ref, kn_ref, vn_ref, sm_ref, *rest, sd, n_steps, pp):
    del pt_ref
    kt_refs, vt_refs, lf_refs = rest[:pp], rest[pp:2 * pp], rest[2 * pp:3 * pp]
    o_ref, lfo_ref, qbd_ref, m_ref, l_ref, acc_ref, cd_ref = rest[3 * pp:]
    p = pl.program_id(1)
    rows = FOX_HEADS * sd
    uo = _suffix_matrix()

    @pl.when(p == 0)
    def _():
        _blockdiag_queries(q_ref, qbd_ref, FOX_HEADS, FOX_KV)
        head_of_row = lax.shift_right_logical(_iota((rows, LANES), 0), 3)
        i_of_row = jnp.bitwise_and(_iota((rows, LANES), 0), sd - 1)
        col = _iota((rows, LANES), 1)
        lf = _log_sigmoid(sm_ref[...])
        lfo_ref[...] = lf[:, :FOX_HEADS]
        sel_h = jnp.where(col == head_of_row, 1.0, 0.0).astype(BF16)
        a = _dot_nt_exact_rhs(sel_h, _pad_rows(lf, LANES))
        incl = (_iota((LANES, LANES), 0) <= _iota((LANES, LANES), 1)).astype(BF16)
        bias_new = -_dot_exact(a, incl)
        kn = _pad_rows(kn_ref[...], LANES).astype(BF16)
        vn = _pad_rows(vn_ref[...], LANES).astype(BF16)
        s = _dot_nt(qbd_ref[...], kn) + bias_new
        s = jnp.where(col <= i_of_row, s, NEG)
        mx = jnp.max(s, axis=1, keepdims=True)
        pr = jnp.exp(s - mx)
        m_ref[...] = jnp.broadcast_to(mx, m_ref.shape)
        l_ref[...] = jnp.broadcast_to(jnp.sum(pr, axis=1, keepdims=True), l_ref.shape)
        acc_ref[...] = _dot(pr.astype(BF16), vn)
        cd_ref[...] = jnp.zeros_like(cd_ref)

    later = cd_ref[...]
    qbd = qbd_ref[...]
    parts = []
    for u in range(pp):
        r = _dot_exact(lf_refs[u][0, 0], uo)
        bias = _expand_heads(r[:, :LANES] + later, sd)
        later = later + r[:, LANES:]
        parts.append(_dot(qbd, kt_refs[u][0, 0].astype(BF16)) + bias)
    cd_ref[...] = later
    s = parts[0] if pp == 1 else jnp.concatenate(parts, axis=1)

    def pv(pr):
        out = _dot_nt(pr[:, :LANES], vt_refs[0][0, 0].astype(BF16))
        for u in range(1, pp):
            out = out + _dot_nt(pr[:, u * LANES:(u + 1) * LANES], vt_refs[u][0, 0].astype(BF16))
        return out

    _softmax_update(s, pv, m_ref, l_ref, acc_ref)

    @pl.when(p == n_steps - 1)
    def _():
        acc = acc_ref[...] / _rep(l_ref[...], acc_ref.shape[1] // LANES)
        _blockdiag_output(acc, o_ref, FOX_HEADS, FOX_KV, sd)


def _fox_decode(p32, page_table, pool_k, pool_v, pool_lf, layer, row0, DB, sd, pp):
    n_pages = page_table.shape[1]
    n_steps = n_pages // pp
    page = pool_k.shape[2]
    Wq, Wk = FOX_HEADS * HEAD_DIM, FOX_KV * HEAD_DIM
    rows = FOX_HEADS * sd
    rb = row0 // sd
    pkt, pvt = _transposed_pool(pool_k), _transposed_pool(pool_v)
    plf = pool_lf.transpose(0, 1, 3, 2)
    pg = lambda u: (lambda b, p, pt: (layer, pt[b, n_pages - 1 - (p * pp + u)], 0, 0))
    grid_spec = pltpu.PrefetchScalarGridSpec(
        num_scalar_prefetch=1,
        grid=(DB, n_steps),
        in_specs=[pl.BlockSpec((sd, Wq), lambda b, p, pt: (rb + b, C_FOXQ // Wq)),
                  pl.BlockSpec((sd, Wk), lambda b, p, pt: (rb + b, C_FOXK // Wk)),
                  pl.BlockSpec((sd, Wk), lambda b, p, pt: (rb + b, C_FOXV // Wk)),
                  pl.BlockSpec((sd, LANES), lambda b, p, pt: (rb + b, C_SMALL // LANES))]
        + _page_specs(pp, (1, 1, Wk, page), pg) + _page_specs(pp, (1, 1, Wk, page), pg)
        + _page_specs(pp, (1, 1, FOX_HEADS, page), pg),
        out_specs=[pl.BlockSpec((sd, Wq), lambda b, p, pt: (b, 0)),
                   pl.BlockSpec((sd, FOX_HEADS), lambda b, p, pt: (b, 0))],
        scratch_shapes=[pltpu.VMEM((rows, Wk), BF16), pltpu.VMEM((rows, LANES), F32),
                        pltpu.VMEM((rows, LANES), F32), pltpu.VMEM((rows, Wk), F32),
                        pltpu.VMEM((FOX_HEADS, LANES), F32)])
    return pl.pallas_call(
        functools.partial(_fox_decode_kernel, sd=sd, n_steps=n_steps, pp=pp),
        grid_spec=grid_spec,
        out_shape=[jax.ShapeDtypeStruct((DB * sd, Wq), F32), jax.ShapeDtypeStruct((DB * sd, FOX_HEADS), F32)],
        compiler_params=_cparams(("parallel", "arbitrary")),
        name="fox_decode",
    )(page_table, p32, p32, p32, p32, *([pkt] * pp), *([pvt] * pp), *([plf] * pp))


def _sb_decode_kernel(pt_ref, q_ref, kn_ref, vn_ref, *rest, sd, n_steps, pp):
    del pt_ref
    kt_refs, vt_refs = rest[:pp], rest[pp:2 * pp]
    o_ref, qbd_ref, carry_ref, acc_ref, top_ref = rest[2 * pp:]
    p = pl.program_id(1)
    rows = SB_HEADS * sd
    uo = _suffix_matrix()

    @pl.when(p == 0)
    def _():
        _blockdiag_queries(q_ref, qbd_ref, SB_HEADS, SB_KV)
        i_of_row = jnp.bitwise_and(_iota((rows, LANES), 0), sd - 1)
        strict = _iota((rows, LANES), 1) < i_of_row
        kn = _pad_rows(kn_ref[...], LANES).astype(BF16)
        vn = _pad_rows(vn_ref[...], LANES).astype(BF16)
        ws, c = _sb_blocks([_dot_nt(qbd_ref[...], kn)], jnp.zeros((rows, LANES), F32), uo, [strict])
        acc_ref[...] = _dot(ws[0].astype(BF16), vn)
        carry_ref[...] = c
        top_ref[0] = jnp.max(c)

    @pl.when(top_ref[0] > EXP_ZERO)
    def _():
        qbd = qbd_ref[...]
        zs = [_dot(qbd, kt_refs[u][0, 0].astype(BF16)) for u in range(pp)]
        ws, c = _sb_blocks(zs, carry_ref[...], uo, [None] * pp)
        acc = acc_ref[...]
        for u in range(pp):
            acc = acc + _dot_nt(ws[u].astype(BF16), vt_refs[u][0, 0].astype(BF16))
        acc_ref[...] = acc
        carry_ref[...] = c
        top_ref[0] = jnp.max(c)

    @pl.when(p == n_steps - 1)
    def _():
        _blockdiag_output(acc_ref[...], o_ref, SB_HEADS, SB_KV, sd)


def _sb_decode(p32, page_table, pool_k, pool_v, layer, row0, DB, sd, pp):
    n_pages = page_table.shape[1]
    n_steps = n_pages // pp
    page = pool_k.shape[2]
    Wq, Wk = SB_HEADS * HEAD_DIM, SB_KV * HEAD_DIM
    rows = SB_HEADS * sd
    rb = row0 // sd
    pkt, pvt = _transposed_pool(pool_k), _transposed_pool(pool_v)
    pg = lambda u: (lambda b, p, pt: (layer, pt[b, n_pages - 1 - (p * pp + u)], 0, 0))
    grid_spec = pltpu.PrefetchScalarGridSpec(
        num_scalar_prefetch=1,
        grid=(DB, n_steps),
        in_specs=[pl.BlockSpec((sd, Wq), lambda b, p, pt: (rb + b, C_SBQ // Wq)),
                  pl.BlockSpec((sd, Wk), lambda b, p, pt: (rb + b, C_SBK // Wk)),
                  pl.BlockSpec((sd, Wk), lambda b, p, pt: (rb + b, C_SBV // Wk))]
        + _page_specs(pp, (1, 1, Wk, page), pg) + _page_specs(pp, (1, 1, Wk, page), pg),
        out_specs=pl.BlockSpec((sd, Wq), lambda b, p, pt: (b, 0)),
        scratch_shapes=[pltpu.VMEM((rows, Wk), BF16), pltpu.VMEM((rows, LANES), F32),
                        pltpu.VMEM((rows, Wk), F32), pltpu.SMEM((1,), F32)])
    return pl.pallas_call(
        functools.partial(_sb_decode_kernel, sd=sd, n_steps=n_steps, pp=pp),
        grid_spec=grid_spec,
        out_shape=jax.ShapeDtypeStruct((DB * sd, Wq), F32),
        compiler_params=_cparams(("parallel", "arbitrary")),
        name="sb_decode",
    )(page_table, p32, p32, p32, *([pkt] * pp), *([pvt] * pp))


def _nsa_decode_kernel(pt_ref, q_ref, ksn_ref, vsn_ref, kwn_ref, vwn_ref, sm_ref, kws_ref, vws_ref,
                       wk_ref, wv_ref, *rest, sd, n_steps, pp, past):
    del pt_ref
    kc_refs, vc_refs = rest[:pp], rest[pp:2 * pp]
    ks_refs, vs_refs = rest[2 * pp:3 * pp], rest[3 * pp:4 * pp]
    o_ref, kcb_ref, vcb_ref, qg_ref, selr_ref, m_ref, l_ref, acc_ref, tot_ref = rest[4 * pp:]
    j = pl.program_id(1)
    rper = NSA_HEADS // NSA_KV
    rows = rper * sd
    nbc = past // NSA_BLOCK
    page = kc_refs[0].shape[3]
    bpp = page // NSA_BLOCK
    win = kws_ref.shape[3]

    i_row = jnp.bitwise_and(_iota((rows, LANES), 0), sd - 1)
    col = _iota((rows, LANES), 1)
    t_row = past + i_row

    def slopes(g, width=LANES):
        rr = lax.shift_right_logical(_iota((rows, width), 0), 3)
        s = jnp.full((rows, width), NSA_SLOPES[rper * g], F32)
        for r in range(1, rper):
            s = jnp.where(rr == r, NSA_SLOPES[rper * g + r], s)
        return s

    def group_q(g):
        return jnp.concatenate([qg_ref[g, r] for r in range(rper)], axis=0).astype(BF16)

    @pl.when(j == 0)
    def _():
        kcb_ref[...] = jnp.zeros_like(kcb_ref)
        vcb_ref[...] = jnp.zeros_like(vcb_ref)

    @pl.when(j < n_steps)
    def _():
        kacc = kcb_ref[...]
        vacc = vcb_ref[...]
        pos_blk = lax.shift_right_logical(_iota((LANES, LANES), 0), 6)
        n_col = _iota((LANES, LANES), 1)
        for u in range(pp):
            ind = (n_col == (j * pp + u) * bpp + pos_blk).astype(BF16)
            pk = kc_refs[u][0, 0] * wk_ref[...]
            pv_ = vc_refs[u][0, 0] * wv_ref[...]
            kh = pk.astype(BF16)
            vh = pv_.astype(BF16)
            stack = jnp.concatenate([kh, (pk - kh.astype(F32)).astype(BF16),
                                     vh, (pv_ - vh.astype(F32)).astype(BF16)], axis=0)
            r = _dot(stack, ind)
            kacc = kacc + r[0:LANES] + r[LANES:2 * LANES]
            vacc = vacc + r[2 * LANES:3 * LANES] + r[3 * LANES:4 * LANES]
        kcb_ref[...] = kacc
        vcb_ref[...] = vacc

    @pl.when(j == n_steps - 1)
    def _():
        _nsa_group_queries(q_ref, qg_ref, sd)
        sig = jax.nn.sigmoid(sm_ref[...])
        blk_end = (col + 1) * NSA_BLOCK - 1
        dist_c = (t_row - blk_end).astype(F32)
        valid_c = jnp.logical_and(blk_end <= t_row, col < nbc)
        kcb = kcb_ref[...].astype(BF16)
        vcb = vcb_ref[...].astype(BF16)
        n_i = _iota((sd, LANES), 1)
        allowed = n_i < nbc
        forced = jnp.logical_or(n_i == 0, n_i == nbc - 1)
        m_idx = _iota((LANES, LANES), 0)
        n_idx = _iota((LANES, LANES), 1)
        for g in range(NSA_KV):
            sc = _dot(group_q(g), kcb) - slopes(g) * dist_c
            sc = jnp.where(valid_c, sc, NEG)
            mx = jnp.max(sc, axis=1, keepdims=True)
            e = jnp.where(valid_c, jnp.exp(sc - mx), 0.0)
            pc = e / jnp.maximum(jnp.sum(e, axis=1, keepdims=True), jnp.finfo(F32).tiny)
            gate_c = jnp.concatenate(
                [jnp.broadcast_to(sig[:, GATE_LANE0 + 3 * (rper * g + r):GATE_LANE0 + 3 * (rper * g + r) + 1],
                                  (sd, LANES)) for r in range(rper)], axis=0)
            tot_ref[g] = gate_c * _dot_nt(pc.astype(BF16), vcb)
            imp = pc[0:sd]
            for r in range(1, rper):
                imp = imp + pc[r * sd:(r + 1) * sd]
            score = jnp.where(allowed, jnp.where(forced, BIG_SCORE, imp), -1.0)
            score_t = _pad_rows(score, LANES).T
            sel_rows = []
            for i in range(sd):
                cm = score_t[:, i:i + 1]
                rn = score[i:i + 1, :]
                beats = jnp.where(n_idx > m_idx, jnp.where(cm >= rn, 1.0, 0.0), jnp.where(cm > rn, 1.0, 0.0))
                cnt = jnp.sum(beats, axis=0, keepdims=True)
                sel_rows.append(jnp.where(jnp.logical_and(cnt < NSA_TOPK - 1, allowed[i:i + 1, :]), 1.0, 0.0))
            sel = jnp.concatenate(sel_rows, axis=0)
            selr_ref[g] = jnp.concatenate([sel] * rper, axis=0).astype(BF16)
        m_ref[...] = jnp.full(m_ref.shape, NEG, F32)
        l_ref[...] = jnp.zeros_like(l_ref)
        acc_ref[...] = jnp.zeros_like(acc_ref)

    @pl.when(j >= n_steps)
    def _():
        p0 = (j - n_steps) * pp
        wide = pp * LANES
        colw = _iota((rows, wide), 1)
        iw = jnp.bitwise_and(_iota((rows, wide), 0), sd - 1)
        dist = (past + iw - (p0 * page + colw)).astype(F32)
        e_p = (_iota((LANES, wide), 0) ==
               p0 * bpp + lax.shift_right_logical(_iota((LANES, wide), 1), 6)).astype(BF16)
        for g in range(NSA_KV):
            qg = group_q(g)
            qk = [_dot(qg, ks_refs[u][0, 0].astype(BF16)) for u in range(pp)]
            s = qk[0] if pp == 1 else jnp.concatenate(qk, axis=1)
            selb = (_dot(selr_ref[g], e_p) - 1.0) * (-NEG)
            s = s - slopes(g, wide) * dist + selb

            def pv(pr):
                out = _dot_nt(pr[:, :LANES], vs_refs[0][0, 0].astype(BF16))
                for u in range(1, pp):
                    out = out + _dot_nt(pr[:, u * LANES:(u + 1) * LANES], vs_refs[u][0, 0].astype(BF16))
                return out

            _softmax_update(s, pv, m_ref, l_ref, acc_ref, g)

    @pl.when(j == 2 * n_steps - 1)
    def _():
        sig = jax.nn.sigmoid(sm_ref[...])

        def gate(g, branch):
            return jnp.concatenate(
                [jnp.broadcast_to(sig[:, GATE_LANE0 + 3 * (rper * g + r) + branch:
                                      GATE_LANE0 + 3 * (rper * g + r) + branch + 1], (sd, LANES))
                 for r in range(rper)], axis=0)

        new_ok = col <= i_row
        dist_new = (i_row - col).astype(F32)
        ksn = _pad_rows(ksn_ref[...], LANES).astype(BF16)
        vsn = _pad_rows(vsn_ref[...], LANES).astype(BF16)
        kwn = _pad_rows(kwn_ref[...], LANES).astype(BF16)
        vwn = _pad_rows(vwn_ref[...], LANES).astype(BF16)
        kws = kws_ref[0, 0].astype(BF16)
        vws = vws_ref[0, 0].astype(BF16)
        colw = _iota((rows, win), 1)
        iw = jnp.bitwise_and(_iota((rows, win), 0), sd - 1)
        dist_w = (win + iw - colw).astype(F32)
        ok_w = colw > iw
        tots = []
        for g in range(NSA_KV):
            qg = group_q(g)
            sl = slopes(g)
            s = jnp.where(new_ok, _dot_nt(qg, ksn) - sl * dist_new, NEG)
            _softmax_update(s, lambda pr: _dot(pr, vsn), m_ref, l_ref, acc_ref, g)
            o_sel = acc_ref[g] / l_ref[g]
            s1 = jnp.where(ok_w, _dot(qg, kws) - slopes(g, win) * dist_w, NEG)
            s2 = jnp.where(new_ok, _dot_nt(qg, kwn) - sl * dist_new, NEG)
            mx = jnp.maximum(jnp.max(s1, axis=1, keepdims=True), jnp.max(s2, axis=1, keepdims=True))
            p1 = jnp.exp(s1 - mx)
            p2 = jnp.exp(s2 - mx)
            den = jnp.sum(p1, axis=1, keepdims=True) + jnp.sum(p2, axis=1, keepdims=True)
            o_win = (_dot_nt(p1.astype(BF16), vws) + _dot(p2.astype(BF16), vwn)) / den
            t = tot_ref[g] + gate(g, 1) * o_sel + gate(g, 2) * o_win
            tots.append([t[r * sd:(r + 1) * sd] for r in range(rper)])
        _nsa_output(tots, o_ref)


def _nsa_decode(p32, page_table, pool_kc, pool_vc, pool_ks, pool_vs, st_kw, st_vw, wk_t, wv_t,
                layer, row0, DB, sd, pp):
    n_pages = page_table.shape[1]
    n_steps = n_pages // pp
    page = pool_kc.shape[2]
    past = n_pages * page
    W = NSA_HEADS * HEAD_DIM
    rper = NSA_HEADS // NSA_KV
    rows = rper * sd
    assert past // NSA_BLOCK <= LANES and past % NSA_BLOCK == 0 and sd <= NSA_BLOCK
    rb = row0 // sd
    win = st_kw.shape[2]
    new = lambda c: pl.BlockSpec((sd, LANES), lambda b, j, pt: (rb + b, C_NKV // LANES + c))
    pg1 = lambda u: (lambda b, j, pt: (layer, pt[b, jnp.minimum(j, n_steps - 1) * pp + u], 0, 0))
    pg2 = lambda u: (lambda b, j, pt: (layer, pt[b, jnp.maximum(j - n_steps, 0) * pp + u], 0, 0))
    pages = lambda index_of: _page_specs(pp, (1, 1, LANES, page), index_of)
    state = pl.BlockSpec((1, 1, LANES, win), lambda b, j, pt: (layer, b, 0, 0))
    grid_spec = pltpu.PrefetchScalarGridSpec(
        num_scalar_prefetch=1,
        grid=(DB, 2 * n_steps),
        in_specs=[pl.BlockSpec((sd, W), lambda b, j, pt: (rb + b, C_NSAQ // W)),
                  new(2), new(3), new(4), new(5),
                  pl.BlockSpec((sd, LANES), lambda b, j, pt: (rb + b, C_SMALL // LANES)),
                  state, state,
                  pl.BlockSpec(wk_t.shape, lambda b, j, pt: (0, 0)),
                  pl.BlockSpec(wv_t.shape, lambda b, j, pt: (0, 0))]
        + pages(pg1) + pages(pg1) + pages(pg2) + pages(pg2),
        out_specs=pl.BlockSpec((sd, W), lambda b, j, pt: (b, 0)),
        scratch_shapes=[pltpu.VMEM((LANES, LANES), F32), pltpu.VMEM((LANES, LANES), F32),
                        pltpu.VMEM((NSA_KV, rper, sd, LANES), F32),
                        pltpu.VMEM((NSA_KV, rows, LANES), BF16),
                        pltpu.VMEM((NSA_KV, rows, LANES), F32), pltpu.VMEM((NSA_KV, rows, LANES), F32),
                        pltpu.VMEM((NSA_KV, rows, LANES), F32), pltpu.VMEM((NSA_KV, rows, LANES), F32)])
    tp = _transposed_pool
    return pl.pallas_call(
        functools.partial(_nsa_decode_kernel, sd=sd, n_steps=n_steps, pp=pp, past=past),
        grid_spec=grid_spec,
        out_shape=jax.ShapeDtypeStruct((DB * sd, W), F32),
        compiler_params=_cparams(("parallel", "arbitrary")),
        name="nsa_decode",
    )(page_table, p32, p32, p32, p32, p32, p32, tp(st_kw), tp(st_vw), wk_t, wv_t,
      *([tp(pool_kc)] * pp), *([tp(pool_vc)] * pp), *([tp(pool_ks)] * pp), *([tp(pool_vs)] * pp))


def _prep_in_weights(w_in, b_in):
    segs = {}
    off = 0
    for name, n in (("fox_q", 512), ("fox_k", 256), ("fox_v", 256), ("fox_f", 8), ("sb_q", 512), ("sb_k", 256),
                    ("sb_v", 256), ("nsa_q", 512), ("nsa_kv", 768), ("nsa_g", 24), ("z", 1536), ("merge", 3072)):
        segs[name] = (off, off + n)
        off += n
    assert off == w_in.shape[-1]
    scale = HEAD_DIM ** -0.5

    def build(a):
        sl = lambda nm: a[..., segs[nm][0]:segs[nm][1]]
        pad = jnp.zeros(a.shape[:-1] + (LANES - 32,), a.dtype)
        return jnp.concatenate([sl("fox_q") * scale, sl("fox_k"), sl("fox_v"), sl("sb_q") * scale, sl("sb_k"),
                                sl("sb_v"), sl("nsa_q") * scale, sl("z"), sl("merge"), sl("nsa_kv"),
                                sl("fox_f"), sl("nsa_g"), pad], axis=-1)
    wp = build(w_in).astype(BF16)
    bp = build(b_in)[:, None, :]
    assert wp.shape[-1] == N_PAD
    return wp, bp


def kernel(x_prompt, x_sample, cache_fox_k, cache_fox_v, cache_fox_logf, cache_sb_k, cache_sb_v, cache_nsa_kc, cache_nsa_vc, cache_nsa_ks, cache_nsa_vs, state_nsa_kw, state_nsa_vw, page_table, norm_g, w_in, b_in, w_cmp_k, w_cmp_v, w_br_fox, w_br_sb, w_br_nsa, w_out, final_norm_g):
    B, S, D = x_prompt.shape
    DB, SD, _ = x_sample.shape
    depth = w_in.shape[0]
    page = cache_fox_k.shape[2]
    n_pages = page_table.shape[1]
    win = state_nsa_kw.shape[2]
    tq = LANES
    tk_fox = _tile(S, 512, LANES)
    tk_sb = _tile(S, 256, LANES)
    pp = _tile(n_pages, 8, 1)
    assert cache_fox_k.shape[3:] == (FOX_KV, HEAD_DIM) and cache_fox_logf.shape[3] == FOX_HEADS
    assert cache_sb_k.shape[3:] == (SB_KV, HEAD_DIM) and cache_nsa_kc.shape[3:] == (NSA_KV, HEAD_DIM)
    assert w_br_fox.shape[1] == FOX_HEADS * HEAD_DIM and w_br_sb.shape[1] == SB_HEADS * HEAD_DIM
    assert w_br_nsa.shape[1] == NSA_HEADS * HEAD_DIM and w_cmp_k.shape[1:] == (NSA_BLOCK, NSA_KV)
    assert SD == 8 and page == LANES and S % tq == 0 and S >= win and n_pages * page >= win
    TP, TS = B * S, DB * SD
    assert TP % SD == 0

    x = jnp.concatenate([x_prompt.reshape(TP, D), x_sample.reshape(TS, D)], axis=0)
    wp, bp = _prep_in_weights(w_in, b_in)
    wbr = jnp.stack([w_br_fox, w_br_sb, w_br_nsa], axis=1).astype(BF16)
    wout = w_out.astype(BF16)
    wk_exp = jnp.repeat(w_cmp_k, HEAD_DIM, axis=2)
    wv_exp = jnp.repeat(w_cmp_v, HEAD_DIM, axis=2)
    bpp = page // NSA_BLOCK
    key_table = _nsa_key_table(S)
    kv_bufs = [jnp.zeros((depth, B, kv * HEAD_DIM, S), F32) for _, kv in KV_LEAVES]

    h = _rmsnorm(x, norm_g[0], BF16)
    logf_p_all, rows_s = [], []
    for l in range(depth):
        p32, p16 = _inproj(h, wp[l], bp[l])
        logf_p, ft = _foxgate(p32, B, S)
        ft4 = ft.reshape(B, FOX_KV // 2, 4, S // tk_fox, tk_fox).transpose(0, 1, 3, 2, 4)
        of_p = _fox_prompt(p32, p16, ft4, B, S, tq, tk_fox)
        os_p = _sb_prompt(p32, p16, B, S, tq, tk_sb)
        on_p = _nsa_prompt(p32, p16, wk_exp[l], wv_exp[l], key_table, B, S, tq, tk_fox, win)
        kv_bufs = _kv_rows(p32, kv_bufs, l, B, S)
        logf_p_all.append(logf_p.reshape(B, S, FOX_HEADS))
        of_s, logf_s = _fox_decode(p32, page_table, cache_fox_k, cache_fox_v, cache_fox_logf, l, TP, DB, SD, pp)
        os_s = _sb_decode(p32, page_table, cache_sb_k, cache_sb_v, l, TP, DB, SD, pp)
        wk_t = jnp.tile(wk_exp[l].T, (1, bpp))
        wv_t = jnp.tile(wv_exp[l].T, (1, bpp))
        on_s = _nsa_decode(p32, page_table, cache_nsa_kc, cache_nsa_vc, cache_nsa_ks, cache_nsa_vs,
                           state_nsa_kw, state_nsa_vw, wk_t, wv_t, l, TP, DB, SD, pp)
        o_fox = jnp.concatenate([of_p, of_s], axis=0)
        o_sb = jnp.concatenate([os_p, os_s], axis=0)
        o_nsa = jnp.concatenate([on_p, on_s], axis=0)
        last = l + 1 == depth
        g_next = final_norm_g if last else norm_g[l + 1]
        x, h = _merge(o_fox, o_sb, o_nsa, p32, x, wbr[l], wout[l], g_next, F32 if last else BF16)

        rs = [p32[TP:, c0:c0 + kv * HEAD_DIM].reshape(DB, SD, kv, HEAD_DIM) for c0, kv in KV_LEAVES]
        rs[8] = jnp.concatenate([state_nsa_kw[l], rs[8]], axis=1)[:, SD:]
        rs[9] = jnp.concatenate([state_nsa_vw[l], rs[9]], axis=1)[:, SD:]
        rows_s.append(rs[:2] + [logf_s.reshape(DB, SD, FOX_HEADS)] + rs[2:])

    y_prompt = h[:TP].reshape(B, S, D)
    y_sample = h[TP:].reshape(DB, SD, D)
    kv_p = [buf.reshape(depth, B, kv, HEAD_DIM, S).transpose(0, 1, 4, 2, 3) for buf, (_, kv) in zip(kv_bufs, KV_LEAVES)]
    kv_p[8], kv_p[9] = kv_p[8][:, :, S - win:], kv_p[9][:, :, S - win:]
    outs_p = kv_p[:2] + [jnp.stack(logf_p_all)] + kv_p[2:]
    outs_s = [jnp.stack([r[i] for r in rows_s]) for i in range(11)]
    return (y_prompt, y_sample, *outs_p, *outs_s)
```

```python
import functools

import numpy as np
import jax
import jax.numpy as jnp
from jax import lax
from jax.experimental import pallas as pl
from jax.experimental.pallas import tpu as pltpu

F32 = jnp.float32
BF16 = jnp.bfloat16

HEAD_DIM = 64
FOX_HEADS, FOX_KV = 8, 4
SB_HEADS, SB_KV = 8, 4
NSA_HEADS, NSA_KV = 8, 2
NSA_BLOCK = 64
NSA_TOPK = 16
RMS_EPS = 1e-6
LANES = 128
NEG = -1e30
BIG_SCORE = 8.0
EXP_ZERO = -104.0

C_FOXQ, C_FOXK, C_FOXV = 0, 512, 768
C_SBQ, C_SBK, C_SBV = 1024, 1536, 1792
C_NSAQ = 2048
C_ZF, C_ZS, C_ZN = 2560, 3072, 3584
C_MERGE = 4096
C_NKV = 7168
C_SMALL = 7936
N_PAD = 8064
GATE_LANE0 = 8
ALIBI_LANE = 64

NSA_SLOPES = tuple(float(2.0 ** (-8.0 * (i + 1) / NSA_HEADS)) for i in range(NSA_HEADS))


def _tile(n, target, mult):
    best = None
    for d in range(mult, min(n, target) + 1, mult):
        if n % d == 0:
            best = d
    return best if best is not None else n


def _cparams(sem, vmem_mb=48):
    return pltpu.CompilerParams(dimension_semantics=sem, vmem_limit_bytes=vmem_mb * 1024 * 1024)


def _dot(a, b):
    return jnp.dot(a, b, preferred_element_type=F32)


def _dot_nt(a, b):
    return lax.dot_general(a, b, (((1,), (1,)), ((), ())), preferred_element_type=F32)


def _split3(x):
    hi = x.astype(BF16)
    r = x - hi.astype(F32)
    mid = r.astype(BF16)
    lo = (r - mid.astype(F32)).astype(BF16)
    return hi, mid, lo


def _dot_exact(x, m01):
    hi, mid, lo = _split3(x)
    return _dot(hi, m01) + _dot(mid, m01) + _dot(lo, m01)


def _dot_nt_exact_rhs(m01, x):
    hi, mid, lo = _split3(x)
    return _dot_nt(m01, hi) + _dot_nt(m01, mid) + _dot_nt(m01, lo)


def _log_sigmoid(x):
    return jnp.minimum(x, 0.0) - jnp.log1p(jnp.exp(-jnp.abs(x)))


def _softplus(x):
    return jnp.maximum(x, 0.0) + jnp.log1p(jnp.exp(-jnp.abs(x)))


def _iota(shape, dim):
    return lax.broadcasted_iota(jnp.int32, shape, dim)


def _swap_halves(x):
    return pltpu.roll(x, 64, 1)


def _rmsnorm_body(x, g):
    return x * lax.rsqrt(jnp.mean(x * x, axis=-1, keepdims=True) + RMS_EPS) * g


def _rmsnorm_kernel(x_ref, g_ref, o_ref):
    o_ref[...] = _rmsnorm_body(x_ref[...], g_ref[...]).astype(o_ref.dtype)


def _rmsnorm(x, g, out_dtype):
    T, D = x.shape
    tm = _tile(T, 1024, 16)
    return pl.pallas_call(
        _rmsnorm_kernel,
        grid=(T // tm,),
        in_specs=[pl.BlockSpec((tm, D), lambda i: (i, 0)), pl.BlockSpec((1, D), lambda i: (0, 0))],
        out_specs=pl.BlockSpec((tm, D), lambda i: (i, 0)),
        out_shape=jax.ShapeDtypeStruct((T, D), out_dtype),
        compiler_params=_cparams(("parallel",)),
        name="rmsnorm",
    )(x, g.reshape(1, D))


def _inproj_kernel(h_ref, w_ref, b_ref, o32_ref, o16_ref):
    acc = _dot(h_ref[...], w_ref[...]) + b_ref[...]
    o32_ref[...] = acc
    o16_ref[...] = acc.astype(BF16)


def _inproj(h, w, b):
    T, D = h.shape
    N = w.shape[1]
    tm = _tile(T, 1280, 16)
    tn = _tile(N, 1152, 128)
    return pl.pallas_call(
        _inproj_kernel,
        grid=(N // tn, T // tm),
        in_specs=[pl.BlockSpec((tm, D), lambda j, i: (i, 0)),
                  pl.BlockSpec((D, tn), lambda j, i: (0, j)),
                  pl.BlockSpec((1, tn), lambda j, i: (0, j))],
        out_specs=[pl.BlockSpec((tm, tn), lambda j, i: (i, j)),
                   pl.BlockSpec((tm, tn), lambda j, i: (i, j))],
        out_shape=[jax.ShapeDtypeStruct((T, N), F32), jax.ShapeDtypeStruct((T, N), BF16)],
        compiler_params=_cparams(("parallel", "parallel")),
        name="inproj",
    )(h, w, b)


def _merge_kernel(of_ref, os_ref, on_ref, zf_ref, zs_ref, zn_ref, g0_ref, g1_ref, g2_ref,
                  x_ref, wbr_ref, wout_ref, gn_ref, xo_ref, ho_ref):
    def branch(o_ref, z_ref, g_ref, k):
        z = z_ref[...]
        u = (o_ref[...] * (z * jax.nn.sigmoid(z))).astype(BF16)
        return jax.nn.sigmoid(g_ref[...]) * _dot(u, wbr_ref[k])

    y = branch(of_ref, zf_ref, g0_ref, 0) + branch(os_ref, zs_ref, g1_ref, 1) + branch(on_ref, zn_ref, g2_ref, 2)
    xn = x_ref[...] + _dot(y.astype(BF16), wout_ref[...])
    xo_ref[...] = xn
    ho_ref[...] = _rmsnorm_body(xn, gn_ref[...]).astype(ho_ref.dtype)


def _merge(o_fox, o_sb, o_nsa, p32, x, wbr, wout, g_next, h_dtype):
    T, D = x.shape
    W = o_fox.shape[1]
    tm = _tile(T, 256, 16)
    row = lambda c: (lambda i: (i, c))
    return pl.pallas_call(
        _merge_kernel,
        grid=(T // tm,),
        in_specs=[pl.BlockSpec((tm, W), row(0)), pl.BlockSpec((tm, W), row(0)), pl.BlockSpec((tm, W), row(0)),
                  pl.BlockSpec((tm, W), row(C_ZF // W)), pl.BlockSpec((tm, W), row(C_ZS // W)),
                  pl.BlockSpec((tm, W), row(C_ZN // W)),
                  pl.BlockSpec((tm, D), row(C_MERGE // D)), pl.BlockSpec((tm, D), row(C_MERGE // D + 1)),
                  pl.BlockSpec((tm, D), row(C_MERGE // D + 2)),
                  pl.BlockSpec((tm, D), row(0)),
                  pl.BlockSpec((3, W, D), lambda i: (0, 0, 0)),
                  pl.BlockSpec((D, D), lambda i: (0, 0)),
                  pl.BlockSpec((1, D), lambda i: (0, 0))],
        out_specs=[pl.BlockSpec((tm, D), row(0)), pl.BlockSpec((tm, D), row(0))],
        out_shape=[jax.ShapeDtypeStruct((T, D), F32), jax.ShapeDtypeStruct((T, D), h_dtype)],
        compiler_params=_cparams(("parallel",)),
        name="merge_out",
    )(o_fox, o_sb, o_nsa, p32, p32, p32, p32, p32, p32, x, wbr, wout, g_next.reshape(1, D))


KV_LEAVES = ((C_FOXK, FOX_KV), (C_FOXV, FOX_KV), (C_SBK, SB_KV), (C_SBV, SB_KV)) + tuple(
    (C_NKV + c * LANES, NSA_KV) for c in range(6))


def _kv_rows_kernel(*refs):
    n = len(KV_LEAVES)
    for i in range(n):
        refs[2 * n + i][0, 0] = refs[i][...].T


def _kv_rows(p32, bufs, layer, B, S):
    n = len(KV_LEAVES)
    ts = _tile(S, 512, LANES)
    nst = S // ts
    widths = [kv * HEAD_DIM for _, kv in KV_LEAVES]
    return pl.pallas_call(
        _kv_rows_kernel,
        grid=(B, nst),
        in_specs=[pl.BlockSpec((ts, w), lambda b, s, c=c0 // w: (b * nst + s, c)) for (c0, _), w in zip(KV_LEAVES, widths)]
        + [pl.BlockSpec(memory_space=pl.ANY)] * n,
        out_specs=[pl.BlockSpec((1, 1, w, ts), lambda b, s: (layer, b, 0, s)) for w in widths],
        out_shape=[jax.ShapeDtypeStruct(buf.shape, F32) for buf in bufs],
        input_output_aliases={n + i: i for i in range(n)},
        compiler_params=_cparams(("parallel", "parallel")),
        name="kv_rows",
    )(*([p32] * n), *bufs)


def _foxgate_kernel(sm_ref, logf_ref, ft_ref, carry_ref, *, ts):
    @pl.when(pl.program_id(1) == 0)
    def _():
        carry_ref[...] = jnp.zeros_like(carry_ref)

    lf = _log_sigmoid(sm_ref[...])
    logf_ref[...] = lf[:, :FOX_HEADS]
    lft = lf.T[:FOX_HEADS, :]
    upper = (_iota((ts, ts), 0) <= _iota((ts, ts), 1)).astype(BF16)
    c = _dot_exact(lft, upper) + carry_ref[:, 0:1]
    ft_ref[0] = c
    carry_ref[...] = jnp.broadcast_to(c[:, ts - 1:ts], carry_ref.shape)


def _foxgate(p32, B, S):
    ts = _tile(S, 512, 128)
    nst = S // ts
    return pl.pallas_call(
        functools.partial(_foxgate_kernel, ts=ts),
        grid=(B, nst),
        in_specs=[pl.BlockSpec((ts, LANES), lambda b, s: (b * nst + s, C_SMALL // LANES))],
        out_specs=[pl.BlockSpec((ts, FOX_HEADS), lambda b, s: (b * nst + s, 0)),
                   pl.BlockSpec((1, FOX_HEADS, ts), lambda b, s: (b, 0, s))],
        out_shape=[jax.ShapeDtypeStruct((B * S, FOX_HEADS), F32), jax.ShapeDtypeStruct((B, FOX_HEADS, S), F32)],
        scratch_shapes=[pltpu.VMEM((FOX_HEADS, LANES), F32)],
        compiler_params=_cparams(("parallel", "arbitrary")),
        name="fox_gate",
    )(p32)


def _pair_queries(q_ref, qz_ref):
    tq = q_ref.shape[0]
    low = _iota((tq, LANES), 1) < HEAD_DIM
    qa, qb = q_ref[:, :LANES], q_ref[:, LANES:]
    qz_ref[0] = jnp.where(low, qa, 0.0).astype(BF16)
    qz_ref[1] = jnp.where(low, _swap_halves(qa), 0.0).astype(BF16)
    qz_ref[2] = jnp.where(low, 0.0, _swap_halves(qb)).astype(BF16)
    qz_ref[3] = jnp.where(low, 0.0, qb).astype(BF16)


def _pair_output(a0, a1, a2, a3, o_ref):
    low = _iota(a0.shape, 1) < HEAD_DIM
    o_ref[:, :LANES] = jnp.where(low, a0, _swap_halves(a1))
    o_ref[:, LANES:] = jnp.where(low, _swap_halves(a2), a3)


def _rep(x, k):
    return x if k == 1 else jnp.concatenate([x] * k, axis=1)


def _stack_rows(row_vectors, rows):
    n = row_vectors[0].shape[1]
    return jnp.concatenate([jnp.broadcast_to(v, (rows, n)) for v in row_vectors], axis=0)


def _softmax_update(s, pv_fn, m_ref, l_ref, acc_ref, idx=Ellipsis):
    m_prev = m_ref[idx]
    m_new = jnp.maximum(m_prev, jnp.max(s, axis=1, keepdims=True))
    alpha = jnp.exp(m_prev - m_new)
    p = jnp.exp(s - _rep(m_new, s.shape[1] // LANES))
    l_ref[idx] = alpha * l_ref[idx] + jnp.sum(p, axis=1, keepdims=True)
    acc = acc_ref[idx]
    acc_ref[idx] = _rep(alpha, acc.shape[1] // LANES) * acc + pv_fn(p.astype(BF16))
    m_ref[idx] = m_new


def _fox_prompt_kernel(q_ref, k_ref, v_ref, ft_ref, o_ref, qz_ref, m_ref, l_ref, acc_ref, s_ref, *, tq, tk):
    qi = pl.program_id(2)
    q0 = qi * tq
    _pair_queries(q_ref, qz_ref)
    m_ref[...] = jnp.full(m_ref.shape, NEG, F32)
    l_ref[...] = jnp.zeros_like(l_ref)
    acc_ref[...] = jnp.zeros_like(acc_ref)
    qz = qz_ref[...].reshape(4 * tq, LANES)

    def scores(kj):
        k0 = pl.multiple_of(kj * tk, tk)
        fk = ft_ref[0, 0, kj]
        return _dot_nt(qz, k_ref[pl.ds(k0, tk), :]) - _stack_rows([fk[hh:hh + 1, :] for hh in range(4)], tq)

    def consume(kj, s, diag):
        k0 = pl.multiple_of(kj * tk, tk)
        vt = v_ref[pl.ds(k0, tk), :]
        if diag:
            t_idx = jnp.bitwise_and(_iota((4 * tq, tk), 0), tq - 1)
            s = jnp.where(k0 + _iota((4 * tq, tk), 1) <= q0 + t_idx, s, NEG)
        _softmax_update(s, lambda p: _dot(p, vt), m_ref, l_ref, acc_ref)

    n_full = q0 // tk
    s_ref[...] = scores(0)

    def body(kj, carry):
        s = s_ref[...]
        s_ref[...] = scores(kj + 1)
        consume(kj, s, False)
        return carry

    lax.fori_loop(0, n_full, body, 0)
    consume(n_full, s_ref[...], True)
    out = acc_ref[...] / l_ref[...]
    _pair_output(*[out[hh * tq:(hh + 1) * tq] for hh in range(4)], o_ref)


def _fox_prompt(p32, p16, ft4, B, S, tq, tk):
    nq = S // tq
    n_pairs = FOX_KV // 2
    W = FOX_HEADS * HEAD_DIM
    return pl.pallas_call(
        functools.partial(_fox_prompt_kernel, tq=tq, tk=tk),
        grid=(B, n_pairs, nq),
        in_specs=[pl.BlockSpec((tq, 2 * LANES), lambda b, g, i: (b * nq + i, C_FOXQ // (2 * LANES) + g)),
                  pl.BlockSpec((S, LANES), lambda b, g, i: (b, C_FOXK // LANES + g)),
                  pl.BlockSpec((S, LANES), lambda b, g, i: (b, C_FOXV // LANES + g)),
                  pl.BlockSpec((1, 1, S // tk, 4, tk), lambda b, g, i: (b, g, 0, 0, 0))],
        out_specs=pl.BlockSpec((tq, 2 * LANES), lambda b, g, i: (b * nq + i, g)),
        out_shape=jax.ShapeDtypeStruct((B * S, W), F32),
        scratch_shapes=[pltpu.VMEM((4, tq, LANES), BF16), pltpu.VMEM((4 * tq, LANES), F32),
                        pltpu.VMEM((4 * tq, LANES), F32), pltpu.VMEM((4 * tq, LANES), F32),
                        pltpu.VMEM((4 * tq, tk), F32)],
        compiler_params=_cparams(("parallel", "parallel", "arbitrary")),
        name="fox_prompt",
    )(p32, p16, p16, ft4)


def _suffix_and_total(lk, uo):
    hi = lk.astype(BF16)
    lo = (lk - hi.astype(F32)).astype(BF16)
    r = _dot(hi, uo) + _dot(lo, uo)
    return r[:, :LANES], r[:, LANES:]


def _suffix_matrix():
    j = _iota((LANES, LANES), 0)
    s = _iota((LANES, LANES), 1)
    return jnp.concatenate([(j > s).astype(BF16), jnp.ones((LANES, LANES), BF16)], axis=1)


def _sb_blocks(zs, carry, uo, masks):
    running = carry
    out = []
    for z, mask in zip(zs, masks):
        sp = _softplus(z)
        lk = -sp
        if mask is not None:
            lk = jnp.where(mask, lk, 0.0)
        after, tot = _suffix_and_total(lk, uo)
        a = jnp.exp(z - sp + after + running)
        if mask is not None:
            a = jnp.where(mask, a, 0.0)
        out.append(a)
        running = running + tot
    return out, running


def _sb_prompt_kernel(q_ref, k_ref, v_ref, o_ref, qz_ref, carry_ref, acc_ref, *, tq, tk):
    qi = pl.program_id(2)
    q0 = qi * tq
    _pair_queries(q_ref, qz_ref)
    carry_ref[...] = jnp.zeros_like(carry_ref)
    acc_ref[...] = jnp.zeros_like(acc_ref)
    qz = qz_ref[...].reshape(4 * tq, LANES)
    uo = _suffix_matrix()
    nsub = tk // LANES

    def tile(kj, diag):
        k0 = pl.multiple_of(kj * tk, tk)
        kt = k_ref[pl.ds(k0, tk), :]
        vt = v_ref[pl.ds(k0, tk), :]
        z = _dot_nt(qz, kt)
        order = list(reversed(range(nsub)))
        masks = [None] * nsub
        if diag:
            t_idx = jnp.bitwise_and(_iota((4 * tq, LANES), 0), tq - 1)
            col = _iota((4 * tq, LANES), 1)
            masks = [k0 + u * LANES + col < q0 + t_idx for u in order]
        ws, running = _sb_blocks([z[:, u * LANES:(u + 1) * LANES] for u in order], carry_ref[...], uo, masks)
        a = ws[0] if nsub == 1 else jnp.concatenate(list(reversed(ws)), axis=1)
        acc_ref[...] = acc_ref[...] + _dot(a.astype(BF16), vt)
        carry_ref[...] = running
        return jnp.max(running)

    kd = q0 // tk
    top0 = tile(kd, True)

    def cond(st):
        return jnp.logical_and(st[0] >= 0, st[1] > EXP_ZERO)

    def body(st):
        return st[0] - 1, tile(st[0], False)

    lax.while_loop(cond, body, (kd - 1, top0))
    acc = acc_ref[...]
    _pair_output(*[acc[hh * tq:(hh + 1) * tq] for hh in range(4)], o_ref)


def _sb_prompt(p32, p16, B, S, tq, tk):
    nq = S // tq
    n_pairs = SB_KV // 2
    W = SB_HEADS * HEAD_DIM
    return pl.pallas_call(
        functools.partial(_sb_prompt_kernel, tq=tq, tk=tk),
        grid=(B, n_pairs, nq),
        in_specs=[pl.BlockSpec((tq, 2 * LANES), lambda b, g, i: (b * nq + i, C_SBQ // (2 * LANES) + g)),
                  pl.BlockSpec((S, LANES), lambda b, g, i: (b, C_SBK // LANES + g)),
                  pl.BlockSpec((S, LANES), lambda b, g, i: (b, C_SBV // LANES + g))],
        out_specs=pl.BlockSpec((tq, 2 * LANES), lambda b, g, i: (b * nq + i, g)),
        out_shape=jax.ShapeDtypeStruct((B * S, W), F32),
        scratch_shapes=[pltpu.VMEM((4, tq, LANES), BF16), pltpu.VMEM((4 * tq, LANES), F32),
                        pltpu.VMEM((4 * tq, LANES), F32)],
        compiler_params=_cparams(("parallel", "parallel", "arbitrary")),
        name="sb_prompt",
    )(p32, p16, p16)


def _nsa_group_queries(q_ref, qg_ref, rows):
    low = _iota((rows, LANES), 1) < HEAD_DIM
    rper = NSA_HEADS // NSA_KV
    for g in range(NSA_KV):
        for r in range(rper):
            h = rper * g + r
            blk = q_ref[:, (h // 2) * LANES:(h // 2 + 1) * LANES]
            if h % 2 != g:
                blk = _swap_halves(blk)
            keep = low if g == 0 else jnp.logical_not(low)
            qg_ref[g, r] = jnp.where(keep, blk, 0.0).astype(qg_ref.dtype)


def _nsa_output(tot, o_ref):
    low = _iota(tot[0][0].shape, 1) < HEAD_DIM
    for j in range(NSA_HEADS // 2):
        g = j // 2
        a, b = tot[g][2 * (j % 2)], tot[g][2 * (j % 2) + 1]
        if g == 0:
            blk = jnp.where(low, a, _swap_halves(b))
        else:
            blk = jnp.where(low, _swap_halves(a), b)
        o_ref[:, j * LANES:(j + 1) * LANES] = blk


def _topk_select_t(score_t, allowed_t, n_real, k):
    nbp = score_t.shape[0]
    sc = score_t[:n_real]
    n_idx = _iota(sc.shape, 0)
    cnt = jnp.zeros(sc.shape, F32)
    for m in range(n_real):
        row = sc[m:m + 1, :]
        cnt = cnt + jnp.where(n_idx > m, jnp.where(row >= sc, 1.0, 0.0), jnp.where(row > sc, 1.0, 0.0))
    sel = jnp.where(jnp.logical_and(cnt < k, allowed_t[:n_real]), 1.0, 0.0)
    if nbp > n_real:
        sel = jnp.concatenate([sel, jnp.zeros((nbp - n_real, sc.shape[1]), F32)], axis=0)
    return sel


def _nsa_key_table(S):
    assert S // NSA_BLOCK <= ALIBI_LANE
    pos = np.arange(S)
    t = np.zeros((S, LANES), np.float32)
    t[pos, pos // NSA_BLOCK] = NEG
    t[:, ALIBI_LANE] = pos // NSA_BLOCK
    t[:, ALIBI_LANE + 1] = pos % NSA_BLOCK
    return jnp.asarray(t, BF16)


def _nsa_prompt_kernel(q_ref, kc_ref, vc_ref, ks_ref, vs_ref, kw_ref, vw_ref, sm_ref, wk_ref, wv_ref,
                       et_ref, o_ref, kcb_ref, vcb_ref, qg_ref, lhs_ref, m_ref, l_ref, acc_ref, tot_ref,
                       *, tq, tk, S, win):
    qi = pl.program_id(1)
    q0 = qi * tq
    nb = S // NSA_BLOCK
    nbp = kcb_ref.shape[0]
    rper = NSA_HEADS // NSA_KV
    rows = rper * tq

    @pl.when(qi == 0)
    def _():
        def pool(x_ref, w_ref):
            x = x_ref[...].astype(F32).reshape(nb, NSA_BLOCK, LANES)
            c = jnp.sum(x * w_ref[...][None], axis=1)
            if nbp > nb:
                c = jnp.concatenate([c, jnp.zeros((nbp - nb, LANES), F32)], axis=0)
            return c.astype(BF16)
        kcb_ref[...] = pool(kc_ref, wk_ref)
        vcb_ref[...] = pool(vc_ref, wv_ref)

    _nsa_group_queries(q_ref, qg_ref, tq)
    sig = jax.nn.sigmoid(sm_ref[...])

    def gate(g, branch):
        cols = [GATE_LANE0 + 3 * (rper * g + r) + branch for r in range(rper)]
        return jnp.concatenate([jnp.broadcast_to(sig[:, c:c + 1], (tq, LANES)) for c in cols], axis=0)

    def group_q(g):
        return qg_ref[g].reshape(rows, LANES)

    lane = _iota((tq, LANES), 1)
    n_i = _iota((rows, nbp), 1)
    t_i = q0 + jnp.bitwise_and(_iota((rows, nbp), 0), tq - 1)
    blk_end = (n_i + 1) * NSA_BLOCK - 1
    dist_c = (t_i - blk_end).astype(F32)
    valid_c = blk_end <= t_i
    n_t = _iota((nbp, tq), 0)
    cur_t = lax.shift_right_logical(q0 + _iota((nbp, tq), 1), 6)
    allowed_t = n_t <= cur_t
    forced_t = jnp.logical_or(n_t == 0, jnp.logical_or(n_t == cur_t, n_t == cur_t - 1))
    for g in range(NSA_KV):
        slope_rows = _stack_rows([jnp.full((1, nbp), NSA_SLOPES[rper * g + r], F32) for r in range(rper)], tq)
        sc = _dot_nt(group_q(g), kcb_ref[...]) - slope_rows * dist_c
        sc = jnp.where(valid_c, sc, NEG)
        mx = jnp.max(sc, axis=1, keepdims=True)
        e = jnp.where(valid_c, jnp.exp(sc - mx), 0.0)
        pc = e / jnp.maximum(jnp.sum(e, axis=1, keepdims=True), jnp.finfo(F32).tiny)
        tot_ref[g] = gate(g, 0) * _dot(pc.astype(BF16), vcb_ref[...])
        imp = pc[0:tq]
        for r in range(1, rper):
            imp = imp + pc[r * tq:(r + 1) * tq]
        score_t = jnp.where(allowed_t, jnp.where(forced_t, BIG_SCORE, imp.T), -1.0)
        not_sel = 1.0 - _topk_select_t(score_t, allowed_t, nb, NSA_TOPK).T
        for r in range(rper):
            slope = NSA_SLOPES[rper * g + r]
            alibi = jnp.where(lane == ALIBI_LANE, NSA_BLOCK * slope, jnp.where(lane == ALIBI_LANE + 1, slope, 0.0))
            lhs_ref[0, g, r * tq:(r + 1) * tq, LANES:] = jnp.where(lane < ALIBI_LANE, not_sel, alibi).astype(BF16)
            lhs_ref[1, g, r * tq:(r + 1) * tq, LANES:] = alibi.astype(BF16)
        lhs_ref[0, g, :, :LANES] = group_q(g)
        lhs_ref[1, g, :, :LANES] = group_q(g)

    m_ref[...] = jnp.full(m_ref.shape, NEG, F32)
    l_ref[...] = jnp.zeros_like(l_ref)
    acc_ref[...] = jnp.zeros_like(acc_ref)

    def sel_tile(kj, diag):
        k0 = pl.multiple_of(kj * tk, tk)
        rhs = jnp.concatenate([ks_ref[pl.ds(k0, tk), :], et_ref[pl.ds(k0, tk), :]], axis=1)
        vt = vs_ref[pl.ds(k0, tk), :]
        if diag:
            causal = (q0 - k0) + jnp.bitwise_and(_iota((rows, tk), 0), tq - 1) - _iota((rows, tk), 1) >= 0
        ss = [_dot_nt(lhs_ref[0, g], rhs) for g in range(NSA_KV)]
        for g in range(NSA_KV):
            s = ss[g]
            if diag:
                s = jnp.where(causal, s, NEG)
            _softmax_update(s, lambda p: _dot(p, vt), m_ref, l_ref, acc_ref, g)

    def sel_body(kj, c):
        sel_tile(kj, False)
        return c

    n_full = q0 // tk
    lax.fori_loop(0, n_full, sel_body, 0)
    sel_tile(n_full, True)

    span = win + tq
    w0 = pl.multiple_of(jnp.maximum(q0 - win, 0), tq)
    rhs_w = jnp.concatenate([kw_ref[pl.ds(w0, span), :], et_ref[pl.ds(w0, span), :]], axis=1)
    vw = vw_ref[pl.ds(w0, span), :]
    dist = (q0 - w0) + jnp.bitwise_and(_iota((rows, span), 0), tq - 1) - _iota((rows, span), 1)
    in_win = jnp.logical_and(dist >= 0, dist < win)
    for g in range(NSA_KV):
        s = jnp.where(in_win, _dot_nt(lhs_ref[1, g], rhs_w), NEG)
        p = jnp.exp(s - jnp.max(s, axis=1, keepdims=True))
        o_win = _dot(p.astype(BF16), vw) / jnp.sum(p, axis=1, keepdims=True)
        tot_ref[g] = tot_ref[g] + gate(g, 1) * (acc_ref[g] / l_ref[g]) + gate(g, 2) * o_win

    _nsa_output([[tot_ref[g, r * tq:(r + 1) * tq, :] for r in range(rper)] for g in range(NSA_KV)], o_ref)


def _nsa_prompt(p32, p16, wk_exp, wv_exp, key_table, B, S, tq, tk, win):
    nq = S // tq
    nbp = LANES
    W = NSA_HEADS * HEAD_DIM
    rper = NSA_HEADS // NSA_KV
    assert S >= win + tq
    kv = lambda c: pl.BlockSpec((S, LANES), lambda b, i: (b, C_NKV // LANES + c))
    full = lambda a: pl.BlockSpec(a.shape, lambda b, i: (0,) * a.ndim)
    return pl.pallas_call(
        functools.partial(_nsa_prompt_kernel, tq=tq, tk=tk, S=S, win=win),
        grid=(B, nq),
        in_specs=[pl.BlockSpec((tq, W), lambda b, i: (b * nq + i, C_NSAQ // W)),
                  kv(0), kv(1), kv(2), kv(3), kv(4), kv(5),
                  pl.BlockSpec((tq, LANES), lambda b, i: (b * nq + i, C_SMALL // LANES)),
                  full(wk_exp), full(wv_exp), full(key_table)],
        out_specs=pl.BlockSpec((tq, W), lambda b, i: (b * nq + i, 0)),
        out_shape=jax.ShapeDtypeStruct((B * S, W), F32),
        scratch_shapes=[pltpu.VMEM((nbp, LANES), BF16), pltpu.VMEM((nbp, LANES), BF16),
                        pltpu.VMEM((NSA_KV, rper, tq, LANES), BF16),
                        pltpu.VMEM((2, NSA_KV, rper * tq, 2 * LANES), BF16),
                        pltpu.VMEM((NSA_KV, rper * tq, LANES), F32),
                        pltpu.VMEM((NSA_KV, rper * tq, LANES), F32),
                        pltpu.VMEM((NSA_KV, rper * tq, LANES), F32),
                        pltpu.VMEM((NSA_KV, rper * tq, LANES), F32)],
        compiler_params=_cparams(("parallel", "arbitrary"), 56),
        name="nsa_prompt",
    )(p32, p16, p16, p16, p16, p16, p16, p32, wk_exp, wv_exp, key_table)


def _blockdiag_queries(q_ref, qbd_ref, n_heads, n_kv):
    sd = q_ref.shape[0]
    rper = n_heads // n_kv
    width = n_kv * HEAD_DIM
    low = _iota((sd, LANES), 1) < HEAD_DIM
    for h in range(n_heads):
        g = h // rper
        blk = q_ref[:, (h // 2) * LANES:(h // 2 + 1) * LANES]
        if h % 2 != g % 2:
            blk = _swap_halves(blk)
        keep = low if g % 2 == 0 else jnp.logical_not(low)
        blk = jnp.where(keep, blk, 0.0)
        parts = [blk if c == g // 2 else jnp.zeros((sd, LANES), F32) for c in range(width // LANES)]
        row = parts[0] if len(parts) == 1 else jnp.concatenate(parts, axis=1)
        qbd_ref[h * sd:(h + 1) * sd, :] = row.astype(BF16)


def _blockdiag_output(acc, o_ref, n_heads, n_kv, sd):
    rper = n_heads // n_kv
    low = _iota((sd, LANES), 1) < HEAD_DIM
    for j in range(n_heads // 2):
        pieces = []
        for h in (2 * j, 2 * j + 1):
            g = h // rper
            p = acc[h * sd:(h + 1) * sd, (g // 2) * LANES:(g // 2 + 1) * LANES]
            if g % 2 != h % 2:
                p = _swap_halves(p)
            pieces.append(p)
        o_ref[:, j * LANES:(j + 1) * LANES] = jnp.where(low, pieces[0], pieces[1])


def _pad_rows(x, rows):
    return jnp.concatenate([x, jnp.zeros((rows - x.shape[0], x.shape[1]), x.dtype)], axis=0)


def _page_specs(n, shape, index_of):
    return [pl.BlockSpec(shape, index_of(u)) for u in range(n)]


def _expand_heads(x8, sd):
    return jnp.concatenate([jnp.broadcast_to(x8[h:h + 1, :], (sd, x8.shape[1])) for h in range(x8.shape[0])], axis=0)


def _transposed_pool(pool):
    d, n, page, kv, hd = pool.shape
    return pool.transpose(0, 1, 3, 4, 2).reshape(d, n, kv * hd, page)


def _fox_decode_kernel(pt_ref, q_ref, kn_ref, vn_ref, sm_ref, *rest, sd, n_steps, pp):
    del pt_ref
    kt_refs, vt_refs, lf_refs = rest[:pp], rest[pp:2 * pp], rest[2 * pp:3 * pp]
    o_ref, lfo_ref, qbd_ref, m_ref, l_ref, acc_ref, cd_ref = rest[3 * pp:]
    p = pl.program_id(1)
    rows = FOX_HEADS * sd
    uo = _suffix_matrix()

    @pl.when(p == 0)
    def _():
        _blockdiag_queries(q_ref, qbd_ref, FOX_HEADS, FOX_KV)
        head_of_row = lax.shift_right_logical(_iota((rows, LANES), 0), 3)
        i_of_row = jnp.bitwise_and(_iota((rows, LANES), 0), sd - 1)
        col = _iota((rows, LANES), 1)
        lf = _log_sigmoid(sm_ref[...])
        lfo_ref[...] = lf[:, :FOX_HEADS]
        sel_h = jnp.where(col == head_of_row, 1.0, 0.0).astype(BF16)
        a = _dot_nt_exact_rhs(sel_h, _pad_rows(lf, LANES))
        incl = (_iota((LANES, LANES), 0) <= _iota((LANES, LANES), 1)).astype(BF16)
        bias_new = -_dot_exact(a, incl)
        kn = _pad_rows(kn_ref[...], LANES).astype(BF16)
        vn = _pad_rows(vn_ref[...], LANES).astype(BF16)
        s = _dot_nt(qbd_ref[...], kn) + bias_new
        s = jnp.where(col <= i_of_row, s, NEG)
        mx = jnp.max(s, axis=1, keepdims=True)
        pr = jnp.exp(s - mx)
        m_ref[...] = jnp.broadcast_to(mx, m_ref.shape)
        l_ref[...] = jnp.broadcast_to(jnp.sum(pr, axis=1, keepdims=True), l_ref.shape)
        acc_ref[...] = _dot(pr.astype(BF16), vn)
        cd_ref[...] = jnp.zeros_like(cd_ref)

    later = cd_ref[...]
    qbd = qbd_ref[...]
    parts = []
    for u in range(pp):
        r = _dot_exact(lf_refs[u][0, 0], uo)
        bias = _expand_heads(r[:, :LANES] + later, sd)
        later = later + r[:, LANES:]
        parts.append(_dot(qbd, kt_refs[u][0, 0].astype(BF16)) + bias)
    cd_ref[...] = later
    s = parts[0] if pp == 1 else jnp.concatenate(parts, axis=1)

    def pv(pr):
        out = _dot_nt(pr[:, :LANES], vt_refs[0][0, 0].astype(BF16))
        for u in range(1, pp):
            out = out + _dot_nt(pr[:, u * LANES:(u + 1) * LANES], vt_refs[u][0, 0].astype(BF16))
        return out

    _softmax_update(s, pv, m_ref, l_ref, acc_ref)

    @pl.when(p == n_steps - 1)
    def _():
        acc = acc_ref[...] / _rep(l_ref[...], acc_ref.shape[1] // LANES)
        _blockdiag_output(acc, o_ref, FOX_HEADS, FOX_KV, sd)


def _fox_decode(p32, page_table, pool_k, pool_v, pool_lf, layer, row0, DB, sd, pp):
    n_pages = page_table.shape[1]
    n_steps = n_pages // pp
    page = pool_k.shape[2]
    Wq, Wk = FOX_HEADS * HEAD_DIM, FOX_KV * HEAD_DIM
    rows = FOX_HEADS * sd
    rb = row0 // sd
    pkt, pvt = _transposed_pool(pool_k), _transposed_pool(pool_v)
    plf = pool_lf.transpose(0, 1, 3, 2)
    pg = lambda u: (lambda b, p, pt: (layer, pt[b, n_pages - 1 - (p * pp + u)], 0, 0))
    grid_spec = pltpu.PrefetchScalarGridSpec(
        num_scalar_prefetch=1,
        grid=(DB, n_steps),
        in_specs=[pl.BlockSpec((sd, Wq), lambda b, p, pt: (rb + b, C_FOXQ // Wq)),
                  pl.BlockSpec((sd, Wk), lambda b, p, pt: (rb + b, C_FOXK // Wk)),
                  pl.BlockSpec((sd, Wk), lambda b, p, pt: (rb + b, C_FOXV // Wk)),
                  pl.BlockSpec((sd, LANES), lambda b, p, pt: (rb + b, C_SMALL // LANES))]
        + _page_specs(pp, (1, 1, Wk, page), pg) + _page_specs(pp, (1, 1, Wk, page), pg)
        + _page_specs(pp, (1, 1, FOX_HEADS, page), pg),
        out_specs=[pl.BlockSpec((sd, Wq), lambda b, p, pt: (b, 0)),
                   pl.BlockSpec((sd, FOX_HEADS), lambda b, p, pt: (b, 0))],
        scratch_shapes=[pltpu.VMEM((rows, Wk), BF16), pltpu.VMEM((rows, LANES), F32),
                        pltpu.VMEM((rows, LANES), F32), pltpu.VMEM((rows, Wk), F32),
                        pltpu.VMEM((FOX_HEADS, LANES), F32)])
    return pl.pallas_call(
        functools.partial(_fox_decode_kernel, sd=sd, n_steps=n_steps, pp=pp),
        grid_spec=grid_spec,
        out_shape=[jax.ShapeDtypeStruct((DB * sd, Wq), F32), jax.ShapeDtypeStruct((DB * sd, FOX_HEADS), F32)],
        compiler_params=_cparams(("parallel", "arbitrary")),
        name="fox_decode",
    )(page_table, p32, p32, p32, p32, *([pkt] * pp), *([pvt] * pp), *([plf] * pp))


def _sb_decode_kernel(pt_ref, q_ref, kn_ref, vn_ref, *rest, sd, n_steps, pp):
    del pt_ref
    kt_refs, vt_refs = rest[:pp], rest[pp:2 * pp]
    o_ref, qbd_ref, carry_ref, acc_ref, top_ref = rest[2 * pp:]
    p = pl.program_id(1)
    rows = SB_HEADS * sd
    uo = _suffix_matrix()

    @pl.when(p == 0)
    def _():
        _blockdiag_queries(q_ref, qbd_ref, SB_HEADS, SB_KV)
        i_of_row = jnp.bitwise_and(_iota((rows, LANES), 0), sd - 1)
        strict = _iota((rows, LANES), 1) < i_of_row
        kn = _pad_rows(kn_ref[...], LANES).astype(BF16)
        vn = _pad_rows(vn_ref[...], LANES).astype(BF16)
        ws, c = _sb_blocks([_dot_nt(qbd_ref[...], kn)], jnp.zeros((rows, LANES), F32), uo, [strict])
        acc_ref[...] = _dot(ws[0].astype(BF16), vn)
        carry_ref[...] = c
        top_ref[0] = jnp.max(c)

    @pl.when(top_ref[0] > EXP_ZERO)
    def _():
        qbd = qbd_ref[...]
        zs = [_dot(qbd, kt_refs[u][0, 0].astype(BF16)) for u in range(pp)]
        ws, c = _sb_blocks(zs, carry_ref[...], uo, [None] * pp)
        acc = acc_ref[...]
        for u in range(pp):
            acc = acc + _dot_nt(ws[u].astype(BF16), vt_refs[u][0, 0].astype(BF16))
        acc_ref[...] = acc
        carry_ref[...] = c
        top_ref[0] = jnp.max(c)

    @pl.when(p == n_steps - 1)
    def _():
        _blockdiag_output(acc_ref[...], o_ref, SB_HEADS, SB_KV, sd)


def _sb_decode(p32, page_table, pool_k, pool_v, layer, row0, DB, sd, pp):
    n_pages = page_table.shape[1]
    n_steps = n_pages // pp
    page = pool_k.shape[2]
    Wq, Wk = SB_HEADS * HEAD_DIM, SB_KV * HEAD_DIM
    rows = SB_HEADS * sd
    rb = row0 // sd
    pkt, pvt = _transposed_pool(pool_k), _transposed_pool(pool_v)
    pg = lambda u: (lambda b, p, pt: (layer, pt[b, n_pages - 1 - (p * pp + u)], 0, 0))
    grid_spec = pltpu.PrefetchScalarGridSpec(
        num_scalar_prefetch=1,
        grid=(DB, n_steps),
        in_specs=[pl.BlockSpec((sd, Wq), lambda b, p, pt: (rb + b, C_SBQ // Wq)),
                  pl.BlockSpec((sd, Wk), lambda b, p, pt: (rb + b, C_SBK // Wk)),
                  pl.BlockSpec((sd, Wk), lambda b, p, pt: (rb + b, C_SBV // Wk))]
        + _page_specs(pp, (1, 1, Wk, page), pg) + _page_specs(pp, (1, 1, Wk, page), pg),
        out_specs=pl.BlockSpec((sd, Wq), lambda b, p, pt: (b, 0)),
        scratch_shapes=[pltpu.VMEM((rows, Wk), BF16), pltpu.VMEM((rows, LANES), F32),
                        pltpu.VMEM((rows, Wk), F32), pltpu.SMEM((1,), F32)])
    return pl.pallas_call(
        functools.partial(_sb_decode_kernel, sd=sd, n_steps=n_steps, pp=pp),
        grid_spec=grid_spec,
        out_shape=jax.ShapeDtypeStruct((DB * sd, Wq), F32),
        compiler_params=_cparams(("parallel", "arbitrary")),
        name="sb_decode",
    )(page_table, p32, p32, p32, *([pkt] * pp), *([pvt] * pp))


def _nsa_decode_kernel(pt_ref, q_ref, ksn_ref, vsn_ref, kwn_ref, vwn_ref, sm_ref, kws_ref, vws_ref,
                       wk_ref, wv_ref, *rest, sd, n_steps, pp, past):
    del pt_ref
    kc_refs, vc_refs = rest[:pp], rest[pp:2 * pp]
    ks_refs, vs_refs = rest[2 * pp:3 * pp], rest[3 * pp:4 * pp]
    o_ref, kcb_ref, vcb_ref, qg_ref, selr_ref, m_ref, l_ref, acc_ref, tot_ref = rest[4 * pp:]
    j = pl.program_id(1)
    rper = NSA_HEADS // NSA_KV
    rows = rper * sd
    nbc = past // NSA_BLOCK
    page = kc_refs[0].shape[3]
    bpp = page // NSA_BLOCK
    win = kws_ref.shape[3]

    i_row = jnp.bitwise_and(_iota((rows, LANES), 0), sd - 1)
    col = _iota((rows, LANES), 1)
    t_row = past + i_row

    def slopes(g, width=LANES):
        rr = lax.shift_right_logical(_iota((rows, width), 0), 3)
        s = jnp.full((rows, width), NSA_SLOPES[rper * g], F32)
        for r in range(1, rper):
            s = jnp.where(rr == r, NSA_SLOPES[rper * g + r], s)
        return s

    def group_q(g):
        return jnp.concatenate([qg_ref[g, r] for r in range(rper)], axis=0).astype(BF16)

    @pl.when(j == 0)
    def _():
        kcb_ref[...] = jnp.zeros_like(kcb_ref)
        vcb_ref[...] = jnp.zeros_like(vcb_ref)

    @pl.when(j < n_steps)
    def _():
        kacc = kcb_ref[...]
        vacc = vcb_ref[...]
        pos_blk = lax.shift_right_logical(_iota((LANES, LANES), 0), 6)
        n_col = _iota((LANES, LANES), 1)
        for u in range(pp):
            ind = (n_col == (j * pp + u) * bpp + pos_blk).astype(BF16)
            pk = kc_refs[u][0, 0] * wk_ref[...]
            pv_ = vc_refs[u][0, 0] * wv_ref[...]
            kh = pk.astype(BF16)
            vh = pv_.astype(BF16)
            stack = jnp.concatenate([kh, (pk - kh.astype(F32)).astype(BF16),
                                     vh, (pv_ - vh.astype(F32)).astype(BF16)], axis=0)
            r = _dot(stack, ind)
            kacc = kacc + r[0:LANES] + r[LANES:2 * LANES]
            vacc = vacc + r[2 * LANES:3 * LANES] + r[3 * LANES:4 * LANES]
        kcb_ref[...] = kacc
        vcb_ref[...] = vacc

    @pl.when(j == n_steps - 1)
    def _():
        _nsa_group_queries(q_ref, qg_ref, sd)
        sig = jax.nn.sigmoid(sm_ref[...])
        blk_end = (col + 1) * NSA_BLOCK - 1
        dist_c = (t_row - blk_end).astype(F32)
        valid_c = jnp.logical_and(blk_end <= t_row, col < nbc)
        kcb = kcb_ref[...].astype(BF16)
        vcb = vcb_ref[...].astype(BF16)
        n_i = _iota((sd, LANES), 1)
        allowed = n_i < nbc
        forced = jnp.logical_or(n_i == 0, n_i == nbc - 1)
        m_idx = _iota((LANES, LANES), 0)
        n_idx = _iota((LANES, LANES), 1)
        for g in range(NSA_KV):
            sc = _dot(group_q(g), kcb) - slopes(g) * dist_c
            sc = jnp.where(valid_c, sc, NEG)
            mx = jnp.max(sc, axis=1, keepdims=True)
            e = jnp.where(valid_c, jnp.exp(sc - mx), 0.0)
            pc = e / jnp.maximum(jnp.sum(e, axis=1, keepdims=True), jnp.finfo(F32).tiny)
            gate_c = jnp.concatenate(
                [jnp.broadcast_to(sig[:, GATE_LANE0 + 3 * (rper * g + r):GATE_LANE0 + 3 * (rper * g + r) + 1],
                                  (sd, LANES)) for r in range(rper)], axis=0)
            tot_ref[g] = gate_c * _dot_nt(pc.astype(BF16), vcb)
            imp = pc[0:sd]
            for r in range(1, rper):
                imp = imp + pc[r * sd:(r + 1) * sd]
            score = jnp.where(allowed, jnp.where(forced, BIG_SCORE, imp), -1.0)
            score_t = _pad_rows(score, LANES).T
            sel_rows = []
            for i in range(sd):
                cm = score_t[:, i:i + 1]
                rn = score[i:i + 1, :]
                beats = jnp.where(n_idx > m_idx, jnp.where(cm >= rn, 1.0, 0.0), jnp.where(cm > rn, 1.0, 0.0))
                cnt = jnp.sum(beats, axis=0, keepdims=True)
                sel_rows.append(jnp.where(jnp.logical_and(cnt < NSA_TOPK - 1, allowed[i:i + 1, :]), 1.0, 0.0))
            sel = jnp.concatenate(sel_rows, axis=0)
            selr_ref[g] = jnp.concatenate([sel] * rper, axis=0).astype(BF16)
        m_ref[...] = jnp.full(m_ref.shape, NEG, F32)
        l_ref[...] = jnp.zeros_like(l_ref)
        acc_ref[...] = jnp.zeros_like(acc_ref)

    @pl.when(j >= n_steps)
    def _():
        p0 = (j - n_steps) * pp
        wide = pp * LANES
        colw = _iota((rows, wide), 1)
        iw = jnp.bitwise_and(_iota((rows, wide), 0), sd - 1)
        dist = (past + iw - (p0 * page + colw)).astype(F32)
        e_p = (_iota((LANES, wide), 0) ==
               p0 * bpp + lax.shift_right_logical(_iota((LANES, wide), 1), 6)).astype(BF16)
        for g in range(NSA_KV):
            qg = group_q(g)
            qk = [_dot(qg, ks_refs[u][0, 0].astype(BF16)) for u in range(pp)]
            s = qk[0] if pp == 1 else jnp.concatenate(qk, axis=1)
            selb = (_dot(selr_ref[g], e_p) - 1.0) * (-NEG)
            s = s - slopes(g, wide) * dist + selb

            def pv(pr):
                out = _dot_nt(pr[:, :LANES], vs_refs[0][0, 0].astype(BF16))
                for u in range(1, pp):
                    out = out + _dot_nt(pr[:, u * LANES:(u + 1) * LANES], vs_refs[u][0, 0].astype(BF16))
                return out

            _softmax_update(s, pv, m_ref, l_ref, acc_ref, g)

    @pl.when(j == 2 * n_steps - 1)
    def _():
        sig = jax.nn.sigmoid(sm_ref[...])

        def gate(g, branch):
            return jnp.concatenate(
                [jnp.broadcast_to(sig[:, GATE_LANE0 + 3 * (rper * g + r) + branch:
                                      GATE_LANE0 + 3 * (rper * g + r) + branch + 1], (sd, LANES))
                 for r in range(rper)], axis=0)

        new_ok = col <= i_row
        dist_new = (i_row - col).astype(F32)
        ksn = _pad_rows(ksn_ref[...], LANES).astype(BF16)
        vsn = _pad_rows(vsn_ref[...], LANES).astype(BF16)
        kwn = _pad_rows(kwn_ref[...], LANES).astype(BF16)
        vwn = _pad_rows(vwn_ref[...], LANES).astype(BF16)
        kws = kws_ref[0, 0].astype(BF16)
        vws = vws_ref[0, 0].astype(BF16)
        colw = _iota((rows, win), 1)
        iw = jnp.bitwise_and(_iota((rows, win), 0), sd - 1)
        dist_w = (win + iw - colw).astype(F32)
        ok_w = colw > iw
        tots = []
        for g in range(NSA_KV):
            qg = group_q(g)
            sl = slopes(g)
            s = jnp.where(new_ok, _dot_nt(qg, ksn) - sl * dist_new, NEG)
            _softmax_update(s, lambda pr: _dot(pr, vsn), m_ref, l_ref, acc_ref, g)
            o_sel = acc_ref[g] / l_ref[g]
            s1 = jnp.where(ok_w, _dot(qg, kws) - slopes(g, win) * dist_w, NEG)
            s2 = jnp.where(new_ok, _dot_nt(qg, kwn) - sl * dist_new, NEG)
            mx = jnp.maximum(jnp.max(s1, axis=1, keepdims=True), jnp.max(s2, axis=1, keepdims=True))
            p1 = jnp.exp(s1 - mx)
            p2 = jnp.exp(s2 - mx)
            den = jnp.sum(p1, axis=1, keepdims=True) + jnp.sum(p2, axis=1, keepdims=True)
            o_win = (_dot_nt(p1.astype(BF16), vws) + _dot(p2.astype(BF16), vwn)) / den
            t = tot_ref[g] + gate(g, 1) * o_sel + gate(g, 2) * o_win
            tots.append([t[r * sd:(r + 1) * sd] for r in range(rper)])
        _nsa_output(tots, o_ref)


def _nsa_decode(p32, page_table, pool_kc, pool_vc, pool_ks, pool_vs, st_kw, st_vw, wk_t, wv_t,
                layer, row0, DB, sd, pp):
    n_pages = page_table.shape[1]
    n_steps = n_pages // pp
    page = pool_kc.shape[2]
    past = n_pages * page
    W = NSA_HEADS * HEAD_DIM
    rper = NSA_HEADS // NSA_KV
    rows = rper * sd
    assert past // NSA_BLOCK <= LANES and past % NSA_BLOCK == 0 and sd <= NSA_BLOCK
    rb = row0 // sd
    win = st_kw.shape[2]
    new = lambda c: pl.BlockSpec((sd, LANES), lambda b, j, pt: (rb + b, C_NKV // LANES + c))
    pg1 = lambda u: (lambda b, j, pt: (layer, pt[b, jnp.minimum(j, n_steps - 1) * pp + u], 0, 0))
    pg2 = lambda u: (lambda b, j, pt: (layer, pt[b, jnp.maximum(j - n_steps, 0) * pp + u], 0, 0))
    pages = lambda index_of: _page_specs(pp, (1, 1, LANES, page), index_of)
    state = pl.BlockSpec((1, 1, LANES, win), lambda b, j, pt: (layer, b, 0, 0))
    grid_spec = pltpu.PrefetchScalarGridSpec(
        num_scalar_prefetch=1,
        grid=(DB, 2 * n_steps),
        in_specs=[pl.BlockSpec((sd, W), lambda b, j, pt: (rb + b, C_NSAQ // W)),
                  new(2), new(3), new(4), new(5),
                  pl.BlockSpec((sd, LANES), lambda b, j, pt: (rb + b, C_SMALL // LANES)),
                  state, state,
                  pl.BlockSpec(wk_t.shape, lambda b, j, pt: (0, 0)),
                  pl.BlockSpec(wv_t.shape, lambda b, j, pt: (0, 0))]
        + pages(pg1) + pages(pg1) + pages(pg2) + pages(pg2),
        out_specs=pl.BlockSpec((sd, W), lambda b, j, pt: (b, 0)),
        scratch_shapes=[pltpu.VMEM((LANES, LANES), F32), pltpu.VMEM((LANES, LANES), F32),
                        pltpu.VMEM((NSA_KV, rper, sd, LANES), F32),
                        pltpu.VMEM((NSA_KV, rows, LANES), BF16),
                        pltpu.VMEM((NSA_KV, rows, LANES), F32), pltpu.VMEM((NSA_KV, rows, LANES), F32),
                        pltpu.VMEM((NSA_KV, rows, LANES), F32), pltpu.VMEM((NSA_KV, rows, LANES), F32)])
    tp = _transposed_pool
    return pl.pallas_call(
        functools.partial(_nsa_decode_kernel, sd=sd, n_steps=n_steps, pp=pp, past=past),
        grid_spec=grid_spec,
        out_shape=jax.ShapeDtypeStruct((DB * sd, W), F32),
        compiler_params=_cparams(("parallel", "arbitrary")),
        name="nsa_decode",
    )(page_table, p32, p32, p32, p32, p32, p32, tp(st_kw), tp(st_vw), wk_t, wv_t,
      *([tp(pool_kc)] * pp), *([tp(pool_vc)] * pp), *([tp(pool_ks)] * pp), *([tp(pool_vs)] * pp))


def _prep_in_weights(w_in, b_in):
    segs = {}
    off = 0
    for name, n in (("fox_q", 512), ("fox_k", 256), ("fox_v", 256), ("fox_f", 8), ("sb_q", 512), ("sb_k", 256),
                    ("sb_v", 256), ("nsa_q", 512), ("nsa_kv", 768), ("nsa_g", 24), ("z", 1536), ("merge", 3072)):
        segs[name] = (off, off + n)
        off += n
    assert off == w_in.shape[-1]
    scale = HEAD_DIM ** -0.5

    def build(a):
        sl = lambda nm: a[..., segs[nm][0]:segs[nm][1]]
        pad = jnp.zeros(a.shape[:-1] + (LANES - 32,), a.dtype)
        return jnp.concatenate([sl("fox_q") * scale, sl("fox_k"), sl("fox_v"), sl("sb_q") * scale, sl("sb_k"),
                                sl("sb_v"), sl("nsa_q") * scale, sl("z"), sl("merge"), sl("nsa_kv"),
                                sl("fox_f"), sl("nsa_g"), pad], axis=-1)
    wp = build(w_in).astype(BF16)
    bp = build(b_in)[:, None, :]
    assert wp.shape[-1] == N_PAD
    return wp, bp


def kernel(x_prompt, x_sample, cache_fox_k, cache_fox_v, cache_fox_logf, cache_sb_k, cache_sb_v, cache_nsa_kc, cache_nsa_vc, cache_nsa_ks, cache_nsa_vs, state_nsa_kw, state_nsa_vw, page_table, norm_g, w_in, b_in, w_cmp_k, w_cmp_v, w_br_fox, w_br_sb, w_br_nsa, w_out, final_norm_g):
    B, S, D = x_prompt.shape
    DB, SD, _ = x_sample.shape
    depth = w_in.shape[0]
    page = cache_fox_k.shape[2]
    n_pages = page_table.shape[1]
    win = state_nsa_kw.shape[2]
    tq = LANES
    tk_fox = _tile(S, 512, LANES)
    tk_sb = _tile(S, 256, LANES)
    pp_sb = _tile(n_pages, 8, 1)
    pp_fox = _tile(n_pages, 32, 1)
    pp_nsa = _tile(n_pages, 32, 1)
    assert cache_fox_k.shape[3:] == (FOX_KV, HEAD_DIM) and cache_fox_logf.shape[3] == FOX_HEADS
    assert cache_sb_k.shape[3:] == (SB_KV, HEAD_DIM) and cache_nsa_kc.shape[3:] == (NSA_KV, HEAD_DIM)
    assert w_br_fox.shape[1] == FOX_HEADS * HEAD_DIM and w_br_sb.shape[1] == SB_HEADS * HEAD_DIM
    assert w_br_nsa.shape[1] == NSA_HEADS * HEAD_DIM and w_cmp_k.shape[1:] == (NSA_BLOCK, NSA_KV)
    assert SD == 8 and page == LANES and S % tq == 0 and S >= win and n_pages * page >= win
    TP, TS = B * S, DB * SD
    assert TP % SD == 0

    x = jnp.concatenate([x_prompt.reshape(TP, D), x_sample.reshape(TS, D)], axis=0)
    wp, bp = _prep_in_weights(w_in, b_in)
    wbr = jnp.stack([w_br_fox, w_br_sb, w_br_nsa], axis=1).astype(BF16)
    wout = w_out.astype(BF16)
    wk_exp = jnp.repeat(w_cmp_k, HEAD_DIM, axis=2)
    wv_exp = jnp.repeat(w_cmp_v, HEAD_DIM, axis=2)
    bpp = page // NSA_BLOCK
    key_table = _nsa_key_table(S)
    kv_bufs = [jnp.zeros((depth, B, kv * HEAD_DIM, S), F32) for _, kv in KV_LEAVES]

    h = _rmsnorm(x, norm_g[0], BF16)
    logf_p_all, rows_s = [], []
    for l in range(depth):
        p32, p16 = _inproj(h, wp[l], bp[l])
        logf_p, ft = _foxgate(p32, B, S)
        ft4 = ft.reshape(B, FOX_KV // 2, 4, S // tk_fox, tk_fox).transpose(0, 1, 3, 2, 4)
        of_p = _fox_prompt(p32, p16, ft4, B, S, tq, tk_fox)
        os_p = _sb_prompt(p32, p16, B, S, tq, tk_sb)
        on_p = _nsa_prompt(p32, p16, wk_exp[l], wv_exp[l], key_table, B, S, tq, tk_fox, win)
        kv_bufs = _kv_rows(p32, kv_bufs, l, B, S)
        logf_p_all.append(logf_p.reshape(B, S, FOX_HEADS))
        of_s, logf_s = _fox_decode(p32, page_table, cache_fox_k, cache_fox_v, cache_fox_logf, l, TP, DB, SD, pp_fox)
        os_s = _sb_decode(p32, page_table, cache_sb_k, cache_sb_v, l, TP, DB, SD, pp_sb)
        wk_t = jnp.tile(wk_exp[l].T, (1, bpp))
        wv_t = jnp.tile(wv_exp[l].T, (1, bpp))
        on_s = _nsa_decode(p32, page_table, cache_nsa_kc, cache_nsa_vc, cache_nsa_ks, cache_nsa_vs,
                           state_nsa_kw, state_nsa_vw, wk_t, wv_t, l, TP, DB, SD, pp_nsa)
        o_fox = jnp.concatenate([of_p, of_s], axis=0)
        o_sb = jnp.concatenate([os_p, os_s], axis=0)
        o_nsa = jnp.concatenate([on_p, on_s], axis=0)
        last = l + 1 == depth
        g_next = final_norm_g if last else norm_g[l + 1]
        x, h = _merge(o_fox, o_sb, o_nsa, p32, x, wbr[l], wout[l], g_next, F32 if last else BF16)

        rs = [p32[TP:, c0:c0 + kv * HEAD_DIM].reshape(DB, SD, kv, HEAD_DIM) for c0, kv in KV_LEAVES]
        rs[8] = jnp.concatenate([state_nsa_kw[l], rs[8]], axis=1)[:, SD:]
        rs[9] = jnp.concatenate([state_nsa_vw[l], rs[9]], axis=1)[:, SD:]
        rows_s.append(rs[:2] + [logf_s.reshape(DB, SD, FOX_HEADS)] + rs[2:])

    y_prompt = h[:TP].reshape(B, S, D)
    y_sample = h[TP:].reshape(DB, SD, D)
    kv_p = [buf.reshape(depth, B, kv, HEAD_DIM, S).transpose(0, 1, 4, 2, 3) for buf, (_, kv) in zip(kv_bufs, KV_LEAVES)]
    kv_p[8], kv_p[9] = kv_p[8][:, :, S - win:], kv_p[9][:, :, S - win:]
    outs_p = kv_p[:2] + [jnp.stack(logf_p_all)] + kv_p[2:]
    outs_s = [jnp.stack([r[i] for r in rows_s]) for i in range(11)]
    return (y_prompt, y_sample, *outs_p, *outs_s)
```

```python
import functools
import math

import numpy as np
import jax
import jax.numpy as jnp
from jax import lax
from jax.experimental import pallas as pl
from jax.experimental.pallas import tpu as pltpu

F32 = jnp.float32
BF16 = jnp.bfloat16

HEAD_DIM = 64
FOX_HEADS, FOX_KV = 8, 4
SB_HEADS, SB_KV = 8, 4
NSA_HEADS, NSA_KV = 8, 2
NSA_BLOCK = 64
NSA_TOPK = 16
RMS_EPS = 1e-6
LANES = 128
NEG = -1e30
BIG_SCORE = 8.0
EXP_ZERO = -104.0

C_FOXQ, C_FOXK, C_FOXV = 0, 512, 768
C_SBQ, C_SBK, C_SBV = 1024, 1536, 1792
C_NSAQ = 2048
C_ZF, C_ZS, C_ZN = 2560, 3072, 3584
C_MERGE = 4096
C_NKV = 7168
C_SMALL = 7936
N_PAD = 8064
GATE_LANE0 = 8
ALIBI_LANE = 64

NSA_SLOPES = tuple(float(2.0 ** (-8.0 * (i + 1) / NSA_HEADS)) for i in range(NSA_HEADS))


def _tile(n, target, mult):
    best = None
    for d in range(mult, min(n, target) + 1, mult):
        if n % d == 0:
            best = d
    return best if best is not None else n


def _cparams(sem, vmem_mb=48):
    return pltpu.CompilerParams(dimension_semantics=sem, vmem_limit_bytes=vmem_mb * 1024 * 1024)


def _dot(a, b):
    return jnp.dot(a, b, preferred_element_type=F32)


def _dot_nt(a, b):
    return lax.dot_general(a, b, (((1,), (1,)), ((), ())), preferred_element_type=F32)


def _split3(x):
    hi = x.astype(BF16)
    r = x - hi.astype(F32)
    mid = r.astype(BF16)
    lo = (r - mid.astype(F32)).astype(BF16)
    return hi, mid, lo


def _dot_exact(x, m01):
    hi, mid, lo = _split3(x)
    return _dot(hi, m01) + _dot(mid, m01) + _dot(lo, m01)


def _dot_nt_exact_rhs(m01, x):
    hi, mid, lo = _split3(x)
    return _dot_nt(m01, hi) + _dot_nt(m01, mid) + _dot_nt(m01, lo)


def _log_sigmoid(x):
    return jnp.minimum(x, 0.0) - jnp.log1p(jnp.exp(-jnp.abs(x)))


def _softplus(x):
    return jnp.maximum(x, 0.0) + jnp.log1p(jnp.exp(-jnp.abs(x)))


def _iota(shape, dim):
    return lax.broadcasted_iota(jnp.int32, shape, dim)


def _swap_halves(x):
    return pltpu.roll(x, 64, 1)


def _rmsnorm_body(x, g):
    return x * lax.rsqrt(jnp.mean(x * x, axis=-1, keepdims=True) + RMS_EPS) * g


def _rmsnorm_kernel(x_ref, g_ref, o_ref):
    o_ref[...] = _rmsnorm_body(x_ref[...], g_ref[...]).astype(o_ref.dtype)


def _rmsnorm(x, g, out_dtype):
    T, D = x.shape
    tm = _tile(T, 1024, 16)
    return pl.pallas_call(
        _rmsnorm_kernel,
        grid=(T // tm,),
        in_specs=[pl.BlockSpec((tm, D), lambda i: (i, 0)), pl.BlockSpec((1, D), lambda i: (0, 0))],
        out_specs=pl.BlockSpec((tm, D), lambda i: (i, 0)),
        out_shape=jax.ShapeDtypeStruct((T, D), out_dtype),
        compiler_params=_cparams(("parallel",)),
        name="rmsnorm",
    )(x, g.reshape(1, D))


def _inproj_kernel(h_ref, w_ref, b_ref, o32_ref, o16_ref):
    acc = _dot(h_ref[...], w_ref[...]) + b_ref[...]
    o32_ref[...] = acc
    o16_ref[...] = acc.astype(BF16)


def _inproj(h, w, b):
    T, D = h.shape
    N = w.shape[1]
    tm = _tile(T, 1280, 16)
    tn = _tile(N, 1152, 128)
    return pl.pallas_call(
        _inproj_kernel,
        grid=(N // tn, T // tm),
        in_specs=[pl.BlockSpec((tm, D), lambda j, i: (i, 0)),
                  pl.BlockSpec((D, tn), lambda j, i: (0, j)),
                  pl.BlockSpec((1, tn), lambda j, i: (0, j))],
        out_specs=[pl.BlockSpec((tm, tn), lambda j, i: (i, j)),
                   pl.BlockSpec((tm, tn), lambda j, i: (i, j))],
        out_shape=[jax.ShapeDtypeStruct((T, N), F32), jax.ShapeDtypeStruct((T, N), BF16)],
        compiler_params=_cparams(("parallel", "parallel")),
        name="inproj",
    )(h, w, b)


def _merge_kernel(ofp_ref, osp_ref, onp_ref, ofs_ref, oss_ref, ons_ref, zf_ref, zs_ref, zn_ref,
                  g0_ref, g1_ref, g2_ref, x_ref, wbr_ref, wout_ref, gn_ref, xo_ref, ho_ref, *, n_prompt_tiles):
    is_prompt = pl.program_id(0) < n_prompt_tiles

    def branch(op_ref, os_ref, z_ref, g_ref, k):
        z = z_ref[...]
        o = jnp.where(is_prompt, op_ref[...], os_ref[...])
        u = (o * (z * jax.nn.sigmoid(z))).astype(BF16)
        return jax.nn.sigmoid(g_ref[...]) * _dot(u, wbr_ref[k])

    y = (branch(ofp_ref, ofs_ref, zf_ref, g0_ref, 0) + branch(osp_ref, oss_ref, zs_ref, g1_ref, 1)
         + branch(onp_ref, ons_ref, zn_ref, g2_ref, 2))
    xn = x_ref[...] + _dot(y.astype(BF16), wout_ref[...])
    xo_ref[...] = xn
    ho_ref[...] = _rmsnorm_body(xn, gn_ref[...]).astype(ho_ref.dtype)


def _merge(o_prompt, o_sample, p32, x, wbr, wout, g_next, h_dtype):
    T, D = x.shape
    TP, W = o_prompt[0].shape
    TS = o_sample[0].shape[0]
    tm = _tile(math.gcd(TP, TS), 256, 16)
    n_p = TP // tm
    row = lambda c: (lambda i: (i, c))
    prompt_rows = pl.BlockSpec((tm, W), lambda i: (jnp.minimum(i, n_p - 1), 0))
    sample_rows = pl.BlockSpec((tm, W), lambda i: (jnp.maximum(i - n_p, 0), 0))
    return pl.pallas_call(
        functools.partial(_merge_kernel, n_prompt_tiles=n_p),
        grid=(T // tm,),
        in_specs=[prompt_rows] * 3 + [sample_rows] * 3 + [
                  pl.BlockSpec((tm, W), row(C_ZF // W)), pl.BlockSpec((tm, W), row(C_ZS // W)),
                  pl.BlockSpec((tm, W), row(C_ZN // W)),
                  pl.BlockSpec((tm, D), row(C_MERGE // D)), pl.BlockSpec((tm, D), row(C_MERGE // D + 1)),
                  pl.BlockSpec((tm, D), row(C_MERGE // D + 2)),
                  pl.BlockSpec((tm, D), row(0)),
                  pl.BlockSpec((3, W, D), lambda i: (0, 0, 0)),
                  pl.BlockSpec((D, D), lambda i: (0, 0)),
                  pl.BlockSpec((1, D), lambda i: (0, 0))],
        out_specs=[pl.BlockSpec((tm, D), row(0)), pl.BlockSpec((tm, D), row(0))],
        out_shape=[jax.ShapeDtypeStruct((T, D), F32), jax.ShapeDtypeStruct((T, D), h_dtype)],
        compiler_params=_cparams(("parallel",)),
        name="merge_out",
    )(*o_prompt, *o_sample, p32, p32, p32, p32, p32, p32, x, wbr, wout, g_next.reshape(1, D))


KV_LEAVES = ((C_FOXK, FOX_KV), (C_FOXV, FOX_KV), (C_SBK, SB_KV), (C_SBV, SB_KV)) + tuple(
    (C_NKV + c * LANES, NSA_KV) for c in range(6))


def _kv_rows_kernel(*refs):
    n = len(KV_LEAVES)
    for i in range(n):
        refs[2 * n + i][0, 0] = refs[i][...].T


def _kv_rows(p32, bufs, layer, B, S):
    n = len(KV_LEAVES)
    ts = _tile(S, 512, LANES)
    nst = S // ts
    widths = [kv * HEAD_DIM for _, kv in KV_LEAVES]
    return pl.pallas_call(
        _kv_rows_kernel,
        grid=(B, nst),
        in_specs=[pl.BlockSpec((ts, w), lambda b, s, c=c0 // w: (b * nst + s, c)) for (c0, _), w in zip(KV_LEAVES, widths)]
        + [pl.BlockSpec(memory_space=pl.ANY)] * n,
        out_specs=[pl.BlockSpec((1, 1, w, ts), lambda b, s: (layer, b, 0, s)) for w in widths],
        out_shape=[jax.ShapeDtypeStruct(buf.shape, F32) for buf in bufs],
        input_output_aliases={n + i: i for i in range(n)},
        compiler_params=_cparams(("parallel", "parallel")),
        name="kv_rows",
    )(*([p32] * n), *bufs)


def _foxgate_kernel(sm_ref, logf_ref, ft_ref, carry_ref, *, ts):
    @pl.when(pl.program_id(1) == 0)
    def _():
        carry_ref[...] = jnp.zeros_like(carry_ref)

    lf = _log_sigmoid(sm_ref[...])
    logf_ref[...] = lf[:, :FOX_HEADS]
    lft = lf.T[:FOX_HEADS, :]
    upper = (_iota((ts, ts), 0) <= _iota((ts, ts), 1)).astype(BF16)
    c = _dot_exact(lft, upper) + carry_ref[:, 0:1]
    ft_ref[0] = c
    carry_ref[...] = jnp.broadcast_to(c[:, ts - 1:ts], carry_ref.shape)


def _foxgate(p32, B, S):
    ts = _tile(S, 512, 128)
    nst = S // ts
    return pl.pallas_call(
        functools.partial(_foxgate_kernel, ts=ts),
        grid=(B, nst),
        in_specs=[pl.BlockSpec((ts, LANES), lambda b, s: (b * nst + s, C_SMALL // LANES))],
        out_specs=[pl.BlockSpec((ts, FOX_HEADS), lambda b, s: (b * nst + s, 0)),
                   pl.BlockSpec((1, FOX_HEADS, ts), lambda b, s: (b, 0, s))],
        out_shape=[jax.ShapeDtypeStruct((B * S, FOX_HEADS), F32), jax.ShapeDtypeStruct((B, FOX_HEADS, S), F32)],
        scratch_shapes=[pltpu.VMEM((FOX_HEADS, LANES), F32)],
        compiler_params=_cparams(("parallel", "arbitrary")),
        name="fox_gate",
    )(p32)


def _pair_queries(q_ref, qz_ref):
    tq = q_ref.shape[0]
    low = _iota((tq, LANES), 1) < HEAD_DIM
    qa, qb = q_ref[:, :LANES], q_ref[:, LANES:]
    qz_ref[0] = jnp.where(low, qa, 0.0).astype(BF16)
    qz_ref[1] = jnp.where(low, _swap_halves(qa), 0.0).astype(BF16)
    qz_ref[2] = jnp.where(low, 0.0, _swap_halves(qb)).astype(BF16)
    qz_ref[3] = jnp.where(low, 0.0, qb).astype(BF16)


def _pair_output(a0, a1, a2, a3, o_ref):
    low = _iota(a0.shape, 1) < HEAD_DIM
    o_ref[:, :LANES] = jnp.where(low, a0, _swap_halves(a1))
    o_ref[:, LANES:] = jnp.where(low, _swap_halves(a2), a3)


def _rep(x, k):
    return x if k == 1 else jnp.concatenate([x] * k, axis=1)


def _stack_rows(row_vectors, rows):
    n = row_vectors[0].shape[1]
    return jnp.concatenate([jnp.broadcast_to(v, (rows, n)) for v in row_vectors], axis=0)


def _softmax_update(s, pv_fn, m_ref, l_ref, acc_ref, idx=Ellipsis):
    m_prev = m_ref[idx]
    m_new = jnp.maximum(m_prev, jnp.max(s, axis=1, keepdims=True))
    alpha = jnp.exp(m_prev - m_new)
    p = jnp.exp(s - _rep(m_new, s.shape[1] // LANES))
    l_ref[idx] = alpha * l_ref[idx] + jnp.sum(p, axis=1, keepdims=True)
    acc = acc_ref[idx]
    acc_ref[idx] = _rep(alpha, acc.shape[1] // LANES) * acc + pv_fn(p.astype(BF16))
    m_ref[idx] = m_new


def _fox_prompt_kernel(q_ref, k_ref, v_ref, ft_ref, o_ref, qz_ref, m_ref, l_ref, acc_ref, s_ref, *, tq, tk):
    qi = pl.program_id(2)
    q0 = qi * tq
    _pair_queries(q_ref, qz_ref)
    m_ref[...] = jnp.full(m_ref.shape, NEG, F32)
    l_ref[...] = jnp.zeros_like(l_ref)
    acc_ref[...] = jnp.zeros_like(acc_ref)
    qz = qz_ref[...].reshape(4 * tq, LANES)

    def scores(kj):
        k0 = pl.multiple_of(kj * tk, tk)
        fk = ft_ref[0, 0, kj]
        return _dot_nt(qz, k_ref[pl.ds(k0, tk), :]) - _stack_rows([fk[hh:hh + 1, :] for hh in range(4)], tq)

    def consume(kj, s, diag):
        k0 = pl.multiple_of(kj * tk, tk)
        vt = v_ref[pl.ds(k0, tk), :]
        if diag:
            t_idx = jnp.bitwise_and(_iota((4 * tq, tk), 0), tq - 1)
            s = jnp.where(k0 + _iota((4 * tq, tk), 1) <= q0 + t_idx, s, NEG)
        _softmax_update(s, lambda p: _dot(p, vt), m_ref, l_ref, acc_ref)

    n_full = q0 // tk
    s_ref[...] = scores(0)

    def body(kj, carry):
        s = s_ref[...]
        s_ref[...] = scores(kj + 1)
        consume(kj, s, False)
        return carry

    lax.fori_loop(0, n_full, body, 0)
    consume(n_full, s_ref[...], True)
    out = acc_ref[...] / l_ref[...]
    _pair_output(*[out[hh * tq:(hh + 1) * tq] for hh in range(4)], o_ref)


def _fox_prompt(p32, p16, ft4, B, S, tq, tk):
    nq = S // tq
    n_pairs = FOX_KV // 2
    W = FOX_HEADS * HEAD_DIM
    return pl.pallas_call(
        functools.partial(_fox_prompt_kernel, tq=tq, tk=tk),
        grid=(B, n_pairs, nq),
        in_specs=[pl.BlockSpec((tq, 2 * LANES), lambda b, g, i: (b * nq + i, C_FOXQ // (2 * LANES) + g)),
                  pl.BlockSpec((S, LANES), lambda b, g, i: (b, C_FOXK // LANES + g)),
                  pl.BlockSpec((S, LANES), lambda b, g, i: (b, C_FOXV // LANES + g)),
                  pl.BlockSpec((1, 1, S // tk, 4, tk), lambda b, g, i: (b, g, 0, 0, 0))],
        out_specs=pl.BlockSpec((tq, 2 * LANES), lambda b, g, i: (b * nq + i, g)),
        out_shape=jax.ShapeDtypeStruct((B * S, W), F32),
        scratch_shapes=[pltpu.VMEM((4, tq, LANES), BF16), pltpu.VMEM((4 * tq, LANES), F32),
                        pltpu.VMEM((4 * tq, LANES), F32), pltpu.VMEM((4 * tq, LANES), F32),
                        pltpu.VMEM((4 * tq, tk), F32)],
        compiler_params=_cparams(("parallel", "parallel", "arbitrary")),
        name="fox_prompt",
    )(p32, p16, p16, ft4)


def _suffix_and_total(lk, uo):
    hi = lk.astype(BF16)
    lo = (lk - hi.astype(F32)).astype(BF16)
    r = _dot(hi, uo) + _dot(lo, uo)
    return r[:, :LANES], r[:, LANES:]


def _suffix_matrix():
    j = _iota((LANES, LANES), 0)
    s = _iota((LANES, LANES), 1)
    return jnp.concatenate([(j > s).astype(BF16), jnp.ones((LANES, LANES), BF16)], axis=1)


def _sb_blocks(zs, carry, uo, masks):
    running = carry
    out = []
    for z, mask in zip(zs, masks):
        sp = _softplus(z)
        lk = -sp
        if mask is not None:
            lk = jnp.where(mask, lk, 0.0)
        after, tot = _suffix_and_total(lk, uo)
        a = jnp.exp(z - sp + after + running)
        if mask is not None:
            a = jnp.where(mask, a, 0.0)
        out.append(a)
        running = running + tot
    return out, running


def _sb_prompt_kernel(q_ref, k_ref, v_ref, o_ref, qz_ref, carry_ref, acc_ref, *, tq, tk):
    qi = pl.program_id(2)
    q0 = qi * tq
    _pair_queries(q_ref, qz_ref)
    carry_ref[...] = jnp.zeros_like(carry_ref)
    acc_ref[...] = jnp.zeros_like(acc_ref)
    qz = qz_ref[...].reshape(4 * tq, LANES)
    uo = _suffix_matrix()
    nsub = tk // LANES

    def tile(kj, diag):
        k0 = pl.multiple_of(kj * tk, tk)
        kt = k_ref[pl.ds(k0, tk), :]
        vt = v_ref[pl.ds(k0, tk), :]
        z = _dot_nt(qz, kt)
        order = list(reversed(range(nsub)))
        masks = [None] * nsub
        if diag:
            t_idx = jnp.bitwise_and(_iota((4 * tq, LANES), 0), tq - 1)
            col = _iota((4 * tq, LANES), 1)
            masks = [k0 + u * LANES + col < q0 + t_idx for u in order]
        ws, running = _sb_blocks([z[:, u * LANES:(u + 1) * LANES] for u in order], carry_ref[...], uo, masks)
        a = ws[0] if nsub == 1 else jnp.concatenate(list(reversed(ws)), axis=1)
        acc_ref[...] = acc_ref[...] + _dot(a.astype(BF16), vt)
        carry_ref[...] = running
        return jnp.max(running)

    kd = q0 // tk
    top0 = tile(kd, True)

    def cond(st):
        return jnp.logical_and(st[0] >= 0, st[1] > EXP_ZERO)

    def body(st):
        return st[0] - 1, tile(st[0], False)

    lax.while_loop(cond, body, (kd - 1, top0))
    acc = acc_ref[...]
    _pair_output(*[acc[hh * tq:(hh + 1) * tq] for hh in range(4)], o_ref)


def _sb_prompt(p32, p16, B, S, tq, tk):
    nq = S // tq
    n_pairs = SB_KV // 2
    W = SB_HEADS * HEAD_DIM
    return pl.pallas_call(
        functools.partial(_sb_prompt_kernel, tq=tq, tk=tk),
        grid=(B, n_pairs, nq),
        in_specs=[pl.BlockSpec((tq, 2 * LANES), lambda b, g, i: (b * nq + i, C_SBQ // (2 * LANES) + g)),
                  pl.BlockSpec((S, LANES), lambda b, g, i: (b, C_SBK // LANES + g)),
                  pl.BlockSpec((S, LANES), lambda b, g, i: (b, C_SBV // LANES + g))],
        out_specs=pl.BlockSpec((tq, 2 * LANES), lambda b, g, i: (b * nq + i, g)),
        out_shape=jax.ShapeDtypeStruct((B * S, W), F32),
        scratch_shapes=[pltpu.VMEM((4, tq, LANES), BF16), pltpu.VMEM((4 * tq, LANES), F32),
                        pltpu.VMEM((4 * tq, LANES), F32)],
        compiler_params=_cparams(("parallel", "parallel", "arbitrary")),
        name="sb_prompt",
    )(p32, p16, p16)


def _nsa_group_queries(q_ref, qg_ref, rows):
    low = _iota((rows, LANES), 1) < HEAD_DIM
    rper = NSA_HEADS // NSA_KV
    for g in range(NSA_KV):
        for r in range(rper):
            h = rper * g + r
            blk = q_ref[:, (h // 2) * LANES:(h // 2 + 1) * LANES]
            if h % 2 != g:
                blk = _swap_halves(blk)
            keep = low if g == 0 else jnp.logical_not(low)
            qg_ref[g, r] = jnp.where(keep, blk, 0.0).astype(qg_ref.dtype)


def _nsa_output(tot, o_ref):
    low = _iota(tot[0][0].shape, 1) < HEAD_DIM
    for j in range(NSA_HEADS // 2):
        g = j // 2
        a, b = tot[g][2 * (j % 2)], tot[g][2 * (j % 2) + 1]
        if g == 0:
            blk = jnp.where(low, a, _swap_halves(b))
        else:
            blk = jnp.where(low, _swap_halves(a), b)
        o_ref[:, j * LANES:(j + 1) * LANES] = blk


def _topk_select_t(score_t, allowed_t, n_real, k):
    nbp = score_t.shape[0]
    sc = score_t[:n_real]
    n_idx = _iota(sc.shape, 0)
    cnt = jnp.zeros(sc.shape, F32)
    for m in range(n_real):
        row = sc[m:m + 1, :]
        cnt = cnt + jnp.where(n_idx > m, jnp.where(row >= sc, 1.0, 0.0), jnp.where(row > sc, 1.0, 0.0))
    sel = jnp.where(jnp.logical_and(cnt < k, allowed_t[:n_real]), 1.0, 0.0)
    if nbp > n_real:
        sel = jnp.concatenate([sel, jnp.zeros((nbp - n_real, sc.shape[1]), F32)], axis=0)
    return sel


def _nsa_key_table(S):
    assert S // NSA_BLOCK <= ALIBI_LANE
    pos = np.arange(S)
    t = np.zeros((S, LANES), np.float32)
    t[pos, pos // NSA_BLOCK] = NEG
    t[:, ALIBI_LANE] = pos // NSA_BLOCK
    t[:, ALIBI_LANE + 1] = pos % NSA_BLOCK
    return jnp.asarray(t, BF16)


def _nsa_prompt_kernel(q_ref, kc_ref, vc_ref, ks_ref, vs_ref, kw_ref, vw_ref, sm_ref, wk_ref, wv_ref,
                       et_ref, o_ref, kcb_ref, vcb_ref, qg_ref, lhs_ref, m_ref, l_ref, acc_ref, tot_ref,
                       *, tq, tk, S, win):
    qi = pl.program_id(1)
    q0 = qi * tq
    nb = S // NSA_BLOCK
    nbp = kcb_ref.shape[0]
    rper = NSA_HEADS // NSA_KV
    rows = rper * tq

    @pl.when(qi == 0)
    def _():
        def pool(x_ref, w_ref):
            x = x_ref[...].astype(F32).reshape(nb, NSA_BLOCK, LANES)
            c = jnp.sum(x * w_ref[...][None], axis=1)
            if nbp > nb:
                c = jnp.concatenate([c, jnp.zeros((nbp - nb, LANES), F32)], axis=0)
            return c.astype(BF16)
        kcb_ref[...] = pool(kc_ref, wk_ref)
        vcb_ref[...] = pool(vc_ref, wv_ref)

    _nsa_group_queries(q_ref, qg_ref, tq)
    sig = jax.nn.sigmoid(sm_ref[...])

    def gate(g, branch):
        cols = [GATE_LANE0 + 3 * (rper * g + r) + branch for r in range(rper)]
        return jnp.concatenate([jnp.broadcast_to(sig[:, c:c + 1], (tq, LANES)) for c in cols], axis=0)

    def group_q(g):
        return qg_ref[g].reshape(rows, LANES)

    lane = _iota((tq, LANES), 1)
    n_i = _iota((rows, nbp), 1)
    t_i = q0 + jnp.bitwise_and(_iota((rows, nbp), 0), tq - 1)
    blk_end = (n_i + 1) * NSA_BLOCK - 1
    dist_c = (t_i - blk_end).astype(F32)
    valid_c = blk_end <= t_i
    n_t = _iota((nbp, tq), 0)
    cur_t = lax.shift_right_logical(q0 + _iota((nbp, tq), 1), 6)
    allowed_t = n_t <= cur_t
    forced_t = jnp.logical_or(n_t == 0, jnp.logical_or(n_t == cur_t, n_t == cur_t - 1))
    for g in range(NSA_KV):
        slope_rows = _stack_rows([jnp.full((1, nbp), NSA_SLOPES[rper * g + r], F32) for r in range(rper)], tq)
        sc = _dot_nt(group_q(g), kcb_ref[...]) - slope_rows * dist_c
        sc = jnp.where(valid_c, sc, NEG)
        mx = jnp.max(sc, axis=1, keepdims=True)
        e = jnp.where(valid_c, jnp.exp(sc - mx), 0.0)
        pc = e / jnp.maximum(jnp.sum(e, axis=1, keepdims=True), jnp.finfo(F32).tiny)
        tot_ref[g] = gate(g, 0) * _dot(pc.astype(BF16), vcb_ref[...])
        imp = pc[0:tq]
        for r in range(1, rper):
            imp = imp + pc[r * tq:(r + 1) * tq]
        score_t = jnp.where(allowed_t, jnp.where(forced_t, BIG_SCORE, imp.T), -1.0)
        not_sel = 1.0 - _topk_select_t(score_t, allowed_t, nb, NSA_TOPK).T
        for r in range(rper):
            slope = NSA_SLOPES[rper * g + r]
            alibi = jnp.where(lane == ALIBI_LANE, NSA_BLOCK * slope, jnp.where(lane == ALIBI_LANE + 1, slope, 0.0))
            lhs_ref[0, g, r * tq:(r + 1) * tq, LANES:] = jnp.where(lane < ALIBI_LANE, not_sel, alibi).astype(BF16)
            lhs_ref[1, g, r * tq:(r + 1) * tq, LANES:] = alibi.astype(BF16)
        lhs_ref[0, g, :, :LANES] = group_q(g)
        lhs_ref[1, g, :, :LANES] = group_q(g)

    m_ref[...] = jnp.full(m_ref.shape, NEG, F32)
    l_ref[...] = jnp.zeros_like(l_ref)
    acc_ref[...] = jnp.zeros_like(acc_ref)

    def sel_tile(kj, diag):
        k0 = pl.multiple_of(kj * tk, tk)
        rhs = jnp.concatenate([ks_ref[pl.ds(k0, tk), :], et_ref[pl.ds(k0, tk), :]], axis=1)
        vt = vs_ref[pl.ds(k0, tk), :]
        if diag:
            causal = (q0 - k0) + jnp.bitwise_and(_iota((rows, tk), 0), tq - 1) - _iota((rows, tk), 1) >= 0
        ss = [_dot_nt(lhs_ref[0, g], rhs) for g in range(NSA_KV)]
        for g in range(NSA_KV):
            s = ss[g]
            if diag:
                s = jnp.where(causal, s, NEG)
            _softmax_update(s, lambda p: _dot(p, vt), m_ref, l_ref, acc_ref, g)

    def sel_body(kj, c):
        sel_tile(kj, False)
        return c

    n_full = q0 // tk
    lax.fori_loop(0, n_full, sel_body, 0)
    sel_tile(n_full, True)

    span = win + tq
    w0 = pl.multiple_of(jnp.maximum(q0 - win, 0), tq)
    rhs_w = jnp.concatenate([kw_ref[pl.ds(w0, span), :], et_ref[pl.ds(w0, span), :]], axis=1)
    vw = vw_ref[pl.ds(w0, span), :]
    dist = (q0 - w0) + jnp.bitwise_and(_iota((rows, span), 0), tq - 1) - _iota((rows, span), 1)
    in_win = jnp.logical_and(dist >= 0, dist < win)
    for g in range(NSA_KV):
        s = jnp.where(in_win, _dot_nt(lhs_ref[1, g], rhs_w), NEG)
        p = jnp.exp(s - jnp.max(s, axis=1, keepdims=True))
        o_win = _dot(p.astype(BF16), vw) / jnp.sum(p, axis=1, keepdims=True)
        tot_ref[g] = tot_ref[g] + gate(g, 1) * (acc_ref[g] / l_ref[g]) + gate(g, 2) * o_win

    _nsa_output([[tot_ref[g, r * tq:(r + 1) * tq, :] for r in range(rper)] for g in range(NSA_KV)], o_ref)


def _nsa_prompt(p32, p16, wk_exp, wv_exp, key_table, B, S, tq, tk, win):
    nq = S // tq
    nbp = LANES
    W = NSA_HEADS * HEAD_DIM
    rper = NSA_HEADS // NSA_KV
    assert S >= win + tq
    kv = lambda c: pl.BlockSpec((S, LANES), lambda b, i: (b, C_NKV // LANES + c))
    full = lambda a: pl.BlockSpec(a.shape, lambda b, i: (0,) * a.ndim)
    return pl.pallas_call(
        functools.partial(_nsa_prompt_kernel, tq=tq, tk=tk, S=S, win=win),
        grid=(B, nq),
        in_specs=[pl.BlockSpec((tq, W), lambda b, i: (b * nq + i, C_NSAQ // W)),
                  kv(0), kv(1), kv(2), kv(3), kv(4), kv(5),
                  pl.BlockSpec((tq, LANES), lambda b, i: (b * nq + i, C_SMALL // LANES)),
                  full(wk_exp), full(wv_exp), full(key_table)],
        out_specs=pl.BlockSpec((tq, W), lambda b, i: (b * nq + i, 0)),
        out_shape=jax.ShapeDtypeStruct((B * S, W), F32),
        scratch_shapes=[pltpu.VMEM((nbp, LANES), BF16), pltpu.VMEM((nbp, LANES), BF16),
                        pltpu.VMEM((NSA_KV, rper, tq, LANES), BF16),
                        pltpu.VMEM((2, NSA_KV, rper * tq, 2 * LANES), BF16),
                        pltpu.VMEM((NSA_KV, rper * tq, LANES), F32),
                        pltpu.VMEM((NSA_KV, rper * tq, LANES), F32),
                        pltpu.VMEM((NSA_KV, rper * tq, LANES), F32),
                        pltpu.VMEM((NSA_KV, rper * tq, LANES), F32)],
        compiler_params=_cparams(("parallel", "arbitrary"), 56),
        name="nsa_prompt",
    )(p32, p16, p16, p16, p16, p16, p16, p32, wk_exp, wv_exp, key_table)


def _blockdiag_queries(q_ref, qbd_ref, n_heads, n_kv):
    sd = q_ref.shape[0]
    rper = n_heads // n_kv
    width = n_kv * HEAD_DIM
    low = _iota((sd, LANES), 1) < HEAD_DIM
    for h in range(n_heads):
        g = h // rper
        blk = q_ref[:, (h // 2) * LANES:(h // 2 + 1) * LANES]
        if h % 2 != g % 2:
            blk = _swap_halves(blk)
        keep = low if g % 2 == 0 else jnp.logical_not(low)
        blk = jnp.where(keep, blk, 0.0)
        parts = [blk if c == g // 2 else jnp.zeros((sd, LANES), F32) for c in range(width // LANES)]
        row = parts[0] if len(parts) == 1 else jnp.concatenate(parts, axis=1)
        qbd_ref[h * sd:(h + 1) * sd, :] = row.astype(BF16)


def _blockdiag_output(acc, o_ref, n_heads, n_kv, sd):
    rper = n_heads // n_kv
    low = _iota((sd, LANES), 1) < HEAD_DIM
    for j in range(n_heads // 2):
        pieces = []
        for h in (2 * j, 2 * j + 1):
            g = h // rper
            p = acc[h * sd:(h + 1) * sd, (g // 2) * LANES:(g // 2 + 1) * LANES]
            if g % 2 != h % 2:
                p = _swap_halves(p)
            pieces.append(p)
        o_ref[:, j * LANES:(j + 1) * LANES] = jnp.where(low, pieces[0], pieces[1])


def _pad_rows(x, rows):
    return jnp.concatenate([x, jnp.zeros((rows - x.shape[0], x.shape[1]), x.dtype)], axis=0)


def _page_specs(n, shape, index_of):
    return [pl.BlockSpec(shape, index_of(u)) for u in range(n)]


def _expand_heads(x8, sd):
    return jnp.concatenate([jnp.broadcast_to(x8[h:h + 1, :], (sd, x8.shape[1])) for h in range(x8.shape[0])], axis=0)


def _transposed_pool(pool):
    d, n, page, kv, hd = pool.shape
    return pool.transpose(0, 1, 3, 4, 2).reshape(d, n, kv * hd, page)


def _fox_decode_kernel(pt_ref, q_ref, kn_ref, vn_ref, sm_ref, *rest, sd, n_steps, pp):
    del pt_ref
    kt_refs, vt_refs, lf_refs = rest[:pp], rest[pp:2 * pp], rest[2 * pp:3 * pp]
    o_ref, lfo_ref, qbd_ref, m_ref, l_ref, acc_ref, cd_ref = rest[3 * pp:]
    p = pl.program_id(1)
    rows = FOX_HEADS * sd
    uo = _suffix_matrix()

    @pl.when(p == 0)
    def _():
        _blockdiag_queries(q_ref, qbd_ref, FOX_HEADS, FOX_KV)
        head_of_row = lax.shift_right_logical(_iota((rows, LANES), 0), 3)
        i_of_row = jnp.bitwise_and(_iota((rows, LANES), 0), sd - 1)
        col = _iota((rows, LANES), 1)
        lf = _log_sigmoid(sm_ref[...])
        lfo_ref[...] = lf[:, :FOX_HEADS]
        sel_h = jnp.where(col == head_of_row, 1.0, 0.0).astype(BF16)
        a = _dot_nt_exact_rhs(sel_h, _pad_rows(lf, LANES))
        incl = (_iota((LANES, LANES), 0) <= _iota((LANES, LANES), 1)).astype(BF16)
        bias_new = -_dot_exact(a, incl)
        kn = _pad_rows(kn_ref[...], LANES).astype(BF16)
        vn = _pad_rows(vn_ref[...], LANES).astype(BF16)
        s = _dot_nt(qbd_ref[...], kn) + bias_new
        s = jnp.where(col <= i_of_row, s, NEG)
        mx = jnp.max(s, axis=1, keepdims=True)
        pr = jnp.exp(s - mx)
        m_ref[...] = jnp.broadcast_to(mx, m_ref.shape)
        l_ref[...] = jnp.broadcast_to(jnp.sum(pr, axis=1, keepdims=True), l_ref.shape)
        acc_ref[...] = _dot(pr.astype(BF16), vn)
        cd_ref[...] = jnp.zeros_like(cd_ref)

    later = cd_ref[...]
    qbd = qbd_ref[...]
    parts = []
    for u in range(pp):
        r = _dot_exact(lf_refs[u][0, 0], uo)
        bias = _expand_heads(r[:, :LANES] + later, sd)
        later = later + r[:, LANES:]
        parts.append(_dot(qbd, kt_refs[u][0, 0].astype(BF16)) + bias)
    cd_ref[...] = later
    s = parts[0] if pp == 1 else jnp.concatenate(parts, axis=1)

    def pv(pr):
        out = _dot_nt(pr[:, :LANES], vt_refs[0][0, 0].astype(BF16))
        for u in range(1, pp):
            out = out + _dot_nt(pr[:, u * LANES:(u + 1) * LANES], vt_refs[u][0, 0].astype(BF16))
        return out

    _softmax_update(s, pv, m_ref, l_ref, acc_ref)

    @pl.when(p == n_steps - 1)
    def _():
        acc = acc_ref[...] / _rep(l_ref[...], acc_ref.shape[1] // LANES)
        _blockdiag_output(acc, o_ref, FOX_HEADS, FOX_KV, sd)


def _fox_decode(p32, page_table, pool_k, pool_v, pool_lf, layer, row0, DB, sd, pp):
    n_pages = page_table.shape[1]
    n_steps = n_pages // pp
    page = pool_k.shape[2]
    Wq, Wk = FOX_HEADS * HEAD_DIM, FOX_KV * HEAD_DIM
    rows = FOX_HEADS * sd
    rb = row0 // sd
    pkt, pvt = _transposed_pool(pool_k), _transposed_pool(pool_v)
    plf = pool_lf.transpose(0, 1, 3, 2)
    pg = lambda u: (lambda b, p, pt: (layer, pt[b, n_pages - 1 - (p * pp + u)], 0, 0))
    grid_spec = pltpu.PrefetchScalarGridSpec(
        num_scalar_prefetch=1,
        grid=(DB, n_steps),
        in_specs=[pl.BlockSpec((sd, Wq), lambda b, p, pt: (rb + b, C_FOXQ // Wq)),
                  pl.BlockSpec((sd, Wk), lambda b, p, pt: (rb + b, C_FOXK // Wk)),
                  pl.BlockSpec((sd, Wk), lambda b, p, pt: (rb + b, C_FOXV // Wk)),
                  pl.BlockSpec((sd, LANES), lambda b, p, pt: (rb + b, C_SMALL // LANES))]
        + _page_specs(pp, (1, 1, Wk, page), pg) + _page_specs(pp, (1, 1, Wk, page), pg)
        + _page_specs(pp, (1, 1, FOX_HEADS, page), pg),
        out_specs=[pl.BlockSpec((sd, Wq), lambda b, p, pt: (b, 0)),
                   pl.BlockSpec((sd, FOX_HEADS), lambda b, p, pt: (b, 0))],
        scratch_shapes=[pltpu.VMEM((rows, Wk), BF16), pltpu.VMEM((rows, LANES), F32),
                        pltpu.VMEM((rows, LANES), F32), pltpu.VMEM((rows, Wk), F32),
                        pltpu.VMEM((FOX_HEADS, LANES), F32)])
    return pl.pallas_call(
        functools.partial(_fox_decode_kernel, sd=sd, n_steps=n_steps, pp=pp),
        grid_spec=grid_spec,
        out_shape=[jax.ShapeDtypeStruct((DB * sd, Wq), F32), jax.ShapeDtypeStruct((DB * sd, FOX_HEADS), F32)],
        compiler_params=_cparams(("parallel", "arbitrary")),
        name="fox_decode",
    )(page_table, p32, p32, p32, p32, *([pkt] * pp), *([pvt] * pp), *([plf] * pp))


def _sb_decode_kernel(pt_ref, q_ref, kn_ref, vn_ref, *rest, sd, n_steps, pp):
    del pt_ref
    kt_refs, vt_refs = rest[:pp], rest[pp:2 * pp]
    o_ref, qbd_ref, carry_ref, acc_ref, top_ref = rest[2 * pp:]
    p = pl.program_id(1)
    rows = SB_HEADS * sd
    uo = _suffix_matrix()

    @pl.when(p == 0)
    def _():
        _blockdiag_queries(q_ref, qbd_ref, SB_HEADS, SB_KV)
        i_of_row = jnp.bitwise_and(_iota((rows, LANES), 0), sd - 1)
        strict = _iota((rows, LANES), 1) < i_of_row
        kn = _pad_rows(kn_ref[...], LANES).astype(BF16)
        vn = _pad_rows(vn_ref[...], LANES).astype(BF16)
        ws, c = _sb_blocks([_dot_nt(qbd_ref[...], kn)], jnp.zeros((rows, LANES), F32), uo, [strict])
        acc_ref[...] = _dot(ws[0].astype(BF16), vn)
        carry_ref[...] = c
        top_ref[0] = jnp.max(c)

    @pl.when(top_ref[0] > EXP_ZERO)
    def _():
        qbd = qbd_ref[...]
        zs = [_dot(qbd, kt_refs[u][0, 0].astype(BF16)) for u in range(pp)]
        ws, c = _sb_blocks(zs, carry_ref[...], uo, [None] * pp)
        acc = acc_ref[...]
        for u in range(pp):
            acc = acc + _dot_nt(ws[u].astype(BF16), vt_refs[u][0, 0].astype(BF16))
        acc_ref[...] = acc
        carry_ref[...] = c
        top_ref[0] = jnp.max(c)

    @pl.when(p == n_steps - 1)
    def _():
        _blockdiag_output(acc_ref[...], o_ref, SB_HEADS, SB_KV, sd)


def _sb_decode(p32, page_table, pool_k, pool_v, layer, row0, DB, sd, pp):
    n_pages = page_table.shape[1]
    n_steps = n_pages // pp
    page = pool_k.shape[2]
    Wq, Wk = SB_HEADS * HEAD_DIM, SB_KV * HEAD_DIM
    rows = SB_HEADS * sd
    rb = row0 // sd
    pkt, pvt = _transposed_pool(pool_k), _transposed_pool(pool_v)
    pg = lambda u: (lambda b, p, pt: (layer, pt[b, n_pages - 1 - (p * pp + u)], 0, 0))
    grid_spec = pltpu.PrefetchScalarGridSpec(
        num_scalar_prefetch=1,
        grid=(DB, n_steps),
        in_specs=[pl.BlockSpec((sd, Wq), lambda b, p, pt: (rb + b, C_SBQ // Wq)),
                  pl.BlockSpec((sd, Wk), lambda b, p, pt: (rb + b, C_SBK // Wk)),
                  pl.BlockSpec((sd, Wk), lambda b, p, pt: (rb + b, C_SBV // Wk))]
        + _page_specs(pp, (1, 1, Wk, page), pg) + _page_specs(pp, (1, 1, Wk, page), pg),
        out_specs=pl.BlockSpec((sd, Wq), lambda b, p, pt: (b, 0)),
        scratch_shapes=[pltpu.VMEM((rows, Wk), BF16), pltpu.VMEM((rows, LANES), F32),
                        pltpu.VMEM((rows, Wk), F32), pltpu.SMEM((1,), F32)])
    return pl.pallas_call(
        functools.partial(_sb_decode_kernel, sd=sd, n_steps=n_steps, pp=pp),
        grid_spec=grid_spec,
        out_shape=jax.ShapeDtypeStruct((DB * sd, Wq), F32),
        compiler_params=_cparams(("parallel", "arbitrary")),
        name="sb_decode",
    )(page_table, p32, p32, p32, *([pkt] * pp), *([pvt] * pp))


def _nsa_decode_slopes(rows, g, width=LANES):
    rper = NSA_HEADS // NSA_KV
    rr = lax.shift_right_logical(_iota((rows, width), 0), 3)
    s = jnp.full((rows, width), NSA_SLOPES[rper * g], F32)
    for r in range(1, rper):
        s = jnp.where(rr == r, NSA_SLOPES[rper * g + r], s)
    return s


def _nsa_decode_group_q(qg_ref, g):
    rper = NSA_HEADS // NSA_KV
    return jnp.concatenate([qg_ref[g, r] for r in range(rper)], axis=0).astype(BF16)


def _nsa_decode_pool_kernel(pt_ref, q_ref, sm_ref, wk_ref, wv_ref, *rest, sd, n_steps, pp, past):
    del pt_ref
    kc_refs, vc_refs = rest[:pp], rest[pp:2 * pp]
    tot_ref, selr_ref, kcb_ref, vcb_ref, qg_ref = rest[2 * pp:]
    j = pl.program_id(1)
    rper = NSA_HEADS // NSA_KV
    rows = rper * sd
    nbc = past // NSA_BLOCK
    page = kc_refs[0].shape[3]
    bpp = page // NSA_BLOCK
    i_row = jnp.bitwise_and(_iota((rows, LANES), 0), sd - 1)
    col = _iota((rows, LANES), 1)
    t_row = past + i_row
    slopes = functools.partial(_nsa_decode_slopes, rows)
    group_q = functools.partial(_nsa_decode_group_q, qg_ref)

    @pl.when(j == 0)
    def _():
        kcb_ref[...] = jnp.zeros_like(kcb_ref)
        vcb_ref[...] = jnp.zeros_like(vcb_ref)

    kacc = kcb_ref[...]
    vacc = vcb_ref[...]
    pos_blk = lax.shift_right_logical(_iota((LANES, LANES), 0), 6)
    n_col = _iota((LANES, LANES), 1)
    for u in range(pp):
        ind = (n_col == (j * pp + u) * bpp + pos_blk).astype(BF16)
        pk = kc_refs[u][0, 0] * wk_ref[...]
        pv_ = vc_refs[u][0, 0] * wv_ref[...]
        kh = pk.astype(BF16)
        vh = pv_.astype(BF16)
        stack = jnp.concatenate([kh, (pk - kh.astype(F32)).astype(BF16),
                                 vh, (pv_ - vh.astype(F32)).astype(BF16)], axis=0)
        r = _dot(stack, ind)
        kacc = kacc + r[0:LANES] + r[LANES:2 * LANES]
        vacc = vacc + r[2 * LANES:3 * LANES] + r[3 * LANES:4 * LANES]
    kcb_ref[...] = kacc
    vcb_ref[...] = vacc

    @pl.when(j == n_steps - 1)
    def _():
        _nsa_group_queries(q_ref, qg_ref, sd)
        sig = jax.nn.sigmoid(sm_ref[...])
        blk_end = (col + 1) * NSA_BLOCK - 1
        dist_c = (t_row - blk_end).astype(F32)
        valid_c = jnp.logical_and(blk_end <= t_row, col < nbc)
        kcb = kcb_ref[...].astype(BF16)
        vcb = vcb_ref[...].astype(BF16)
        n_i = _iota((sd, LANES), 1)
        allowed = n_i < nbc
        forced = jnp.logical_or(n_i == 0, n_i == nbc - 1)
        m_idx = _iota((LANES, LANES), 0)
        n_idx = _iota((LANES, LANES), 1)
        for g in range(NSA_KV):
            sc = _dot(group_q(g), kcb) - slopes(g) * dist_c
            sc = jnp.where(valid_c, sc, NEG)
            mx = jnp.max(sc, axis=1, keepdims=True)
            e = jnp.where(valid_c, jnp.exp(sc - mx), 0.0)
            pc = e / jnp.maximum(jnp.sum(e, axis=1, keepdims=True), jnp.finfo(F32).tiny)
            gate_c = jnp.concatenate(
                [jnp.broadcast_to(sig[:, GATE_LANE0 + 3 * (rper * g + r):GATE_LANE0 + 3 * (rper * g + r) + 1],
                                  (sd, LANES)) for r in range(rper)], axis=0)
            tot_ref[0, g] = gate_c * _dot_nt(pc.astype(BF16), vcb)
            imp = pc[0:sd]
            for r in range(1, rper):
                imp = imp + pc[r * sd:(r + 1) * sd]
            score = jnp.where(allowed, jnp.where(forced, BIG_SCORE, imp), -1.0)
            score_t = _pad_rows(score, LANES).T
            sel_rows = []
            for i in range(sd):
                cm = score_t[:, i:i + 1]
                rn = score[i:i + 1, :]
                beats = jnp.where(n_idx > m_idx, jnp.where(cm >= rn, 1.0, 0.0), jnp.where(cm > rn, 1.0, 0.0))
                cnt = jnp.sum(beats, axis=0, keepdims=True)
                sel_rows.append(jnp.where(jnp.logical_and(cnt < NSA_TOPK - 1, allowed[i:i + 1, :]), 1.0, 0.0))
            sel = jnp.concatenate(sel_rows, axis=0)
            selr_ref[0, g] = jnp.concatenate([sel] * rper, axis=0).astype(BF16)


def _nsa_decode_attend_kernel(pt_ref, q_ref, ksn_ref, vsn_ref, kwn_ref, vwn_ref, sm_ref, kws_ref, vws_ref,
                              tot_ref, selr_ref, *rest, sd, n_steps, pp, past):
    del pt_ref
    ks_refs, vs_refs = rest[:pp], rest[pp:2 * pp]
    o_ref, qg_ref, m_ref, l_ref, acc_ref = rest[2 * pp:]
    j = pl.program_id(1)
    rper = NSA_HEADS // NSA_KV
    rows = rper * sd
    page = ks_refs[0].shape[3]
    bpp = page // NSA_BLOCK
    win = kws_ref.shape[3]
    i_row = jnp.bitwise_and(_iota((rows, LANES), 0), sd - 1)
    col = _iota((rows, LANES), 1)
    slopes = functools.partial(_nsa_decode_slopes, rows)
    group_q = functools.partial(_nsa_decode_group_q, qg_ref)

    @pl.when(j == 0)
    def _():
        _nsa_group_queries(q_ref, qg_ref, sd)
        m_ref[...] = jnp.full(m_ref.shape, NEG, F32)
        l_ref[...] = jnp.zeros_like(l_ref)
        acc_ref[...] = jnp.zeros_like(acc_ref)

    def selection_step():
        p0 = j * pp
        wide = pp * LANES
        colw = _iota((rows, wide), 1)
        iw = jnp.bitwise_and(_iota((rows, wide), 0), sd - 1)
        dist = (past + iw - (p0 * page + colw)).astype(F32)
        e_p = (_iota((LANES, wide), 0) ==
               p0 * bpp + lax.shift_right_logical(_iota((LANES, wide), 1), 6)).astype(BF16)
        for g in range(NSA_KV):
            qg = group_q(g)
            qk = [_dot(qg, ks_refs[u][0, 0].astype(BF16)) for u in range(pp)]
            s = qk[0] if pp == 1 else jnp.concatenate(qk, axis=1)
            selb = (_dot(selr_ref[0, g], e_p) - 1.0) * (-NEG)
            s = s - slopes(g, wide) * dist + selb

            def pv(pr):
                out = _dot_nt(pr[:, :LANES], vs_refs[0][0, 0].astype(BF16))
                for u in range(1, pp):
                    out = out + _dot_nt(pr[:, u * LANES:(u + 1) * LANES], vs_refs[u][0, 0].astype(BF16))
                return out

            _softmax_update(s, pv, m_ref, l_ref, acc_ref, g)

    selection_step()

    @pl.when(j == n_steps - 1)
    def _():
        sig = jax.nn.sigmoid(sm_ref[...])

        def gate(g, branch):
            return jnp.concatenate(
                [jnp.broadcast_to(sig[:, GATE_LANE0 + 3 * (rper * g + r) + branch:
                                      GATE_LANE0 + 3 * (rper * g + r) + branch + 1], (sd, LANES))
                 for r in range(rper)], axis=0)

        new_ok = col <= i_row
        dist_new = (i_row - col).astype(F32)
        ksn = _pad_rows(ksn_ref[...], LANES).astype(BF16)
        vsn = _pad_rows(vsn_ref[...], LANES).astype(BF16)
        kwn = _pad_rows(kwn_ref[...], LANES).astype(BF16)
        vwn = _pad_rows(vwn_ref[...], LANES).astype(BF16)
        kws = kws_ref[0, 0].astype(BF16)
        vws = vws_ref[0, 0].astype(BF16)
        colw = _iota((rows, win), 1)
        iw = jnp.bitwise_and(_iota((rows, win), 0), sd - 1)
        dist_w = (win + iw - colw).astype(F32)
        ok_w = colw > iw
        tots = []
        for g in range(NSA_KV):
            qg = group_q(g)
            sl = slopes(g)
            s = jnp.where(new_ok, _dot_nt(qg, ksn) - sl * dist_new, NEG)
            _softmax_update(s, lambda pr: _dot(pr, vsn), m_ref, l_ref, acc_ref, g)
            o_sel = acc_ref[g] / l_ref[g]
            s1 = jnp.where(ok_w, _dot(qg, kws) - slopes(g, win) * dist_w, NEG)
            s2 = jnp.where(new_ok, _dot_nt(qg, kwn) - sl * dist_new, NEG)
            mx = jnp.maximum(jnp.max(s1, axis=1, keepdims=True), jnp.max(s2, axis=1, keepdims=True))
            p1 = jnp.exp(s1 - mx)
            p2 = jnp.exp(s2 - mx)
            den = jnp.sum(p1, axis=1, keepdims=True) + jnp.sum(p2, axis=1, keepdims=True)
            o_win = (_dot_nt(p1.astype(BF16), vws) + _dot(p2.astype(BF16), vwn)) / den
            t = tot_ref[0, g] + gate(g, 1) * o_sel + gate(g, 2) * o_win
            tots.append([t[r * sd:(r + 1) * sd] for r in range(rper)])
        _nsa_output(tots, o_ref)


def _nsa_decode(p32, page_table, pool_kc, pool_vc, pool_ks, pool_vs, st_kw, st_vw, wk_t, wv_t,
                layer, row0, DB, sd, pp):
    n_pages = page_table.shape[1]
    n_steps = n_pages // pp
    page = pool_kc.shape[2]
    past = n_pages * page
    W = NSA_HEADS * HEAD_DIM
    rper = NSA_HEADS // NSA_KV
    rows = rper * sd
    assert past // NSA_BLOCK <= LANES and past % NSA_BLOCK == 0 and sd <= NSA_BLOCK
    rb = row0 // sd
    win = st_kw.shape[2]
    new = lambda c: pl.BlockSpec((sd, LANES), lambda b, j, pt: (rb + b, C_NKV // LANES + c))
    pg = lambda u: (lambda b, j, pt: (layer, pt[b, j * pp + u], 0, 0))
    pages = _page_specs(pp, (1, 1, LANES, page), pg)
    q_spec = pl.BlockSpec((sd, W), lambda b, j, pt: (rb + b, C_NSAQ // W))
    sm_spec = pl.BlockSpec((sd, LANES), lambda b, j, pt: (rb + b, C_SMALL // LANES))
    state = pl.BlockSpec((1, 1, LANES, win), lambda b, j, pt: (layer, b, 0, 0))
    per_seq = pl.BlockSpec((1, NSA_KV, rows, LANES), lambda b, j, pt: (b, 0, 0, 0))
    tp = _transposed_pool
    tot_c, sel_rows = pl.pallas_call(
        functools.partial(_nsa_decode_pool_kernel, sd=sd, n_steps=n_steps, pp=pp, past=past),
        grid_spec=pltpu.PrefetchScalarGridSpec(
            num_scalar_prefetch=1,
            grid=(DB, n_steps),
            in_specs=[q_spec, sm_spec,
                      pl.BlockSpec(wk_t.shape, lambda b, j, pt: (0, 0)),
                      pl.BlockSpec(wv_t.shape, lambda b, j, pt: (0, 0))] + pages + pages,
            out_specs=[per_seq, per_seq],
            scratch_shapes=[pltpu.VMEM((LANES, LANES), F32), pltpu.VMEM((LANES, LANES), F32),
                            pltpu.VMEM((NSA_KV, rper, sd, LANES), F32)]),
        out_shape=[jax.ShapeDtypeStruct((DB, NSA_KV, rows, LANES), F32),
                   jax.ShapeDtypeStruct((DB, NSA_KV, rows, LANES), BF16)],
        compiler_params=_cparams(("parallel", "arbitrary")),
        name="nsa_decode_pool",
    )(page_table, p32, p32, wk_t, wv_t, *([tp(pool_kc)] * pp), *([tp(pool_vc)] * pp))
    return pl.pallas_call(
        functools.partial(_nsa_decode_attend_kernel, sd=sd, n_steps=n_steps, pp=pp, past=past),
        grid_spec=pltpu.PrefetchScalarGridSpec(
            num_scalar_prefetch=1,
            grid=(DB, n_steps),
            in_specs=[q_spec, new(2), new(3), new(4), new(5), sm_spec, state, state, per_seq, per_seq]
            + pages + pages,
            out_specs=pl.BlockSpec((sd, W), lambda b, j, pt: (b, 0)),
            scratch_shapes=[pltpu.VMEM((NSA_KV, rper, sd, LANES), F32),
                            pltpu.VMEM((NSA_KV, rows, LANES), F32), pltpu.VMEM((NSA_KV, rows, LANES), F32),
                            pltpu.VMEM((NSA_KV, rows, LANES), F32)]),
        out_shape=jax.ShapeDtypeStruct((DB * sd, W), F32),
        compiler_params=_cparams(("parallel", "arbitrary")),
        name="nsa_decode_attend",
    )(page_table, p32, p32, p32, p32, p32, p32, tp(st_kw), tp(st_vw), tot_c, sel_rows,
      *([tp(pool_ks)] * pp), *([tp(pool_vs)] * pp))


def _prep_in_weights(w_in, b_in):
    segs = {}
    off = 0
    for name, n in (("fox_q", 512), ("fox_k", 256), ("fox_v", 256), ("fox_f", 8), ("sb_q", 512), ("sb_k", 256),
                    ("sb_v", 256), ("nsa_q", 512), ("nsa_kv", 768), ("nsa_g", 24), ("z", 1536), ("merge", 3072)):
        segs[name] = (off, off + n)
        off += n
    assert off == w_in.shape[-1]
    scale = HEAD_DIM ** -0.5

    def build(a):
        sl = lambda nm: a[..., segs[nm][0]:segs[nm][1]]
        pad = jnp.zeros(a.shape[:-1] + (LANES - 32,), a.dtype)
        return jnp.concatenate([sl("fox_q") * scale, sl("fox_k"), sl("fox_v"), sl("sb_q") * scale, sl("sb_k"),
                                sl("sb_v"), sl("nsa_q") * scale, sl("z"), sl("merge"), sl("nsa_kv"),
                                sl("fox_f"), sl("nsa_g"), pad], axis=-1)
    wp = build(w_in).astype(BF16)
    bp = build(b_in)[:, None, :]
    assert wp.shape[-1] == N_PAD
    return wp, bp


def kernel(x_prompt, x_sample, cache_fox_k, cache_fox_v, cache_fox_logf, cache_sb_k, cache_sb_v, cache_nsa_kc, cache_nsa_vc, cache_nsa_ks, cache_nsa_vs, state_nsa_kw, state_nsa_vw, page_table, norm_g, w_in, b_in, w_cmp_k, w_cmp_v, w_br_fox, w_br_sb, w_br_nsa, w_out, final_norm_g):
    B, S, D = x_prompt.shape
    DB, SD, _ = x_sample.shape
    depth = w_in.shape[0]
    page = cache_fox_k.shape[2]
    n_pages = page_table.shape[1]
    win = state_nsa_kw.shape[2]
    tq = LANES
    tk_fox = _tile(S, 512, LANES)
    tk_sb = _tile(S, 256, LANES)
    pp_sb = _tile(n_pages, 8, 1)
    pp_fox = _tile(n_pages, 32, 1)
    pp_nsa = _tile(n_pages, 64, 1)
    assert cache_fox_k.shape[3:] == (FOX_KV, HEAD_DIM) and cache_fox_logf.shape[3] == FOX_HEADS
    assert cache_sb_k.shape[3:] == (SB_KV, HEAD_DIM) and cache_nsa_kc.shape[3:] == (NSA_KV, HEAD_DIM)
    assert w_br_fox.shape[1] == FOX_HEADS * HEAD_DIM and w_br_sb.shape[1] == SB_HEADS * HEAD_DIM
    assert w_br_nsa.shape[1] == NSA_HEADS * HEAD_DIM and w_cmp_k.shape[1:] == (NSA_BLOCK, NSA_KV)
    assert SD == 8 and page == LANES and S % tq == 0 and S >= win and n_pages * page >= win
    TP, TS = B * S, DB * SD
    assert TP % SD == 0

    x = jnp.concatenate([x_prompt.reshape(TP, D), x_sample.reshape(TS, D)], axis=0)
    wp, bp = _prep_in_weights(w_in, b_in)
    wbr = jnp.stack([w_br_fox, w_br_sb, w_br_nsa], axis=1).astype(BF16)
    wout = w_out.astype(BF16)
    wk_exp = jnp.repeat(w_cmp_k, HEAD_DIM, axis=2)
    wv_exp = jnp.repeat(w_cmp_v, HEAD_DIM, axis=2)
    bpp = page // NSA_BLOCK
    key_table = _nsa_key_table(S)
    kv_bufs = [jnp.zeros((depth, B, kv * HEAD_DIM, S), F32) for _, kv in KV_LEAVES]

    h = _rmsnorm(x, norm_g[0], BF16)
    logf_p_all, rows_s = [], []
    for l in range(depth):
        p32, p16 = _inproj(h, wp[l], bp[l])
        logf_p, ft = _foxgate(p32, B, S)
        ft4 = ft.reshape(B, FOX_KV // 2, 4, S // tk_fox, tk_fox).transpose(0, 1, 3, 2, 4)
        of_p = _fox_prompt(p32, p16, ft4, B, S, tq, tk_fox)
        os_p = _sb_prompt(p32, p16, B, S, tq, tk_sb)
        on_p = _nsa_prompt(p32, p16, wk_exp[l], wv_exp[l], key_table, B, S, tq, tk_fox, win)
        kv_bufs = _kv_rows(p32, kv_bufs, l, B, S)
        logf_p_all.append(logf_p.reshape(B, S, FOX_HEADS))
        of_s, logf_s = _fox_decode(p32, page_table, cache_fox_k, cache_fox_v, cache_fox_logf, l, TP, DB, SD, pp_fox)
        os_s = _sb_decode(p32, page_table, cache_sb_k, cache_sb_v, l, TP, DB, SD, pp_sb)
        wk_t = jnp.tile(wk_exp[l].T, (1, bpp))
        wv_t = jnp.tile(wv_exp[l].T, (1, bpp))
        on_s = _nsa_decode(p32, page_table, cache_nsa_kc, cache_nsa_vc, cache_nsa_ks, cache_nsa_vs,
                           state_nsa_kw, state_nsa_vw, wk_t, wv_t, l, TP, DB, SD, pp_nsa)
        last = l + 1 == depth
        g_next = final_norm_g if last else norm_g[l + 1]
        x, h = _merge((of_p, os_p, on_p), (of_s, os_s, on_s), p32, x, wbr[l], wout[l], g_next,
                      F32 if last else BF16)

        rs = [p32[TP:, c0:c0 + kv * HEAD_DIM].reshape(DB, SD, kv, HEAD_DIM) for c0, kv in KV_LEAVES]
        rs[8] = jnp.concatenate([state_nsa_kw[l], rs[8]], axis=1)[:, SD:]
        rs[9] = jnp.concatenate([state_nsa_vw[l], rs[9]], axis=1)[:, SD:]
        rows_s.append(rs[:2] + [logf_s.reshape(DB, SD, FOX_HEADS)] + rs[2:])

    y_prompt = h[:TP].reshape(B, S, D)
    y_sample = h[TP:].reshape(DB, SD, D)
    kv_p = [buf.reshape(depth, B, kv, HEAD_DIM, S).transpose(0, 1, 4, 2, 3) for buf, (_, kv) in zip(kv_bufs, KV_LEAVES)]
    kv_p[8], kv_p[9] = kv_p[8][:, :, S - win:], kv_p[9][:, :, S - win:]
    outs_p = kv_p[:2] + [jnp.stack(logf_p_all)] + kv_p[2:]
    outs_s = [jnp.stack([r[i] for r in rows_s]) for i in range(11)]
    return (y_prompt, y_sample, *outs_p, *outs_s)
```

```python
import functools
import math

import numpy as np
import jax
import jax.numpy as jnp
from jax import lax
from jax.experimental import pallas as pl
from jax.experimental.pallas import tpu as pltpu

F32 = jnp.float32
BF16 = jnp.bfloat16

HEAD_DIM = 64
FOX_HEADS, FOX_KV = 8, 4
SB_HEADS, SB_KV = 8, 4
NSA_HEADS, NSA_KV = 8, 2
NSA_BLOCK = 64
NSA_TOPK = 16
RMS_EPS = 1e-6
LANES = 128
NEG = -1e30
BIG_SCORE = 8.0
EXP_ZERO = -104.0

C_FOXQ, C_FOXK, C_FOXV = 0, 512, 768
C_SBQ, C_SBK, C_SBV = 1024, 1536, 1792
C_NSAQ = 2048
C_ZF, C_ZS, C_ZN = 2560, 3072, 3584
C_MERGE = 4096
C_NKV = 7168
C_SMALL = 7936
N_PAD = 8064
GATE_LANE0 = 8
ALIBI_LANE = 64

NSA_SLOPES = tuple(float(2.0 ** (-8.0 * (i + 1) / NSA_HEADS)) for i in range(NSA_HEADS))


def _tile(n, target, mult):
    best = None
    for d in range(mult, min(n, target) + 1, mult):
        if n % d == 0:
            best = d
    return best if best is not None else n


def _cparams(sem, vmem_mb=48):
    return pltpu.CompilerParams(dimension_semantics=sem, vmem_limit_bytes=vmem_mb * 1024 * 1024)


def _dot(a, b):
    return jnp.dot(a, b, preferred_element_type=F32)


def _dot_nt(a, b):
    return lax.dot_general(a, b, (((1,), (1,)), ((), ())), preferred_element_type=F32)


def _split3(x):
    hi = x.astype(BF16)
    r = x - hi.astype(F32)
    mid = r.astype(BF16)
    lo = (r - mid.astype(F32)).astype(BF16)
    return hi, mid, lo


def _dot_exact(x, m01):
    hi, mid, lo = _split3(x)
    return _dot(hi, m01) + _dot(mid, m01) + _dot(lo, m01)


def _dot_nt_exact_rhs(m01, x):
    hi, mid, lo = _split3(x)
    return _dot_nt(m01, hi) + _dot_nt(m01, mid) + _dot_nt(m01, lo)


def _log_sigmoid(x):
    return jnp.minimum(x, 0.0) - jnp.log1p(jnp.exp(-jnp.abs(x)))


def _softplus(x):
    return jnp.maximum(x, 0.0) + jnp.log1p(jnp.exp(-jnp.abs(x)))


def _iota(shape, dim):
    return lax.broadcasted_iota(jnp.int32, shape, dim)


def _swap_halves(x):
    return pltpu.roll(x, 64, 1)


def _rmsnorm_body(x, g):
    return x * lax.rsqrt(jnp.mean(x * x, axis=-1, keepdims=True) + RMS_EPS) * g


def _rmsnorm_kernel(x_ref, g_ref, o_ref):
    o_ref[...] = _rmsnorm_body(x_ref[...], g_ref[...]).astype(o_ref.dtype)


def _rmsnorm(x, g, out_dtype):
    T, D = x.shape
    tm = _tile(T, 1024, 16)
    return pl.pallas_call(
        _rmsnorm_kernel,
        grid=(T // tm,),
        in_specs=[pl.BlockSpec((tm, D), lambda i: (i, 0)), pl.BlockSpec((1, D), lambda i: (0, 0))],
        out_specs=pl.BlockSpec((tm, D), lambda i: (i, 0)),
        out_shape=jax.ShapeDtypeStruct((T, D), out_dtype),
        compiler_params=_cparams(("parallel",)),
        name="rmsnorm",
    )(x, g.reshape(1, D))


def _inproj_kernel(h_ref, w_ref, b_ref, o32_ref, o16_ref):
    acc = _dot(h_ref[...], w_ref[...]) + b_ref[...]
    o32_ref[...] = acc
    o16_ref[...] = acc.astype(BF16)


def _inproj(h, w, b):
    T, D = h.shape
    N = w.shape[1]
    tm = _tile(T, 1280, 16)
    tn = _tile(N, 1152, 128)
    return pl.pallas_call(
        _inproj_kernel,
        grid=(N // tn, T // tm),
        in_specs=[pl.BlockSpec((tm, D), lambda j, i: (i, 0)),
                  pl.BlockSpec((D, tn), lambda j, i: (0, j)),
                  pl.BlockSpec((1, tn), lambda j, i: (0, j))],
        out_specs=[pl.BlockSpec((tm, tn), lambda j, i: (i, j)),
                   pl.BlockSpec((tm, tn), lambda j, i: (i, j))],
        out_shape=[jax.ShapeDtypeStruct((T, N), F32), jax.ShapeDtypeStruct((T, N), BF16)],
        compiler_params=_cparams(("parallel", "parallel")),
        name="inproj",
    )(h, w, b)


def _merge_kernel(ofp_ref, osp_ref, onp_ref, ofs_ref, oss_ref, ons_ref, zf_ref, zs_ref, zn_ref,
                  g0_ref, g1_ref, g2_ref, x_ref, wbr_ref, wout_ref, gn_ref, xo_ref, ho_ref, *, n_prompt_tiles):
    is_prompt = pl.program_id(0) < n_prompt_tiles

    def branch(op_ref, os_ref, z_ref, g_ref, k):
        z = z_ref[...]
        o = jnp.where(is_prompt, op_ref[...], os_ref[...])
        u = (o * (z * jax.nn.sigmoid(z))).astype(BF16)
        return jax.nn.sigmoid(g_ref[...]) * _dot(u, wbr_ref[k])

    y = (branch(ofp_ref, ofs_ref, zf_ref, g0_ref, 0) + branch(osp_ref, oss_ref, zs_ref, g1_ref, 1)
         + branch(onp_ref, ons_ref, zn_ref, g2_ref, 2))
    xn = x_ref[...] + _dot(y.astype(BF16), wout_ref[...])
    xo_ref[...] = xn
    ho_ref[...] = _rmsnorm_body(xn, gn_ref[...]).astype(ho_ref.dtype)


def _merge(o_prompt, o_sample, p32, x, wbr, wout, g_next, h_dtype):
    T, D = x.shape
    TP, W = o_prompt[0].shape
    TS = o_sample[0].shape[0]
    tm = _tile(math.gcd(TP, TS), 256, 16)
    n_p = TP // tm
    row = lambda c: (lambda i: (i, c))
    prompt_rows = pl.BlockSpec((tm, W), lambda i: (jnp.minimum(i, n_p - 1), 0))
    sample_rows = pl.BlockSpec((tm, W), lambda i: (jnp.maximum(i - n_p, 0), 0))
    return pl.pallas_call(
        functools.partial(_merge_kernel, n_prompt_tiles=n_p),
        grid=(T // tm,),
        in_specs=[prompt_rows] * 3 + [sample_rows] * 3 + [
                  pl.BlockSpec((tm, W), row(C_ZF // W)), pl.BlockSpec((tm, W), row(C_ZS // W)),
                  pl.BlockSpec((tm, W), row(C_ZN // W)),
                  pl.BlockSpec((tm, D), row(C_MERGE // D)), pl.BlockSpec((tm, D), row(C_MERGE // D + 1)),
                  pl.BlockSpec((tm, D), row(C_MERGE // D + 2)),
                  pl.BlockSpec((tm, D), row(0)),
                  pl.BlockSpec((3, W, D), lambda i: (0, 0, 0)),
                  pl.BlockSpec((D, D), lambda i: (0, 0)),
                  pl.BlockSpec((1, D), lambda i: (0, 0))],
        out_specs=[pl.BlockSpec((tm, D), row(0)), pl.BlockSpec((tm, D), row(0))],
        out_shape=[jax.ShapeDtypeStruct((T, D), F32), jax.ShapeDtypeStruct((T, D), h_dtype)],
        compiler_params=_cparams(("parallel",)),
        name="merge_out",
    )(*o_prompt, *o_sample, p32, p32, p32, p32, p32, p32, x, wbr, wout, g_next.reshape(1, D))


KV_LEAVES = ((C_FOXK, FOX_KV), (C_FOXV, FOX_KV), (C_SBK, SB_KV), (C_SBV, SB_KV)) + tuple(
    (C_NKV + c * LANES, NSA_KV) for c in range(6))


def _kv_rows_kernel(*refs):
    n = len(KV_LEAVES)
    for i in range(n):
        refs[2 * n + i][0, 0] = refs[i][...].T


def _kv_rows(p32, bufs, layer, B, S):
    n = len(KV_LEAVES)
    ts = _tile(S, 512, LANES)
    nst = S // ts
    widths = [kv * HEAD_DIM for _, kv in KV_LEAVES]
    return pl.pallas_call(
        _kv_rows_kernel,
        grid=(B, nst),
        in_specs=[pl.BlockSpec((ts, w), lambda b, s, c=c0 // w: (b * nst + s, c)) for (c0, _), w in zip(KV_LEAVES, widths)]
        + [pl.BlockSpec(memory_space=pl.ANY)] * n,
        out_specs=[pl.BlockSpec((1, 1, w, ts), lambda b, s: (layer, b, 0, s)) for w in widths],
        out_shape=[jax.ShapeDtypeStruct(buf.shape, F32) for buf in bufs],
        input_output_aliases={n + i: i for i in range(n)},
        compiler_params=_cparams(("parallel", "parallel")),
        name="kv_rows",
    )(*([p32] * n), *bufs)


def _foxgate_kernel(sm_ref, logf_ref, ft_ref, carry_ref, *, ts):
    @pl.when(pl.program_id(1) == 0)
    def _():
        carry_ref[...] = jnp.zeros_like(carry_ref)

    lf = _log_sigmoid(sm_ref[...])
    logf_ref[...] = lf[:, :FOX_HEADS]
    lft = lf.T[:FOX_HEADS, :]
    upper = (_iota((ts, ts), 0) <= _iota((ts, ts), 1)).astype(BF16)
    c = _dot_exact(lft, upper) + carry_ref[:, 0:1]
    ft_ref[0] = c
    carry_ref[...] = jnp.broadcast_to(c[:, ts - 1:ts], carry_ref.shape)


def _foxgate(p32, B, S):
    ts = _tile(S, 512, 128)
    nst = S // ts
    return pl.pallas_call(
        functools.partial(_foxgate_kernel, ts=ts),
        grid=(B, nst),
        in_specs=[pl.BlockSpec((ts, LANES), lambda b, s: (b * nst + s, C_SMALL // LANES))],
        out_specs=[pl.BlockSpec((ts, FOX_HEADS), lambda b, s: (b * nst + s, 0)),
                   pl.BlockSpec((1, FOX_HEADS, ts), lambda b, s: (b, 0, s))],
        out_shape=[jax.ShapeDtypeStruct((B * S, FOX_HEADS), F32), jax.ShapeDtypeStruct((B, FOX_HEADS, S), F32)],
        scratch_shapes=[pltpu.VMEM((FOX_HEADS, LANES), F32)],
        compiler_params=_cparams(("parallel", "arbitrary")),
        name="fox_gate",
    )(p32)


def _pair_queries(q_ref, qz_ref):
    tq = q_ref.shape[0]
    low = _iota((tq, LANES), 1) < HEAD_DIM
    qa, qb = q_ref[:, :LANES], q_ref[:, LANES:]
    qz_ref[0] = jnp.where(low, qa, 0.0).astype(BF16)
    qz_ref[1] = jnp.where(low, _swap_halves(qa), 0.0).astype(BF16)
    qz_ref[2] = jnp.where(low, 0.0, _swap_halves(qb)).astype(BF16)
    qz_ref[3] = jnp.where(low, 0.0, qb).astype(BF16)


def _pair_output(a0, a1, a2, a3, o_ref):
    low = _iota(a0.shape, 1) < HEAD_DIM
    o_ref[:, :LANES] = jnp.where(low, a0, _swap_halves(a1))
    o_ref[:, LANES:] = jnp.where(low, _swap_halves(a2), a3)


def _rep(x, k):
    return x if k == 1 else jnp.concatenate([x] * k, axis=1)


def _stack_rows(row_vectors, rows):
    n = row_vectors[0].shape[1]
    return jnp.concatenate([jnp.broadcast_to(v, (rows, n)) for v in row_vectors], axis=0)


def _softmax_update(s, pv_fn, m_ref, l_ref, acc_ref, idx=Ellipsis):
    m_prev = m_ref[idx]
    m_new = jnp.maximum(m_prev, jnp.max(s, axis=1, keepdims=True))
    alpha = jnp.exp(m_prev - m_new)
    p = jnp.exp(s - _rep(m_new, s.shape[1] // LANES))
    l_ref[idx] = alpha * l_ref[idx] + jnp.sum(p, axis=1, keepdims=True)
    acc = acc_ref[idx]
    acc_ref[idx] = _rep(alpha, acc.shape[1] // LANES) * acc + pv_fn(p.astype(BF16))
    m_ref[idx] = m_new


def _fox_prompt_kernel(q_ref, k_ref, v_ref, ft_ref, o_ref, qz_ref, m_ref, l_ref, acc_ref, s_ref, *, tq, tk):
    qi = pl.program_id(2)
    q0 = qi * tq
    _pair_queries(q_ref, qz_ref)
    m_ref[...] = jnp.full(m_ref.shape, NEG, F32)
    l_ref[...] = jnp.zeros_like(l_ref)
    acc_ref[...] = jnp.zeros_like(acc_ref)
    qz = qz_ref[...].reshape(4 * tq, LANES)

    def scores(kj):
        k0 = pl.multiple_of(kj * tk, tk)
        fk = ft_ref[0, 0, kj]
        return _dot_nt(qz, k_ref[pl.ds(k0, tk), :]) - _stack_rows([fk[hh:hh + 1, :] for hh in range(4)], tq)

    def consume(kj, s, diag):
        k0 = pl.multiple_of(kj * tk, tk)
        vt = v_ref[pl.ds(k0, tk), :]
        if diag:
            t_idx = jnp.bitwise_and(_iota((4 * tq, tk), 0), tq - 1)
            s = jnp.where(k0 + _iota((4 * tq, tk), 1) <= q0 + t_idx, s, NEG)
        _softmax_update(s, lambda p: _dot(p, vt), m_ref, l_ref, acc_ref)

    n_full = q0 // tk
    s_ref[...] = scores(0)

    def body(kj, carry):
        s = s_ref[...]
        s_ref[...] = scores(kj + 1)
        consume(kj, s, False)
        return carry

    lax.fori_loop(0, n_full, body, 0)
    consume(n_full, s_ref[...], True)
    out = acc_ref[...] / l_ref[...]
    _pair_output(*[out[hh * tq:(hh + 1) * tq] for hh in range(4)], o_ref)


def _fox_prompt(p32, p16, ft4, B, S, tq, tk):
    nq = S // tq
    n_pairs = FOX_KV // 2
    W = FOX_HEADS * HEAD_DIM
    return pl.pallas_call(
        functools.partial(_fox_prompt_kernel, tq=tq, tk=tk),
        grid=(B, n_pairs, nq),
        in_specs=[pl.BlockSpec((tq, 2 * LANES), lambda b, g, i: (b * nq + i, C_FOXQ // (2 * LANES) + g)),
                  pl.BlockSpec((S, LANES), lambda b, g, i: (b, C_FOXK // LANES + g)),
                  pl.BlockSpec((S, LANES), lambda b, g, i: (b, C_FOXV // LANES + g)),
                  pl.BlockSpec((1, 1, S // tk, 4, tk), lambda b, g, i: (b, g, 0, 0, 0))],
        out_specs=pl.BlockSpec((tq, 2 * LANES), lambda b, g, i: (b * nq + i, g)),
        out_shape=jax.ShapeDtypeStruct((B * S, W), F32),
        scratch_shapes=[pltpu.VMEM((4, tq, LANES), BF16), pltpu.VMEM((4 * tq, LANES), F32),
                        pltpu.VMEM((4 * tq, LANES), F32), pltpu.VMEM((4 * tq, LANES), F32),
                        pltpu.VMEM((4 * tq, tk), F32)],
        compiler_params=_cparams(("parallel", "parallel", "arbitrary")),
        name="fox_prompt",
    )(p32, p16, p16, ft4)


def _suffix_and_total(lk, uo):
    hi = lk.astype(BF16)
    lo = (lk - hi.astype(F32)).astype(BF16)
    r = _dot(hi, uo) + _dot(lo, uo)
    return r[:, :LANES], r[:, LANES:]


def _suffix_matrix():
    j = _iota((LANES, LANES), 0)
    s = _iota((LANES, LANES), 1)
    return jnp.concatenate([(j > s).astype(BF16), jnp.ones((LANES, LANES), BF16)], axis=1)


def _sb_blocks(zs, carry, uo, masks):
    running = carry
    out = []
    for z, mask in zip(zs, masks):
        sp = _softplus(z)
        lk = -sp
        if mask is not None:
            lk = jnp.where(mask, lk, 0.0)
        after, tot = _suffix_and_total(lk, uo)
        a = jnp.exp(z - sp + after + running)
        if mask is not None:
            a = jnp.where(mask, a, 0.0)
        out.append(a)
        running = running + tot
    return out, running


def _sb_prompt_kernel(q_ref, k_ref, v_ref, o_ref, qz_ref, carry_ref, acc_ref, *, tq, tk):
    qi = pl.program_id(2)
    q0 = qi * tq
    _pair_queries(q_ref, qz_ref)
    carry_ref[...] = jnp.zeros_like(carry_ref)
    acc_ref[...] = jnp.zeros_like(acc_ref)
    qz = qz_ref[...].reshape(4 * tq, LANES)
    uo = _suffix_matrix()
    nsub = tk // LANES

    def tile(kj, diag):
        k0 = pl.multiple_of(kj * tk, tk)
        kt = k_ref[pl.ds(k0, tk), :]
        vt = v_ref[pl.ds(k0, tk), :]
        z = _dot_nt(qz, kt)
        order = list(reversed(range(nsub)))
        masks = [None] * nsub
        if diag:
            t_idx = jnp.bitwise_and(_iota((4 * tq, LANES), 0), tq - 1)
            col = _iota((4 * tq, LANES), 1)
            masks = [k0 + u * LANES + col < q0 + t_idx for u in order]
        ws, running = _sb_blocks([z[:, u * LANES:(u + 1) * LANES] for u in order], carry_ref[...], uo, masks)
        a = ws[0] if nsub == 1 else jnp.concatenate(list(reversed(ws)), axis=1)
        acc_ref[...] = acc_ref[...] + _dot(a.astype(BF16), vt)
        carry_ref[...] = running
        return jnp.max(running)

    kd = q0 // tk
    top0 = tile(kd, True)

    def cond(st):
        return jnp.logical_and(st[0] >= 0, st[1] > EXP_ZERO)

    def body(st):
        return st[0] - 1, tile(st[0], False)

    lax.while_loop(cond, body, (kd - 1, top0))
    acc = acc_ref[...]
    _pair_output(*[acc[hh * tq:(hh + 1) * tq] for hh in range(4)], o_ref)


def _sb_prompt(p32, p16, B, S, tq, tk):
    nq = S // tq
    n_pairs = SB_KV // 2
    W = SB_HEADS * HEAD_DIM
    return pl.pallas_call(
        functools.partial(_sb_prompt_kernel, tq=tq, tk=tk),
        grid=(B, n_pairs, nq),
        in_specs=[pl.BlockSpec((tq, 2 * LANES), lambda b, g, i: (b * nq + i, C_SBQ // (2 * LANES) + g)),
                  pl.BlockSpec((S, LANES), lambda b, g, i: (b, C_SBK // LANES + g)),
                  pl.BlockSpec((S, LANES), lambda b, g, i: (b, C_SBV // LANES + g))],
        out_specs=pl.BlockSpec((tq, 2 * LANES), lambda b, g, i: (b * nq + i, g)),
        out_shape=jax.ShapeDtypeStruct((B * S, W), F32),
        scratch_shapes=[pltpu.VMEM((4, tq, LANES), BF16), pltpu.VMEM((4 * tq, LANES), F32),
                        pltpu.VMEM((4 * tq, LANES), F32)],
        compiler_params=_cparams(("parallel", "parallel", "arbitrary")),
        name="sb_prompt",
    )(p32, p16, p16)


def _nsa_group_queries(q_ref, qg_ref, rows):
    low = _iota((rows, LANES), 1) < HEAD_DIM
    rper = NSA_HEADS // NSA_KV
    for g in range(NSA_KV):
        for r in range(rper):
            h = rper * g + r
            blk = q_ref[:, (h // 2) * LANES:(h // 2 + 1) * LANES]
            if h % 2 != g:
                blk = _swap_halves(blk)
            keep = low if g == 0 else jnp.logical_not(low)
            qg_ref[g, r] = jnp.where(keep, blk, 0.0).astype(qg_ref.dtype)


def _nsa_output(tot, o_ref):
    low = _iota(tot[0][0].shape, 1) < HEAD_DIM
    for j in range(NSA_HEADS // 2):
        g = j // 2
        a, b = tot[g][2 * (j % 2)], tot[g][2 * (j % 2) + 1]
        if g == 0:
            blk = jnp.where(low, a, _swap_halves(b))
        else:
            blk = jnp.where(low, _swap_halves(a), b)
        o_ref[:, j * LANES:(j + 1) * LANES] = blk


def _topk_select_t(score_t, allowed_t, n_real, k):
    nbp = score_t.shape[0]
    sc = score_t[:n_real]
    n_idx = _iota(sc.shape, 0)
    cnt = jnp.zeros(sc.shape, F32)
    for m in range(n_real):
        row = sc[m:m + 1, :]
        cnt = cnt + jnp.where(n_idx > m, jnp.where(row >= sc, 1.0, 0.0), jnp.where(row > sc, 1.0, 0.0))
    sel = jnp.where(jnp.logical_and(cnt < k, allowed_t[:n_real]), 1.0, 0.0)
    if nbp > n_real:
        sel = jnp.concatenate([sel, jnp.zeros((nbp - n_real, sc.shape[1]), F32)], axis=0)
    return sel


def _nsa_key_table(S):
    assert S // NSA_BLOCK <= ALIBI_LANE
    pos = np.arange(S)
    t = np.zeros((S, LANES), np.float32)
    t[pos, pos // NSA_BLOCK] = NEG
    t[:, ALIBI_LANE] = pos // NSA_BLOCK
    t[:, ALIBI_LANE + 1] = pos % NSA_BLOCK
    return jnp.asarray(t, BF16)


def _nsa_prompt_kernel(q_ref, kc_ref, vc_ref, ks_ref, vs_ref, kw_ref, vw_ref, sm_ref, wk_ref, wv_ref,
                       et_ref, o_ref, kcb_ref, vcb_ref, qg_ref, lhs_ref, m_ref, l_ref, acc_ref, tot_ref,
                       *, tq, tk, S, win):
    qi = pl.program_id(1)
    q0 = qi * tq
    nb = S // NSA_BLOCK
    nbp = kcb_ref.shape[0]
    rper = NSA_HEADS // NSA_KV
    rows = rper * tq

    @pl.when(qi == 0)
    def _():
        def pool(x_ref, w_ref):
            x = x_ref[...].astype(F32).reshape(nb, NSA_BLOCK, LANES)
            c = jnp.sum(x * w_ref[...][None], axis=1)
            if nbp > nb:
                c = jnp.concatenate([c, jnp.zeros((nbp - nb, LANES), F32)], axis=0)
            return c.astype(BF16)
        kcb_ref[...] = pool(kc_ref, wk_ref)
        vcb_ref[...] = pool(vc_ref, wv_ref)

    _nsa_group_queries(q_ref, qg_ref, tq)
    sig = jax.nn.sigmoid(sm_ref[...])

    def gate(g, branch):
        cols = [GATE_LANE0 + 3 * (rper * g + r) + branch for r in range(rper)]
        return jnp.concatenate([jnp.broadcast_to(sig[:, c:c + 1], (tq, LANES)) for c in cols], axis=0)

    def group_q(g):
        return qg_ref[g].reshape(rows, LANES)

    lane = _iota((tq, LANES), 1)
    n_i = _iota((rows, nbp), 1)
    t_i = q0 + jnp.bitwise_and(_iota((rows, nbp), 0), tq - 1)
    blk_end = (n_i + 1) * NSA_BLOCK - 1
    dist_c = (t_i - blk_end).astype(F32)
    valid_c = blk_end <= t_i
    n_t = _iota((nbp, tq), 0)
    cur_t = lax.shift_right_logical(q0 + _iota((nbp, tq), 1), 6)
    allowed_t = n_t <= cur_t
    forced_t = jnp.logical_or(n_t == 0, jnp.logical_or(n_t == cur_t, n_t == cur_t - 1))
    for g in range(NSA_KV):
        slope_rows = _stack_rows([jnp.full((1, nbp), NSA_SLOPES[rper * g + r], F32) for r in range(rper)], tq)
        sc = _dot_nt(group_q(g), kcb_ref[...]) - slope_rows * dist_c
        sc = jnp.where(valid_c, sc, NEG)
        mx = jnp.max(sc, axis=1, keepdims=True)
        e = jnp.where(valid_c, jnp.exp(sc - mx), 0.0)
        pc = e / jnp.maximum(jnp.sum(e, axis=1, keepdims=True), jnp.finfo(F32).tiny)
        tot_ref[g] = gate(g, 0) * _dot(pc.astype(BF16), vcb_ref[...])
        imp = pc[0:tq]
        for r in range(1, rper):
            imp = imp + pc[r * tq:(r + 1) * tq]
        score_t = jnp.where(allowed_t, jnp.where(forced_t, BIG_SCORE, imp.T), -1.0)
        not_sel = 1.0 - _topk_select_t(score_t, allowed_t, nb, NSA_TOPK).T
        for r in range(rper):
            slope = NSA_SLOPES[rper * g + r]
            alibi = jnp.where(lane == ALIBI_LANE, NSA_BLOCK * slope, jnp.where(lane == ALIBI_LANE + 1, slope, 0.0))
            lhs_ref[0, g, r * tq:(r + 1) * tq, LANES:] = jnp.where(lane < ALIBI_LANE, not_sel, alibi).astype(BF16)
            lhs_ref[1, g, r * tq:(r + 1) * tq, LANES:] = alibi.astype(BF16)
        lhs_ref[0, g, :, :LANES] = group_q(g)
        lhs_ref[1, g, :, :LANES] = group_q(g)

    m_ref[...] = jnp.full(m_ref.shape, NEG, F32)
    l_ref[...] = jnp.zeros_like(l_ref)
    acc_ref[...] = jnp.zeros_like(acc_ref)

    def sel_tile(kj, diag):
        k0 = pl.multiple_of(kj * tk, tk)
        rhs = jnp.concatenate([ks_ref[pl.ds(k0, tk), :], et_ref[pl.ds(k0, tk), :]], axis=1)
        vt = vs_ref[pl.ds(k0, tk), :]
        if diag:
            causal = (q0 - k0) + jnp.bitwise_and(_iota((rows, tk), 0), tq - 1) - _iota((rows, tk), 1) >= 0
        ss = [_dot_nt(lhs_ref[0, g], rhs) for g in range(NSA_KV)]
        for g in range(NSA_KV):
            s = ss[g]
            if diag:
                s = jnp.where(causal, s, NEG)
            _softmax_update(s, lambda p: _dot(p, vt), m_ref, l_ref, acc_ref, g)

    def sel_body(kj, c):
        sel_tile(kj, False)
        return c

    n_full = q0 // tk
    lax.fori_loop(0, n_full, sel_body, 0)
    sel_tile(n_full, True)

    span = win + tq
    w0 = pl.multiple_of(jnp.maximum(q0 - win, 0), tq)
    rhs_w = jnp.concatenate([kw_ref[pl.ds(w0, span), :], et_ref[pl.ds(w0, span), :]], axis=1)
    vw = vw_ref[pl.ds(w0, span), :]
    dist = (q0 - w0) + jnp.bitwise_and(_iota((rows, span), 0), tq - 1) - _iota((rows, span), 1)
    in_win = jnp.logical_and(dist >= 0, dist < win)
    for g in range(NSA_KV):
        s = jnp.where(in_win, _dot_nt(lhs_ref[1, g], rhs_w), NEG)
        p = jnp.exp(s - jnp.max(s, axis=1, keepdims=True))
        o_win = _dot(p.astype(BF16), vw) / jnp.sum(p, axis=1, keepdims=True)
        tot_ref[g] = tot_ref[g] + gate(g, 1) * (acc_ref[g] / l_ref[g]) + gate(g, 2) * o_win

    _nsa_output([[tot_ref[g, r * tq:(r + 1) * tq, :] for r in range(rper)] for g in range(NSA_KV)], o_ref)


def _nsa_prompt(p32, p16, wk_exp, wv_exp, key_table, B, S, tq, tk, win):
    nq = S // tq
    nbp = LANES
    W = NSA_HEADS * HEAD_DIM
    rper = NSA_HEADS // NSA_KV
    assert S >= win + tq
    kv = lambda c: pl.BlockSpec((S, LANES), lambda b, i: (b, C_NKV // LANES + c))
    full = lambda a: pl.BlockSpec(a.shape, lambda b, i: (0,) * a.ndim)
    return pl.pallas_call(
        functools.partial(_nsa_prompt_kernel, tq=tq, tk=tk, S=S, win=win),
        grid=(B, nq),
        in_specs=[pl.BlockSpec((tq, W), lambda b, i: (b * nq + i, C_NSAQ // W)),
                  kv(0), kv(1), kv(2), kv(3), kv(4), kv(5),
                  pl.BlockSpec((tq, LANES), lambda b, i: (b * nq + i, C_SMALL // LANES)),
                  full(wk_exp), full(wv_exp), full(key_table)],
        out_specs=pl.BlockSpec((tq, W), lambda b, i: (b * nq + i, 0)),
        out_shape=jax.ShapeDtypeStruct((B * S, W), F32),
        scratch_shapes=[pltpu.VMEM((nbp, LANES), BF16), pltpu.VMEM((nbp, LANES), BF16),
                        pltpu.VMEM((NSA_KV, rper, tq, LANES), BF16),
                        pltpu.VMEM((2, NSA_KV, rper * tq, 2 * LANES), BF16),
                        pltpu.VMEM((NSA_KV, rper * tq, LANES), F32),
                        pltpu.VMEM((NSA_KV, rper * tq, LANES), F32),
                        pltpu.VMEM((NSA_KV, rper * tq, LANES), F32),
                        pltpu.VMEM((NSA_KV, rper * tq, LANES), F32)],
        compiler_params=_cparams(("parallel", "arbitrary"), 56),
        name="nsa_prompt",
    )(p32, p16, p16, p16, p16, p16, p16, p32, wk_exp, wv_exp, key_table)


def _blockdiag_queries(q_ref, qbd_ref, n_heads, n_kv):
    sd = q_ref.shape[0]
    rper = n_heads // n_kv
    width = n_kv * HEAD_DIM
    low = _iota((sd, LANES), 1) < HEAD_DIM
    for h in range(n_heads):
        g = h // rper
        blk = q_ref[:, (h // 2) * LANES:(h // 2 + 1) * LANES]
        if h % 2 != g % 2:
            blk = _swap_halves(blk)
        keep = low if g % 2 == 0 else jnp.logical_not(low)
        blk = jnp.where(keep, blk, 0.0)
        parts = [blk if c == g // 2 else jnp.zeros((sd, LANES), F32) for c in range(width // LANES)]
        row = parts[0] if len(parts) == 1 else jnp.concatenate(parts, axis=1)
        qbd_ref[h * sd:(h + 1) * sd, :] = row.astype(BF16)


def _blockdiag_output(acc, o_ref, n_heads, n_kv, sd):
    rper = n_heads // n_kv
    low = _iota((sd, LANES), 1) < HEAD_DIM
    for j in range(n_heads // 2):
        pieces = []
        for h in (2 * j, 2 * j + 1):
            g = h // rper
            p = acc[h * sd:(h + 1) * sd, (g // 2) * LANES:(g // 2 + 1) * LANES]
            if g % 2 != h % 2:
                p = _swap_halves(p)
            pieces.append(p)
        o_ref[:, j * LANES:(j + 1) * LANES] = jnp.where(low, pieces[0], pieces[1])


def _pad_rows(x, rows):
    return jnp.concatenate([x, jnp.zeros((rows - x.shape[0], x.shape[1]), x.dtype)], axis=0)


def _page_specs(n, shape, index_of):
    return [pl.BlockSpec(shape, index_of(u)) for u in range(n)]


def _expand_heads(x8, sd):
    return jnp.concatenate([jnp.broadcast_to(x8[h:h + 1, :], (sd, x8.shape[1])) for h in range(x8.shape[0])], axis=0)


def _transposed_pool(pool):
    d, n, page, kv, hd = pool.shape
    return pool.transpose(0, 1, 3, 4, 2).reshape(d, n, kv * hd, page)


def _fox_decode_kernel(pt_ref, q_ref, kn_ref, vn_ref, sm_ref, lf_ref, *rest, sd, n_steps, pp):
    kt_refs, vt_refs = rest[:pp], rest[pp:2 * pp]
    o_ref, lfo_ref, qbd_ref, m_ref, l_ref, acc_ref, cd_ref = rest[2 * pp:]
    b = pl.program_id(0)
    p = pl.program_id(1)
    n_pages = n_steps * pp
    rows = FOX_HEADS * sd
    uo = _suffix_matrix()

    @pl.when(p == 0)
    def _():
        _blockdiag_queries(q_ref, qbd_ref, FOX_HEADS, FOX_KV)
        head_of_row = lax.shift_right_logical(_iota((rows, LANES), 0), 3)
        i_of_row = jnp.bitwise_and(_iota((rows, LANES), 0), sd - 1)
        col = _iota((rows, LANES), 1)
        lf = _log_sigmoid(sm_ref[...])
        lfo_ref[...] = lf[:, :FOX_HEADS]
        sel_h = jnp.where(col == head_of_row, 1.0, 0.0).astype(BF16)
        a = _dot_nt_exact_rhs(sel_h, _pad_rows(lf, LANES))
        incl = (_iota((LANES, LANES), 0) <= _iota((LANES, LANES), 1)).astype(BF16)
        bias_new = -_dot_exact(a, incl)
        kn = _pad_rows(kn_ref[...], LANES).astype(BF16)
        vn = _pad_rows(vn_ref[...], LANES).astype(BF16)
        s = _dot_nt(qbd_ref[...], kn) + bias_new
        s = jnp.where(col <= i_of_row, s, NEG)
        mx = jnp.max(s, axis=1, keepdims=True)
        pr = jnp.exp(s - mx)
        m_ref[...] = jnp.broadcast_to(mx, m_ref.shape)
        l_ref[...] = jnp.broadcast_to(jnp.sum(pr, axis=1, keepdims=True), l_ref.shape)
        acc_ref[...] = _dot(pr.astype(BF16), vn)
        cd_ref[...] = jnp.zeros_like(cd_ref)

    later = cd_ref[...]
    qbd = qbd_ref[...]
    parts = []
    for u in range(pp):
        lf_page = lf_ref[0, pt_ref[b, n_pages - 1 - (p * pp + u)]]
        r = _dot_exact(lf_page, uo)
        bias = _expand_heads(r[:, :LANES] + later, sd)
        later = later + r[:, LANES:]
        parts.append(_dot(qbd, kt_refs[u][0, 0].astype(BF16)) + bias)
    cd_ref[...] = later
    s = parts[0] if pp == 1 else jnp.concatenate(parts, axis=1)

    def pv(pr):
        out = _dot_nt(pr[:, :LANES], vt_refs[0][0, 0].astype(BF16))
        for u in range(1, pp):
            out = out + _dot_nt(pr[:, u * LANES:(u + 1) * LANES], vt_refs[u][0, 0].astype(BF16))
        return out

    _softmax_update(s, pv, m_ref, l_ref, acc_ref)

    @pl.when(p == n_steps - 1)
    def _():
        acc = acc_ref[...] / _rep(l_ref[...], acc_ref.shape[1] // LANES)
        _blockdiag_output(acc, o_ref, FOX_HEADS, FOX_KV, sd)


def _fox_decode(p32, page_table, pool_k, pool_v, pool_lf, layer, row0, DB, sd, pp):
    n_pages = page_table.shape[1]
    n_steps = n_pages // pp
    page = pool_k.shape[2]
    Wq, Wk = FOX_HEADS * HEAD_DIM, FOX_KV * HEAD_DIM
    rows = FOX_HEADS * sd
    rb = row0 // sd
    pkt, pvt = _transposed_pool(pool_k), _transposed_pool(pool_v)
    plf = pool_lf.transpose(0, 1, 3, 2)
    pg = lambda u: (lambda b, p, pt: (layer, pt[b, n_pages - 1 - (p * pp + u)], 0, 0))
    grid_spec = pltpu.PrefetchScalarGridSpec(
        num_scalar_prefetch=1,
        grid=(DB, n_steps),
        in_specs=[pl.BlockSpec((sd, Wq), lambda b, p, pt: (rb + b, C_FOXQ // Wq)),
                  pl.BlockSpec((sd, Wk), lambda b, p, pt: (rb + b, C_FOXK // Wk)),
                  pl.BlockSpec((sd, Wk), lambda b, p, pt: (rb + b, C_FOXV // Wk)),
                  pl.BlockSpec((sd, LANES), lambda b, p, pt: (rb + b, C_SMALL // LANES)),
                  pl.BlockSpec((1,) + plf.shape[1:], lambda b, p, pt: (layer, 0, 0, 0))]
        + _page_specs(pp, (1, 1, Wk, page), pg) + _page_specs(pp, (1, 1, Wk, page), pg),
        out_specs=[pl.BlockSpec((sd, Wq), lambda b, p, pt: (b, 0)),
                   pl.BlockSpec((sd, FOX_HEADS), lambda b, p, pt: (b, 0))],
        scratch_shapes=[pltpu.VMEM((rows, Wk), BF16), pltpu.VMEM((rows, LANES), F32),
                        pltpu.VMEM((rows, LANES), F32), pltpu.VMEM((rows, Wk), F32),
                        pltpu.VMEM((FOX_HEADS, LANES), F32)])
    return pl.pallas_call(
        functools.partial(_fox_decode_kernel, sd=sd, n_steps=n_steps, pp=pp),
        grid_spec=grid_spec,
        out_shape=[jax.ShapeDtypeStruct((DB * sd, Wq), F32), jax.ShapeDtypeStruct((DB * sd, FOX_HEADS), F32)],
        compiler_params=_cparams(("parallel", "arbitrary"), 56),
        name="fox_decode",
    )(page_table, p32, p32, p32, p32, plf, *([pkt] * pp), *([pvt] * pp))


def _sb_decode_kernel(pt_ref, q_ref, kn_ref, vn_ref, *rest, sd, n_steps, pp):
    del pt_ref
    kt_refs, vt_refs = rest[:pp], rest[pp:2 * pp]
    o_ref, qbd_ref, carry_ref, acc_ref, top_ref = rest[2 * pp:]
    p = pl.program_id(1)
    rows = SB_HEADS * sd
    uo = _suffix_matrix()

    @pl.when(p == 0)
    def _():
        _blockdiag_queries(q_ref, qbd_ref, SB_HEADS, SB_KV)
        i_of_row = jnp.bitwise_and(_iota((rows, LANES), 0), sd - 1)
        strict = _iota((rows, LANES), 1) < i_of_row
        kn = _pad_rows(kn_ref[...], LANES).astype(BF16)
        vn = _pad_rows(vn_ref[...], LANES).astype(BF16)
        ws, c = _sb_blocks([_dot_nt(qbd_ref[...], kn)], jnp.zeros((rows, LANES), F32), uo, [strict])
        acc_ref[...] = _dot(ws[0].astype(BF16), vn)
        carry_ref[...] = c
        top_ref[0] = jnp.max(c)

    @pl.when(top_ref[0] > EXP_ZERO)
    def _():
        qbd = qbd_ref[...]
        zs = [_dot(qbd, kt_refs[u][0, 0].astype(BF16)) for u in range(pp)]
        ws, c = _sb_blocks(zs, carry_ref[...], uo, [None] * pp)
        acc = acc_ref[...]
        for u in range(pp):
            acc = acc + _dot_nt(ws[u].astype(BF16), vt_refs[u][0, 0].astype(BF16))
        acc_ref[...] = acc
        carry_ref[...] = c
        top_ref[0] = jnp.max(c)

    @pl.when(p == n_steps - 1)
    def _():
        _blockdiag_output(acc_ref[...], o_ref, SB_HEADS, SB_KV, sd)


def _sb_decode(p32, page_table, pool_k, pool_v, layer, row0, DB, sd, pp):
    n_pages = page_table.shape[1]
    n_steps = n_pages // pp
    page = pool_k.shape[2]
    Wq, Wk = SB_HEADS * HEAD_DIM, SB_KV * HEAD_DIM
    rows = SB_HEADS * sd
    rb = row0 // sd
    pkt, pvt = _transposed_pool(pool_k), _transposed_pool(pool_v)
    pg = lambda u: (lambda b, p, pt: (layer, pt[b, n_pages - 1 - (p * pp + u)], 0, 0))
    grid_spec = pltpu.PrefetchScalarGridSpec(
        num_scalar_prefetch=1,
        grid=(DB, n_steps),
        in_specs=[pl.BlockSpec((sd, Wq), lambda b, p, pt: (rb + b, C_SBQ // Wq)),
                  pl.BlockSpec((sd, Wk), lambda b, p, pt: (rb + b, C_SBK // Wk)),
                  pl.BlockSpec((sd, Wk), lambda b, p, pt: (rb + b, C_SBV // Wk))]
        + _page_specs(pp, (1, 1, Wk, page), pg) + _page_specs(pp, (1, 1, Wk, page), pg),
        out_specs=pl.BlockSpec((sd, Wq), lambda b, p, pt: (b, 0)),
        scratch_shapes=[pltpu.VMEM((rows, Wk), BF16), pltpu.VMEM((rows, LANES), F32),
                        pltpu.VMEM((rows, Wk), F32), pltpu.SMEM((1,), F32)])
    return pl.pallas_call(
        functools.partial(_sb_decode_kernel, sd=sd, n_steps=n_steps, pp=pp),
        grid_spec=grid_spec,
        out_shape=jax.ShapeDtypeStruct((DB * sd, Wq), F32),
        compiler_params=_cparams(("parallel", "arbitrary")),
        name="sb_decode",
    )(page_table, p32, p32, p32, *([pkt] * pp), *([pvt] * pp))


def _nsa_decode_slopes(rows, g, width=LANES):
    rper = NSA_HEADS // NSA_KV
    rr = lax.shift_right_logical(_iota((rows, width), 0), 3)
    s = jnp.full((rows, width), NSA_SLOPES[rper * g], F32)
    for r in range(1, rper):
        s = jnp.where(rr == r, NSA_SLOPES[rper * g + r], s)
    return s


def _nsa_decode_group_q(qg_ref, g):
    rper = NSA_HEADS // NSA_KV
    return jnp.concatenate([qg_ref[g, r] for r in range(rper)], axis=0).astype(BF16)


def _nsa_decode_pool_kernel(pt_ref, q_ref, sm_ref, wk_ref, wv_ref, *rest, sd, n_steps, pp, past):
    del pt_ref
    kc_refs, vc_refs = rest[:pp], rest[pp:2 * pp]
    tot_ref, selr_ref, kcb_ref, vcb_ref, qg_ref = rest[2 * pp:]
    j = pl.program_id(1)
    rper = NSA_HEADS // NSA_KV
    rows = rper * sd
    nbc = past // NSA_BLOCK
    page = kc_refs[0].shape[3]
    bpp = page // NSA_BLOCK
    i_row = jnp.bitwise_and(_iota((rows, LANES), 0), sd - 1)
    col = _iota((rows, LANES), 1)
    t_row = past + i_row
    slopes = functools.partial(_nsa_decode_slopes, rows)
    group_q = functools.partial(_nsa_decode_group_q, qg_ref)

    @pl.when(j == 0)
    def _():
        kcb_ref[...] = jnp.zeros_like(kcb_ref)
        vcb_ref[...] = jnp.zeros_like(vcb_ref)

    kacc = kcb_ref[...]
    vacc = vcb_ref[...]
    pos_blk = lax.shift_right_logical(_iota((LANES, LANES), 0), 6)
    n_col = _iota((LANES, LANES), 1)
    for u in range(pp):
        ind = (n_col == (j * pp + u) * bpp + pos_blk).astype(BF16)
        pk = kc_refs[u][0, 0] * wk_ref[...]
        pv_ = vc_refs[u][0, 0] * wv_ref[...]
        kh = pk.astype(BF16)
        vh = pv_.astype(BF16)
        stack = jnp.concatenate([kh, (pk - kh.astype(F32)).astype(BF16),
                                 vh, (pv_ - vh.astype(F32)).astype(BF16)], axis=0)
        r = _dot(stack, ind)
        kacc = kacc + r[0:LANES] + r[LANES:2 * LANES]
        vacc = vacc + r[2 * LANES:3 * LANES] + r[3 * LANES:4 * LANES]
    kcb_ref[...] = kacc
    vcb_ref[...] = vacc

    @pl.when(j == n_steps - 1)
    def _():
        _nsa_group_queries(q_ref, qg_ref, sd)
        sig = jax.nn.sigmoid(sm_ref[...])
        blk_end = (col + 1) * NSA_BLOCK - 1
        dist_c = (t_row - blk_end).astype(F32)
        valid_c = jnp.logical_and(blk_end <= t_row, col < nbc)
        kcb = kcb_ref[...].astype(BF16)
        vcb = vcb_ref[...].astype(BF16)
        n_i = _iota((sd, LANES), 1)
        allowed = n_i < nbc
        forced = jnp.logical_or(n_i == 0, n_i == nbc - 1)
        m_idx = _iota((LANES, LANES), 0)
        n_idx = _iota((LANES, LANES), 1)
        for g in range(NSA_KV):
            sc = _dot(group_q(g), kcb) - slopes(g) * dist_c
            sc = jnp.where(valid_c, sc, NEG)
            mx = jnp.max(sc, axis=1, keepdims=True)
            e = jnp.where(valid_c, jnp.exp(sc - mx), 0.0)
            pc = e / jnp.maximum(jnp.sum(e, axis=1, keepdims=True), jnp.finfo(F32).tiny)
            gate_c = jnp.concatenate(
                [jnp.broadcast_to(sig[:, GATE_LANE0 + 3 * (rper * g + r):GATE_LANE0 + 3 * (rper * g + r) + 1],
                                  (sd, LANES)) for r in range(rper)], axis=0)
            tot_ref[0, g] = gate_c * _dot_nt(pc.astype(BF16), vcb)
            imp = pc[0:sd]
            for r in range(1, rper):
                imp = imp + pc[r * sd:(r + 1) * sd]
            score = jnp.where(allowed, jnp.where(forced, BIG_SCORE, imp), -1.0)
            score_t = _pad_rows(score, LANES).T
            sel_rows = []
            for i in range(sd):
                cm = score_t[:, i:i + 1]
                rn = score[i:i + 1, :]
                beats = jnp.where(n_idx > m_idx, jnp.where(cm >= rn, 1.0, 0.0), jnp.where(cm > rn, 1.0, 0.0))
                cnt = jnp.sum(beats, axis=0, keepdims=True)
                sel_rows.append(jnp.where(jnp.logical_and(cnt < NSA_TOPK - 1, allowed[i:i + 1, :]), 1.0, 0.0))
            sel = jnp.concatenate(sel_rows, axis=0)
            selr_ref[0, g] = jnp.concatenate([sel] * rper, axis=0).astype(BF16)


def _nsa_decode_attend_kernel(pt_ref, q_ref, ksn_ref, vsn_ref, kwn_ref, vwn_ref, sm_ref, kws_ref, vws_ref,
                              tot_ref, selr_ref, *rest, sd, n_steps, pp, past):
    del pt_ref
    ks_refs, vs_refs = rest[:pp], rest[pp:2 * pp]
    o_ref, qg_ref, m_ref, l_ref, acc_ref = rest[2 * pp:]
    j = pl.program_id(1)
    rper = NSA_HEADS // NSA_KV
    rows = rper * sd
    page = ks_refs[0].shape[3]
    bpp = page // NSA_BLOCK
    win = kws_ref.shape[3]
    i_row = jnp.bitwise_and(_iota((rows, LANES), 0), sd - 1)
    col = _iota((rows, LANES), 1)
    slopes = functools.partial(_nsa_decode_slopes, rows)
    group_q = functools.partial(_nsa_decode_group_q, qg_ref)

    @pl.when(j == 0)
    def _():
        _nsa_group_queries(q_ref, qg_ref, sd)
        m_ref[...] = jnp.full(m_ref.shape, NEG, F32)
        l_ref[...] = jnp.zeros_like(l_ref)
        acc_ref[...] = jnp.zeros_like(acc_ref)

    def selection_step():
        p0 = j * pp
        wide = pp * LANES
        colw = _iota((rows, wide), 1)
        iw = jnp.bitwise_and(_iota((rows, wide), 0), sd - 1)
        dist = (past + iw - (p0 * page + colw)).astype(F32)
        e_p = (_iota((LANES, wide), 0) ==
               p0 * bpp + lax.shift_right_logical(_iota((LANES, wide), 1), 6)).astype(BF16)
        for g in range(NSA_KV):
            qg = group_q(g)
            qk = [_dot(qg, ks_refs[u][0, 0].astype(BF16)) for u in range(pp)]
            s = qk[0] if pp == 1 else jnp.concatenate(qk, axis=1)
            selb = (_dot(selr_ref[0, g], e_p) - 1.0) * (-NEG)
            s = s - slopes(g, wide) * dist + selb

            def pv(pr):
                out = _dot_nt(pr[:, :LANES], vs_refs[0][0, 0].astype(BF16))
                for u in range(1, pp):
                    out = out + _dot_nt(pr[:, u * LANES:(u + 1) * LANES], vs_refs[u][0, 0].astype(BF16))
                return out

            _softmax_update(s, pv, m_ref, l_ref, acc_ref, g)

    selection_step()

    @pl.when(j == n_steps - 1)
    def _():
        sig = jax.nn.sigmoid(sm_ref[...])

        def gate(g, branch):
            return jnp.concatenate(
                [jnp.broadcast_to(sig[:, GATE_LANE0 + 3 * (rper * g + r) + branch:
                                      GATE_LANE0 + 3 * (rper * g + r) + branch + 1], (sd, LANES))
                 for r in range(rper)], axis=0)

        new_ok = col <= i_row
        dist_new = (i_row - col).astype(F32)
        ksn = _pad_rows(ksn_ref[...], LANES).astype(BF16)
        vsn = _pad_rows(vsn_ref[...], LANES).astype(BF16)
        kwn = _pad_rows(kwn_ref[...], LANES).astype(BF16)
        vwn = _pad_rows(vwn_ref[...], LANES).astype(BF16)
        kws = kws_ref[0, 0].astype(BF16)
        vws = vws_ref[0, 0].astype(BF16)
        colw = _iota((rows, win), 1)
        iw = jnp.bitwise_and(_iota((rows, win), 0), sd - 1)
        dist_w = (win + iw - colw).astype(F32)
        ok_w = colw > iw
        tots = []
        for g in range(NSA_KV):
            qg = group_q(g)
            sl = slopes(g)
            s = jnp.where(new_ok, _dot_nt(qg, ksn) - sl * dist_new, NEG)
            _softmax_update(s, lambda pr: _dot(pr, vsn), m_ref, l_ref, acc_ref, g)
            o_sel = acc_ref[g] / l_ref[g]
            s1 = jnp.where(ok_w, _dot(qg, kws) - slopes(g, win) * dist_w, NEG)
            s2 = jnp.where(new_ok, _dot_nt(qg, kwn) - sl * dist_new, NEG)
            mx = jnp.maximum(jnp.max(s1, axis=1, keepdims=True), jnp.max(s2, axis=1, keepdims=True))
            p1 = jnp.exp(s1 - mx)
            p2 = jnp.exp(s2 - mx)
            den = jnp.sum(p1, axis=1, keepdims=True) + jnp.sum(p2, axis=1, keepdims=True)
            o_win = (_dot_nt(p1.astype(BF16), vws) + _dot(p2.astype(BF16), vwn)) / den
            t = tot_ref[0, g] + gate(g, 1) * o_sel + gate(g, 2) * o_win
            tots.append([t[r * sd:(r + 1) * sd] for r in range(rper)])
        _nsa_output(tots, o_ref)


def _nsa_decode(p32, page_table, pool_kc, pool_vc, pool_ks, pool_vs, st_kw, st_vw, wk_t, wv_t,
                layer, row0, DB, sd, pp):
    n_pages = page_table.shape[1]
    n_steps = n_pages // pp
    page = pool_kc.shape[2]
    past = n_pages * page
    W = NSA_HEADS * HEAD_DIM
    rper = NSA_HEADS // NSA_KV
    rows = rper * sd
    assert past // NSA_BLOCK <= LANES and past % NSA_BLOCK == 0 and sd <= NSA_BLOCK
    rb = row0 // sd
    win = st_kw.shape[2]
    new = lambda c: pl.BlockSpec((sd, LANES), lambda b, j, pt: (rb + b, C_NKV // LANES + c))
    pg = lambda u: (lambda b, j, pt: (layer, pt[b, j * pp + u], 0, 0))
    pages = _page_specs(pp, (1, 1, LANES, page), pg)
    q_spec = pl.BlockSpec((sd, W), lambda b, j, pt: (rb + b, C_NSAQ // W))
    sm_spec = pl.BlockSpec((sd, LANES), lambda b, j, pt: (rb + b, C_SMALL // LANES))
    state = pl.BlockSpec((1, 1, LANES, win), lambda b, j, pt: (layer, b, 0, 0))
    per_seq = pl.BlockSpec((1, NSA_KV, rows, LANES), lambda b, j, pt: (b, 0, 0, 0))
    tp = _transposed_pool
    tot_c, sel_rows = pl.pallas_call(
        functools.partial(_nsa_decode_pool_kernel, sd=sd, n_steps=n_steps, pp=pp, past=past),
        grid_spec=pltpu.PrefetchScalarGridSpec(
            num_scalar_prefetch=1,
            grid=(DB, n_steps),
            in_specs=[q_spec, sm_spec,
                      pl.BlockSpec(wk_t.shape, lambda b, j, pt: (0, 0)),
                      pl.BlockSpec(wv_t.shape, lambda b, j, pt: (0, 0))] + pages + pages,
            out_specs=[per_seq, per_seq],
            scratch_shapes=[pltpu.VMEM((LANES, LANES), F32), pltpu.VMEM((LANES, LANES), F32),
                            pltpu.VMEM((NSA_KV, rper, sd, LANES), F32)]),
        out_shape=[jax.ShapeDtypeStruct((DB, NSA_KV, rows, LANES), F32),
                   jax.ShapeDtypeStruct((DB, NSA_KV, rows, LANES), BF16)],
        compiler_params=_cparams(("parallel", "arbitrary")),
        name="nsa_decode_pool",
    )(page_table, p32, p32, wk_t, wv_t, *([tp(pool_kc)] * pp), *([tp(pool_vc)] * pp))
    return pl.pallas_call(
        functools.partial(_nsa_decode_attend_kernel, sd=sd, n_steps=n_steps, pp=pp, past=past),
        grid_spec=pltpu.PrefetchScalarGridSpec(
            num_scalar_prefetch=1,
            grid=(DB, n_steps),
            in_specs=[q_spec, new(2), new(3), new(4), new(5), sm_spec, state, state, per_seq, per_seq]
            + pages + pages,
            out_specs=pl.BlockSpec((sd, W), lambda b, j, pt: (b, 0)),
            scratch_shapes=[pltpu.VMEM((NSA_KV, rper, sd, LANES), F32),
                            pltpu.VMEM((NSA_KV, rows, LANES), F32), pltpu.VMEM((NSA_KV, rows, LANES), F32),
                            pltpu.VMEM((NSA_KV, rows, LANES), F32)]),
        out_shape=jax.ShapeDtypeStruct((DB * sd, W), F32),
        compiler_params=_cparams(("parallel", "arbitrary")),
        name="nsa_decode_attend",
    )(page_table, p32, p32, p32, p32, p32, p32, tp(st_kw), tp(st_vw), tot_c, sel_rows,
      *([tp(pool_ks)] * pp), *([tp(pool_vs)] * pp))


def _prep_in_weights(w_in, b_in):
    segs = {}
    off = 0
    for name, n in (("fox_q", 512), ("fox_k", 256), ("fox_v", 256), ("fox_f", 8), ("sb_q", 512), ("sb_k", 256),
                    ("sb_v", 256), ("nsa_q", 512), ("nsa_kv", 768), ("nsa_g", 24), ("z", 1536), ("merge", 3072)):
        segs[name] = (off, off + n)
        off += n
    assert off == w_in.shape[-1]
    scale = HEAD_DIM ** -0.5

    def build(a):
        sl = lambda nm: a[..., segs[nm][0]:segs[nm][1]]
        pad = jnp.zeros(a.shape[:-1] + (LANES - 32,), a.dtype)
        return jnp.concatenate([sl("fox_q") * scale, sl("fox_k"), sl("fox_v"), sl("sb_q") * scale, sl("sb_k"),
                                sl("sb_v"), sl("nsa_q") * scale, sl("z"), sl("merge"), sl("nsa_kv"),
                                sl("fox_f"), sl("nsa_g"), pad], axis=-1)
    wp = build(w_in).astype(BF16)
    bp = build(b_in)[:, None, :]
    assert wp.shape[-1] == N_PAD
    return wp, bp


def kernel(x_prompt, x_sample, cache_fox_k, cache_fox_v, cache_fox_logf, cache_sb_k, cache_sb_v, cache_nsa_kc, cache_nsa_vc, cache_nsa_ks, cache_nsa_vs, state_nsa_kw, state_nsa_vw, page_table, norm_g, w_in, b_in, w_cmp_k, w_cmp_v, w_br_fox, w_br_sb, w_br_nsa, w_out, final_norm_g):
    B, S, D = x_prompt.shape
    DB, SD, _ = x_sample.shape
    depth = w_in.shape[0]
    page = cache_fox_k.shape[2]
    n_pages = page_table.shape[1]
    win = state_nsa_kw.shape[2]
    tq = LANES
    tk_fox = _tile(S, 512, LANES)
    tk_sb = _tile(S, 256, LANES)
    pp_sb = _tile(n_pages, 32, 1)
    pp_fox = _tile(n_pages, 32, 1)
    pp_nsa = _tile(n_pages, 64, 1)
    assert cache_fox_k.shape[3:] == (FOX_KV, HEAD_DIM) and cache_fox_logf.shape[3] == FOX_HEADS
    assert cache_sb_k.shape[3:] == (SB_KV, HEAD_DIM) and cache_nsa_kc.shape[3:] == (NSA_KV, HEAD_DIM)
    assert w_br_fox.shape[1] == FOX_HEADS * HEAD_DIM and w_br_sb.shape[1] == SB_HEADS * HEAD_DIM
    assert w_br_nsa.shape[1] == NSA_HEADS * HEAD_DIM and w_cmp_k.shape[1:] == (NSA_BLOCK, NSA_KV)
    assert SD == 8 and page == LANES and S % tq == 0 and S >= win and n_pages * page >= win
    TP, TS = B * S, DB * SD
    assert TP % SD == 0

    x = jnp.concatenate([x_prompt.reshape(TP, D), x_sample.reshape(TS, D)], axis=0)
    wp, bp = _prep_in_weights(w_in, b_in)
    wbr = jnp.stack([w_br_fox, w_br_sb, w_br_nsa], axis=1).astype(BF16)
    wout = w_out.astype(BF16)
    wk_exp = jnp.repeat(w_cmp_k, HEAD_DIM, axis=2)
    wv_exp = jnp.repeat(w_cmp_v, HEAD_DIM, axis=2)
    bpp = page // NSA_BLOCK
    key_table = _nsa_key_table(S)
    kv_bufs = [jnp.zeros((depth, B, kv * HEAD_DIM, S), F32) for _, kv in KV_LEAVES]

    h = _rmsnorm(x, norm_g[0], BF16)
    logf_p_all, logf_s_all, proj_s = [], [], []
    for l in range(depth):
        p32, p16 = _inproj(h, wp[l], bp[l])
        logf_p, ft = _foxgate(p32, B, S)
        ft4 = ft.reshape(B, FOX_KV // 2, 4, S // tk_fox, tk_fox).transpose(0, 1, 3, 2, 4)
        of_p = _fox_prompt(p32, p16, ft4, B, S, tq, tk_fox)
        os_p = _sb_prompt(p32, p16, B, S, tq, tk_sb)
        on_p = _nsa_prompt(p32, p16, wk_exp[l], wv_exp[l], key_table, B, S, tq, tk_fox, win)
        kv_bufs = _kv_rows(p32, kv_bufs, l, B, S)
        logf_p_all.append(logf_p.reshape(B, S, FOX_HEADS))
        of_s, logf_s = _fox_decode(p32, page_table, cache_fox_k, cache_fox_v, cache_fox_logf, l, TP, DB, SD, pp_fox)
        os_s = _sb_decode(p32, page_table, cache_sb_k, cache_sb_v, l, TP, DB, SD, pp_sb)
        wk_t = jnp.tile(wk_exp[l].T, (1, bpp))
        wv_t = jnp.tile(wv_exp[l].T, (1, bpp))
        on_s = _nsa_decode(p32, page_table, cache_nsa_kc, cache_nsa_vc, cache_nsa_ks, cache_nsa_vs,
                           state_nsa_kw, state_nsa_vw, wk_t, wv_t, l, TP, DB, SD, pp_nsa)
        last = l + 1 == depth
        g_next = final_norm_g if last else norm_g[l + 1]
        x, h = _merge((of_p, os_p, on_p), (of_s, os_s, on_s), p32, x, wbr[l], wout[l], g_next,
                      F32 if last else BF16)

        proj_s.append(p32[TP:])
        logf_s_all.append(logf_s.reshape(DB, SD, FOX_HEADS))

    y_prompt = h[:TP].reshape(B, S, D)
    y_sample = h[TP:].reshape(DB, SD, D)
    proj_s = jnp.stack(proj_s)
    kv_s = [proj_s[:, :, c0:c0 + kv * HEAD_DIM].reshape(depth, DB, SD, kv, HEAD_DIM) for c0, kv in KV_LEAVES]
    kv_s[8] = jnp.concatenate([state_nsa_kw, kv_s[8]], axis=2)[:, :, SD:]
    kv_s[9] = jnp.concatenate([state_nsa_vw, kv_s[9]], axis=2)[:, :, SD:]
    outs_s = kv_s[:2] + [jnp.stack(logf_s_all)] + kv_s[2:]
    kv_p = [buf.reshape(depth, B, kv, HEAD_DIM, S).transpose(0, 1, 4, 2, 3) for buf, (_, kv) in zip(kv_bufs, KV_LEAVES)]
    kv_p[8], kv_p[9] = kv_p[8][:, :, S - win:], kv_p[9][:, :, S - win:]
    outs_p = kv_p[:2] + [jnp.stack(logf_p_all)] + kv_p[2:]
    return (y_prompt, y_sample, *outs_p, *outs_s)
```

```python
import functools
import math

import numpy as np
import jax
import jax.numpy as jnp
from jax import lax
from jax.experimental import pallas as pl
from jax.experimental.pallas import tpu as pltpu

F32 = jnp.float32
BF16 = jnp.bfloat16

HEAD_DIM = 64
FOX_HEADS, FOX_KV = 8, 4
SB_HEADS, SB_KV = 8, 4
NSA_HEADS, NSA_KV = 8, 2
NSA_BLOCK = 64
BLOCK_SHIFT = NSA_BLOCK.bit_length() - 1
DEC_ROWS = 8
SD_SHIFT = DEC_ROWS.bit_length() - 1
NSA_TOPK = 16
RMS_EPS = 1e-6
LANES = 128
NEG = -1e30
BIG_SCORE = 8.0
EXP_ZERO = -104.0

C_FOXQ, C_FOXK, C_FOXV = 0, 512, 768
C_SBQ, C_SBK, C_SBV = 1024, 1536, 1792
C_NSAQ = 2048
C_ZF, C_ZS, C_ZN = 2560, 3072, 3584
C_MERGE = 4096
C_NKV = 7168
C_SMALL = 7936
N_PAD = 8064
GATE_LANE0 = 8
ALIBI_LANE = 64

NSA_SLOPES = tuple(float(2.0 ** (-8.0 * (i + 1) / NSA_HEADS)) for i in range(NSA_HEADS))


def _tile(n, target, mult):
    best = None
    for d in range(mult, min(n, target) + 1, mult):
        if n % d == 0:
            best = d
    return best if best is not None else n


def _cparams(sem, vmem_mb=48):
    return pltpu.CompilerParams(dimension_semantics=sem, vmem_limit_bytes=vmem_mb * 1024 * 1024)


def _dot(a, b):
    return jnp.dot(a, b, preferred_element_type=F32)


def _dot_nt(a, b):
    return lax.dot_general(a, b, (((1,), (1,)), ((), ())), preferred_element_type=F32)


def _split3(x):
    hi = x.astype(BF16)
    r = x - hi.astype(F32)
    mid = r.astype(BF16)
    lo = (r - mid.astype(F32)).astype(BF16)
    return hi, mid, lo


def _dot_exact(x, m01):
    hi, mid, lo = _split3(x)
    return _dot(hi, m01) + _dot(mid, m01) + _dot(lo, m01)


def _dot_nt_exact_rhs(m01, x):
    hi, mid, lo = _split3(x)
    return _dot_nt(m01, hi) + _dot_nt(m01, mid) + _dot_nt(m01, lo)


def _log_sigmoid(x):
    return jnp.minimum(x, 0.0) - jnp.log1p(jnp.exp(-jnp.abs(x)))


def _softplus(x):
    return jnp.maximum(x, 0.0) + jnp.log1p(jnp.exp(-jnp.abs(x)))


def _iota(shape, dim):
    return lax.broadcasted_iota(jnp.int32, shape, dim)


def _swap_halves(x):
    return pltpu.roll(x, 64, 1)


def _rmsnorm_body(x, g):
    return x * lax.rsqrt(jnp.mean(x * x, axis=-1, keepdims=True) + RMS_EPS) * g


def _rmsnorm_kernel(x_ref, g_ref, o_ref):
    o_ref[...] = _rmsnorm_body(x_ref[...], g_ref[...]).astype(o_ref.dtype)


def _rmsnorm(x, g, out_dtype):
    T, D = x.shape
    tm = _tile(T, 1024, 16)
    return pl.pallas_call(
        _rmsnorm_kernel,
        grid=(T // tm,),
        in_specs=[pl.BlockSpec((tm, D), lambda i: (i, 0)), pl.BlockSpec((1, D), lambda i: (0, 0))],
        out_specs=pl.BlockSpec((tm, D), lambda i: (i, 0)),
        out_shape=jax.ShapeDtypeStruct((T, D), out_dtype),
        compiler_params=_cparams(("parallel",)),
        name="rmsnorm",
    )(x, g.reshape(1, D))


def _inproj_kernel(h_ref, w_ref, b_ref, o32_ref, o16_ref):
    acc = _dot(h_ref[...], w_ref[...]) + b_ref[...]
    o32_ref[...] = acc
    o16_ref[...] = acc.astype(BF16)


def _inproj(h, w, b):
    T, D = h.shape
    N = w.shape[1]
    tm = _tile(T, 1280, 16)
    tn = _tile(N, 1152, 128)
    return pl.pallas_call(
        _inproj_kernel,
        grid=(N // tn, T // tm),
        in_specs=[pl.BlockSpec((tm, D), lambda j, i: (i, 0)),
                  pl.BlockSpec((D, tn), lambda j, i: (0, j)),
                  pl.BlockSpec((1, tn), lambda j, i: (0, j))],
        out_specs=[pl.BlockSpec((tm, tn), lambda j, i: (i, j)),
                   pl.BlockSpec((tm, tn), lambda j, i: (i, j))],
        out_shape=[jax.ShapeDtypeStruct((T, N), F32), jax.ShapeDtypeStruct((T, N), BF16)],
        compiler_params=_cparams(("parallel", "parallel")),
        name="inproj",
    )(h, w, b)


def _merge_kernel(ofp_ref, osp_ref, onp_ref, ofs_ref, oss_ref, ons_ref, zf_ref, zs_ref, zn_ref,
                  g0_ref, g1_ref, g2_ref, x_ref, wbr_ref, wout_ref, gn_ref, xo_ref, ho_ref, *, n_prompt_tiles):
    is_prompt = pl.program_id(0) < n_prompt_tiles

    def branch(op_ref, os_ref, z_ref, g_ref, k):
        z = z_ref[...]
        o = jnp.where(is_prompt, op_ref[...], os_ref[...])
        u = (o * (z * jax.nn.sigmoid(z))).astype(BF16)
        return jax.nn.sigmoid(g_ref[...]) * _dot(u, wbr_ref[k])

    y = (branch(ofp_ref, ofs_ref, zf_ref, g0_ref, 0) + branch(osp_ref, oss_ref, zs_ref, g1_ref, 1)
         + branch(onp_ref, ons_ref, zn_ref, g2_ref, 2))
    xn = x_ref[...] + _dot(y.astype(BF16), wout_ref[...])
    xo_ref[...] = xn
    ho_ref[...] = _rmsnorm_body(xn, gn_ref[...]).astype(ho_ref.dtype)


def _merge(o_prompt, o_sample, p32, x, wbr, wout, g_next, h_dtype):
    T, D = x.shape
    TP, W = o_prompt[0].shape
    TS = o_sample[0].shape[0]
    tm = _tile(math.gcd(TP, TS), 256, 16)
    n_p = TP // tm
    row = lambda c: (lambda i: (i, c))
    prompt_rows = pl.BlockSpec((tm, W), lambda i: (jnp.minimum(i, n_p - 1), 0))
    sample_rows = pl.BlockSpec((tm, W), lambda i: (jnp.maximum(i - n_p, 0), 0))
    return pl.pallas_call(
        functools.partial(_merge_kernel, n_prompt_tiles=n_p),
        grid=(T // tm,),
        in_specs=[prompt_rows] * 3 + [sample_rows] * 3 + [
                  pl.BlockSpec((tm, W), row(C_ZF // W)), pl.BlockSpec((tm, W), row(C_ZS // W)),
                  pl.BlockSpec((tm, W), row(C_ZN // W)),
                  pl.BlockSpec((tm, D), row(C_MERGE // D)), pl.BlockSpec((tm, D), row(C_MERGE // D + 1)),
                  pl.BlockSpec((tm, D), row(C_MERGE // D + 2)),
                  pl.BlockSpec((tm, D), row(0)),
                  pl.BlockSpec((3, W, D), lambda i: (0, 0, 0)),
                  pl.BlockSpec((D, D), lambda i: (0, 0)),
                  pl.BlockSpec((1, D), lambda i: (0, 0))],
        out_specs=[pl.BlockSpec((tm, D), row(0)), pl.BlockSpec((tm, D), row(0))],
        out_shape=[jax.ShapeDtypeStruct((T, D), F32), jax.ShapeDtypeStruct((T, D), h_dtype)],
        compiler_params=_cparams(("parallel",)),
        name="merge_out",
    )(*o_prompt, *o_sample, p32, p32, p32, p32, p32, p32, x, wbr, wout, g_next.reshape(1, D))


KV_LEAVES = ((C_FOXK, FOX_KV), (C_FOXV, FOX_KV), (C_SBK, SB_KV), (C_SBV, SB_KV)) + tuple(
    (C_NKV + c * LANES, NSA_KV) for c in range(6))


def _kv_rows_kernel(*refs):
    n = len(KV_LEAVES)
    for i in range(n):
        refs[2 * n + i][0, 0] = refs[i][...].T


def _kv_rows(p32, bufs, layer, B, S):
    n = len(KV_LEAVES)
    ts = _tile(S, 512, LANES)
    nst = S // ts
    widths = [kv * HEAD_DIM for _, kv in KV_LEAVES]
    return pl.pallas_call(
        _kv_rows_kernel,
        grid=(B, nst),
        in_specs=[pl.BlockSpec((ts, w), lambda b, s, c=c0 // w: (b * nst + s, c)) for (c0, _), w in zip(KV_LEAVES, widths)]
        + [pl.BlockSpec(memory_space=pl.ANY)] * n,
        out_specs=[pl.BlockSpec((1, 1, w, ts), lambda b, s: (layer, b, 0, s)) for w in widths],
        out_shape=[jax.ShapeDtypeStruct(buf.shape, F32) for buf in bufs],
        input_output_aliases={n + i: i for i in range(n)},
        compiler_params=_cparams(("parallel", "parallel")),
        name="kv_rows",
    )(*([p32] * n), *bufs)


def _foxgate_kernel(sm_ref, logf_ref, ft_ref, carry_ref, *, ts):
    @pl.when(pl.program_id(1) == 0)
    def _():
        carry_ref[...] = jnp.zeros_like(carry_ref)

    lf = _log_sigmoid(sm_ref[...])
    logf_ref[...] = lf[:, :FOX_HEADS]
    lft = lf.T[:FOX_HEADS, :]
    upper = (_iota((ts, ts), 0) <= _iota((ts, ts), 1)).astype(BF16)
    c = _dot_exact(lft, upper) + carry_ref[:, 0:1]
    ft_ref[0] = c
    carry_ref[...] = jnp.broadcast_to(c[:, ts - 1:ts], carry_ref.shape)


def _foxgate(p32, B, S):
    ts = _tile(S, 512, 128)
    nst = S // ts
    return pl.pallas_call(
        functools.partial(_foxgate_kernel, ts=ts),
        grid=(B, nst),
        in_specs=[pl.BlockSpec((ts, LANES), lambda b, s: (b * nst + s, C_SMALL // LANES))],
        out_specs=[pl.BlockSpec((ts, FOX_HEADS), lambda b, s: (b * nst + s, 0)),
                   pl.BlockSpec((1, FOX_HEADS, ts), lambda b, s: (b, 0, s))],
        out_shape=[jax.ShapeDtypeStruct((B * S, FOX_HEADS), F32), jax.ShapeDtypeStruct((B, FOX_HEADS, S), F32)],
        scratch_shapes=[pltpu.VMEM((FOX_HEADS, LANES), F32)],
        compiler_params=_cparams(("parallel", "arbitrary")),
        name="fox_gate",
    )(p32)


def _pair_queries(q_ref, qz_ref):
    tq = q_ref.shape[0]
    low = _iota((tq, LANES), 1) < HEAD_DIM
    qa, qb = q_ref[:, :LANES], q_ref[:, LANES:]
    qz_ref[0] = jnp.where(low, qa, 0.0).astype(BF16)
    qz_ref[1] = jnp.where(low, _swap_halves(qa), 0.0).astype(BF16)
    qz_ref[2] = jnp.where(low, 0.0, _swap_halves(qb)).astype(BF16)
    qz_ref[3] = jnp.where(low, 0.0, qb).astype(BF16)


def _pair_output(a0, a1, a2, a3, o_ref):
    low = _iota(a0.shape, 1) < HEAD_DIM
    o_ref[:, :LANES] = jnp.where(low, a0, _swap_halves(a1))
    o_ref[:, LANES:] = jnp.where(low, _swap_halves(a2), a3)


def _rep(x, k):
    return x if k == 1 else jnp.concatenate([x] * k, axis=1)


def _stack_rows(row_vectors, rows):
    n = row_vectors[0].shape[1]
    return jnp.concatenate([jnp.broadcast_to(v, (rows, n)) for v in row_vectors], axis=0)


def _softmax_update(s, pv_fn, m_ref, l_ref, acc_ref, idx=Ellipsis):
    m_prev = m_ref[idx]
    m_new = jnp.maximum(m_prev, jnp.max(s, axis=1, keepdims=True))
    alpha = jnp.exp(m_prev - m_new)
    p = jnp.exp(s - _rep(m_new, s.shape[1] // LANES))
    l_ref[idx] = alpha * l_ref[idx] + jnp.sum(p, axis=1, keepdims=True)
    acc = acc_ref[idx]
    acc_ref[idx] = _rep(alpha, acc.shape[1] // LANES) * acc + pv_fn(p.astype(BF16))
    m_ref[idx] = m_new


def _fox_prompt_kernel(q_ref, k_ref, v_ref, ft_ref, o_ref, qz_ref, m_ref, l_ref, acc_ref, s_ref, *, tq, tk):
    qi = pl.program_id(2)
    q0 = qi * tq
    _pair_queries(q_ref, qz_ref)
    m_ref[...] = jnp.full(m_ref.shape, NEG, F32)
    l_ref[...] = jnp.zeros_like(l_ref)
    acc_ref[...] = jnp.zeros_like(acc_ref)
    qz = qz_ref[...].reshape(4 * tq, LANES)

    def scores(kj):
        k0 = pl.multiple_of(kj * tk, tk)
        fk = ft_ref[0, 0, kj]
        return _dot_nt(qz, k_ref[pl.ds(k0, tk), :]) - _stack_rows([fk[hh:hh + 1, :] for hh in range(4)], tq)

    def consume(kj, s, diag):
        k0 = pl.multiple_of(kj * tk, tk)
        vt = v_ref[pl.ds(k0, tk), :]
        if diag:
            t_idx = jnp.bitwise_and(_iota((4 * tq, tk), 0), tq - 1)
            s = jnp.where(k0 + _iota((4 * tq, tk), 1) <= q0 + t_idx, s, NEG)
        _softmax_update(s, lambda p: _dot(p, vt), m_ref, l_ref, acc_ref)

    n_full = q0 // tk
    s_ref[...] = scores(0)

    def body(kj, carry):
        s = s_ref[...]
        s_ref[...] = scores(kj + 1)
        consume(kj, s, False)
        return carry

    lax.fori_loop(0, n_full, body, 0)
    consume(n_full, s_ref[...], True)
    out = acc_ref[...] / l_ref[...]
    _pair_output(*[out[hh * tq:(hh + 1) * tq] for hh in range(4)], o_ref)


def _fox_prompt(p32, p16, ft4, B, S, tq, tk):
    nq = S // tq
    n_pairs = FOX_KV // 2
    W = FOX_HEADS * HEAD_DIM
    return pl.pallas_call(
        functools.partial(_fox_prompt_kernel, tq=tq, tk=tk),
        grid=(B, n_pairs, nq),
        in_specs=[pl.BlockSpec((tq, 2 * LANES), lambda b, g, i: (b * nq + i, C_FOXQ // (2 * LANES) + g)),
                  pl.BlockSpec((S, LANES), lambda b, g, i: (b, C_FOXK // LANES + g)),
                  pl.BlockSpec((S, LANES), lambda b, g, i: (b, C_FOXV // LANES + g)),
                  pl.BlockSpec((1, 1, S // tk, 4, tk), lambda b, g, i: (b, g, 0, 0, 0))],
        out_specs=pl.BlockSpec((tq, 2 * LANES), lambda b, g, i: (b * nq + i, g)),
        out_shape=jax.ShapeDtypeStruct((B * S, W), F32),
        scratch_shapes=[pltpu.VMEM((4, tq, LANES), BF16), pltpu.VMEM((4 * tq, LANES), F32),
                        pltpu.VMEM((4 * tq, LANES), F32), pltpu.VMEM((4 * tq, LANES), F32),
                        pltpu.VMEM((4 * tq, tk), F32)],
        compiler_params=_cparams(("parallel", "parallel", "arbitrary")),
        name="fox_prompt",
    )(p32, p16, p16, ft4)


def _suffix_and_total(lk, uo):
    hi = lk.astype(BF16)
    lo = (lk - hi.astype(F32)).astype(BF16)
    r = _dot(hi, uo) + _dot(lo, uo)
    return r[:, :LANES], r[:, LANES:]


def _suffix_matrix():
    j = _iota((LANES, LANES), 0)
    s = _iota((LANES, LANES), 1)
    return jnp.concatenate([(j > s).astype(BF16), jnp.ones((LANES, LANES), BF16)], axis=1)


def _sb_blocks(zs, carry, uo, masks):
    running = carry
    out = []
    for z, mask in zip(zs, masks):
        sp = _softplus(z)
        lk = -sp
        if mask is not None:
            lk = jnp.where(mask, lk, 0.0)
        after, tot = _suffix_and_total(lk, uo)
        a = jnp.exp(z - sp + after + running)
        if mask is not None:
            a = jnp.where(mask, a, 0.0)
        out.append(a)
        running = running + tot
    return out, running


def _sb_prompt_kernel(q_ref, k_ref, v_ref, o_ref, qz_ref, carry_ref, acc_ref, *, tq, tk):
    qi = pl.program_id(2)
    q0 = qi * tq
    _pair_queries(q_ref, qz_ref)
    carry_ref[...] = jnp.zeros_like(carry_ref)
    acc_ref[...] = jnp.zeros_like(acc_ref)
    qz = qz_ref[...].reshape(4 * tq, LANES)
    uo = _suffix_matrix()
    S = k_ref.shape[0]
    first = min(tq + tk, S)

    def tile(k0, width, masked):
        kt = k_ref[pl.ds(k0, width), :]
        vt = v_ref[pl.ds(k0, width), :]
        z = _dot_nt(qz, kt)
        nsub = width // LANES
        order = list(reversed(range(nsub)))
        masks = [None] * nsub
        if masked:
            t_idx = jnp.bitwise_and(_iota((4 * tq, LANES), 0), tq - 1)
            col = _iota((4 * tq, LANES), 1)
            masks = [k0 + u * LANES + col < q0 + t_idx for u in order]
        ws, running = _sb_blocks([z[:, u * LANES:(u + 1) * LANES] for u in order], carry_ref[...], uo, masks)
        a = ws[0] if nsub == 1 else jnp.concatenate(list(reversed(ws)), axis=1)
        acc_ref[...] = acc_ref[...] + _dot(a.astype(BF16), vt)
        carry_ref[...] = running
        return jnp.max(running)

    k_first = pl.multiple_of(jnp.maximum(q0 - tk, 0), LANES)
    top0 = tile(k_first, first, True)

    def cond(st):
        return jnp.logical_and(st[0] > 0, st[1] > EXP_ZERO)

    def body(st):
        k0 = pl.multiple_of(st[0] - LANES, LANES)
        return k0, tile(k0, LANES, False)

    lax.while_loop(cond, body, (k_first, top0))
    acc = acc_ref[...]
    _pair_output(*[acc[hh * tq:(hh + 1) * tq] for hh in range(4)], o_ref)


def _sb_prompt(p32, p16, B, S, tq, tk):
    nq = S // tq
    n_pairs = SB_KV // 2
    W = SB_HEADS * HEAD_DIM
    return pl.pallas_call(
        functools.partial(_sb_prompt_kernel, tq=tq, tk=tk),
        grid=(B, n_pairs, nq),
        in_specs=[pl.BlockSpec((tq, 2 * LANES), lambda b, g, i: (b * nq + i, C_SBQ // (2 * LANES) + g)),
                  pl.BlockSpec((S, LANES), lambda b, g, i: (b, C_SBK // LANES + g)),
                  pl.BlockSpec((S, LANES), lambda b, g, i: (b, C_SBV // LANES + g))],
        out_specs=pl.BlockSpec((tq, 2 * LANES), lambda b, g, i: (b * nq + i, g)),
        out_shape=jax.ShapeDtypeStruct((B * S, W), F32),
        scratch_shapes=[pltpu.VMEM((4, tq, LANES), BF16), pltpu.VMEM((4 * tq, LANES), F32),
                        pltpu.VMEM((4 * tq, LANES), F32)],
        compiler_params=_cparams(("parallel", "parallel", "arbitrary")),
        name="sb_prompt",
    )(p32, p16, p16)


def _nsa_group_queries(q_ref, qg_ref, rows):
    low = _iota((rows, LANES), 1) < HEAD_DIM
    rper = NSA_HEADS // NSA_KV
    for g in range(NSA_KV):
        for r in range(rper):
            h = rper * g + r
            blk = q_ref[:, (h // 2) * LANES:(h // 2 + 1) * LANES]
            if h % 2 != g:
                blk = _swap_halves(blk)
            keep = low if g == 0 else jnp.logical_not(low)
            qg_ref[g, r] = jnp.where(keep, blk, 0.0).astype(qg_ref.dtype)


def _nsa_output(tot, o_ref):
    low = _iota(tot[0][0].shape, 1) < HEAD_DIM
    for j in range(NSA_HEADS // 2):
        g = j // 2
        a, b = tot[g][2 * (j % 2)], tot[g][2 * (j % 2) + 1]
        if g == 0:
            blk = jnp.where(low, a, _swap_halves(b))
        else:
            blk = jnp.where(low, _swap_halves(a), b)
        o_ref[:, j * LANES:(j + 1) * LANES] = blk


def _topk_select_t(score_t, allowed_t, n_real, k):
    nbp = score_t.shape[0]
    sc = score_t[:n_real]
    n_idx = _iota(sc.shape, 0)
    cnt = jnp.zeros(sc.shape, F32)
    for m in range(n_real):
        row = sc[m:m + 1, :]
        cnt = cnt + jnp.where(n_idx > m, jnp.where(row >= sc, 1.0, 0.0), jnp.where(row > sc, 1.0, 0.0))
    sel = jnp.where(jnp.logical_and(cnt < k, allowed_t[:n_real]), 1.0, 0.0)
    if nbp > n_real:
        sel = jnp.concatenate([sel, jnp.zeros((nbp - n_real, sc.shape[1]), F32)], axis=0)
    return sel


def _nsa_key_table(S):
    assert S // NSA_BLOCK <= ALIBI_LANE
    pos = np.arange(S)
    t = np.zeros((S, LANES), np.float32)
    t[pos, pos // NSA_BLOCK] = NEG
    t[:, ALIBI_LANE] = pos // NSA_BLOCK
    t[:, ALIBI_LANE + 1] = pos % NSA_BLOCK
    return jnp.asarray(t, BF16)


def _nsa_prompt_kernel(q_ref, kc_ref, vc_ref, ks_ref, vs_ref, kw_ref, vw_ref, sm_ref, wk_ref, wv_ref,
                       et_ref, o_ref, kcb_ref, vcb_ref, qg_ref, lhs_ref, m_ref, l_ref, acc_ref, tot_ref,
                       *, tq, tk, S, win):
    qi = pl.program_id(1)
    q0 = qi * tq
    nb = S // NSA_BLOCK
    nbp = kcb_ref.shape[0]
    rper = NSA_HEADS // NSA_KV
    rows = rper * tq

    @pl.when(qi == 0)
    def _():
        def pool(x_ref, w_ref):
            x = x_ref[...].astype(F32).reshape(nb, NSA_BLOCK, LANES)
            c = jnp.sum(x * w_ref[...][None], axis=1)
            if nbp > nb:
                c = jnp.concatenate([c, jnp.zeros((nbp - nb, LANES), F32)], axis=0)
            return c.astype(BF16)
        kcb_ref[...] = pool(kc_ref, wk_ref)
        vcb_ref[...] = pool(vc_ref, wv_ref)

    _nsa_group_queries(q_ref, qg_ref, tq)
    sig = jax.nn.sigmoid(sm_ref[...])

    def gate(g, branch):
        cols = [GATE_LANE0 + 3 * (rper * g + r) + branch for r in range(rper)]
        return jnp.concatenate([jnp.broadcast_to(sig[:, c:c + 1], (tq, LANES)) for c in cols], axis=0)

    def group_q(g):
        return qg_ref[g].reshape(rows, LANES)

    lane = _iota((tq, LANES), 1)
    n_i = _iota((rows, nbp), 1)
    t_i = q0 + jnp.bitwise_and(_iota((rows, nbp), 0), tq - 1)
    blk_end = (n_i + 1) * NSA_BLOCK - 1
    dist_c = (t_i - blk_end).astype(F32)
    valid_c = blk_end <= t_i
    n_t = _iota((nbp, tq), 0)
    cur_t = lax.shift_right_logical(q0 + _iota((nbp, tq), 1), BLOCK_SHIFT)
    allowed_t = n_t <= cur_t
    forced_t = jnp.logical_or(n_t == 0, jnp.logical_or(n_t == cur_t, n_t == cur_t - 1))
    for g in range(NSA_KV):
        slope_rows = _stack_rows([jnp.full((1, nbp), NSA_SLOPES[rper * g + r], F32) for r in range(rper)], tq)
        sc = _dot_nt(group_q(g), kcb_ref[...]) - slope_rows * dist_c
        sc = jnp.where(valid_c, sc, NEG)
        mx = jnp.max(sc, axis=1, keepdims=True)
        e = jnp.where(valid_c, jnp.exp(sc - mx), 0.0)
        pc = e / jnp.maximum(jnp.sum(e, axis=1, keepdims=True), jnp.finfo(F32).tiny)
        tot_ref[g] = gate(g, 0) * _dot(pc.astype(BF16), vcb_ref[...])
        imp = pc[0:tq]
        for r in range(1, rper):
            imp = imp + pc[r * tq:(r + 1) * tq]
        score_t = jnp.where(allowed_t, jnp.where(forced_t, BIG_SCORE, imp.T), -1.0)
        not_sel = 1.0 - _topk_select_t(score_t, allowed_t, nb, NSA_TOPK).T
        for r in range(rper):
            slope = NSA_SLOPES[rper * g + r]
            alibi = jnp.where(lane == ALIBI_LANE, NSA_BLOCK * slope, jnp.where(lane == ALIBI_LANE + 1, slope, 0.0))
            lhs_ref[0, g, r * tq:(r + 1) * tq, LANES:] = jnp.where(lane < ALIBI_LANE, not_sel, alibi).astype(BF16)
            lhs_ref[1, g, r * tq:(r + 1) * tq, LANES:] = alibi.astype(BF16)
        lhs_ref[0, g, :, :LANES] = group_q(g)
        lhs_ref[1, g, :, :LANES] = group_q(g)

    m_ref[...] = jnp.full(m_ref.shape, NEG, F32)
    l_ref[...] = jnp.zeros_like(l_ref)
    acc_ref[...] = jnp.zeros_like(acc_ref)

    def sel_tile(kj, diag):
        k0 = pl.multiple_of(kj * tk, tk)
        rhs = jnp.concatenate([ks_ref[pl.ds(k0, tk), :], et_ref[pl.ds(k0, tk), :]], axis=1)
        vt = vs_ref[pl.ds(k0, tk), :]
        if diag:
            causal = (q0 - k0) + jnp.bitwise_and(_iota((rows, tk), 0), tq - 1) - _iota((rows, tk), 1) >= 0
        ss = [_dot_nt(lhs_ref[0, g], rhs) for g in range(NSA_KV)]
        for g in range(NSA_KV):
            s = ss[g]
            if diag:
                s = jnp.where(causal, s, NEG)
            _softmax_update(s, lambda p: _dot(p, vt), m_ref, l_ref, acc_ref, g)

    def sel_body(kj, c):
        sel_tile(kj, False)
        return c

    n_full = q0 // tk
    lax.fori_loop(0, n_full, sel_body, 0)
    sel_tile(n_full, True)

    span = win + tq
    w0 = pl.multiple_of(jnp.maximum(q0 - win, 0), tq)
    rhs_w = jnp.concatenate([kw_ref[pl.ds(w0, span), :], et_ref[pl.ds(w0, span), :]], axis=1)
    vw = vw_ref[pl.ds(w0, span), :]
    dist = (q0 - w0) + jnp.bitwise_and(_iota((rows, span), 0), tq - 1) - _iota((rows, span), 1)
    in_win = jnp.logical_and(dist >= 0, dist < win)
    for g in range(NSA_KV):
        s = jnp.where(in_win, _dot_nt(lhs_ref[1, g], rhs_w), NEG)
        p = jnp.exp(s - jnp.max(s, axis=1, keepdims=True))
        o_win = _dot(p.astype(BF16), vw) / jnp.sum(p, axis=1, keepdims=True)
        tot_ref[g] = tot_ref[g] + gate(g, 1) * (acc_ref[g] / l_ref[g]) + gate(g, 2) * o_win

    _nsa_output([[tot_ref[g, r * tq:(r + 1) * tq, :] for r in range(rper)] for g in range(NSA_KV)], o_ref)


def _nsa_prompt(p32, p16, wk_exp, wv_exp, key_table, B, S, tq, tk, win):
    nq = S // tq
    nbp = LANES
    W = NSA_HEADS * HEAD_DIM
    rper = NSA_HEADS // NSA_KV
    assert S >= win + tq
    kv = lambda c: pl.BlockSpec((S, LANES), lambda b, i: (b, C_NKV // LANES + c))
    full = lambda a: pl.BlockSpec(a.shape, lambda b, i: (0,) * a.ndim)
    return pl.pallas_call(
        functools.partial(_nsa_prompt_kernel, tq=tq, tk=tk, S=S, win=win),
        grid=(B, nq),
        in_specs=[pl.BlockSpec((tq, W), lambda b, i: (b * nq + i, C_NSAQ // W)),
                  kv(0), kv(1), kv(2), kv(3), kv(4), kv(5),
                  pl.BlockSpec((tq, LANES), lambda b, i: (b * nq + i, C_SMALL // LANES)),
                  full(wk_exp), full(wv_exp), full(key_table)],
        out_specs=pl.BlockSpec((tq, W), lambda b, i: (b * nq + i, 0)),
        out_shape=jax.ShapeDtypeStruct((B * S, W), F32),
        scratch_shapes=[pltpu.VMEM((nbp, LANES), BF16), pltpu.VMEM((nbp, LANES), BF16),
                        pltpu.VMEM((NSA_KV, rper, tq, LANES), BF16),
                        pltpu.VMEM((2, NSA_KV, rper * tq, 2 * LANES), BF16),
                        pltpu.VMEM((NSA_KV, rper * tq, LANES), F32),
                        pltpu.VMEM((NSA_KV, rper * tq, LANES), F32),
                        pltpu.VMEM((NSA_KV, rper * tq, LANES), F32),
                        pltpu.VMEM((NSA_KV, rper * tq, LANES), F32)],
        compiler_params=_cparams(("parallel", "arbitrary"), 56),
        name="nsa_prompt",
    )(p32, p16, p16, p16, p16, p16, p16, p32, wk_exp, wv_exp, key_table)


def _blockdiag_queries(q_ref, qbd_ref, n_heads, n_kv):
    sd = q_ref.shape[0]
    rper = n_heads // n_kv
    width = n_kv * HEAD_DIM
    low = _iota((sd, LANES), 1) < HEAD_DIM
    for h in range(n_heads):
        g = h // rper
        blk = q_ref[:, (h // 2) * LANES:(h // 2 + 1) * LANES]
        if h % 2 != g % 2:
            blk = _swap_halves(blk)
        keep = low if g % 2 == 0 else jnp.logical_not(low)
        blk = jnp.where(keep, blk, 0.0)
        parts = [blk if c == g // 2 else jnp.zeros((sd, LANES), F32) for c in range(width // LANES)]
        row = parts[0] if len(parts) == 1 else jnp.concatenate(parts, axis=1)
        qbd_ref[h * sd:(h + 1) * sd, :] = row.astype(BF16)


def _blockdiag_output(acc, o_ref, n_heads, n_kv, sd):
    rper = n_heads // n_kv
    low = _iota((sd, LANES), 1) < HEAD_DIM
    for j in range(n_heads // 2):
        pieces = []
        for h in (2 * j, 2 * j + 1):
            g = h // rper
            p = acc[h * sd:(h + 1) * sd, (g // 2) * LANES:(g // 2 + 1) * LANES]
            if g % 2 != h % 2:
                p = _swap_halves(p)
            pieces.append(p)
        o_ref[:, j * LANES:(j + 1) * LANES] = jnp.where(low, pieces[0], pieces[1])


def _pad_rows(x, rows):
    return jnp.concatenate([x, jnp.zeros((rows - x.shape[0], x.shape[1]), x.dtype)], axis=0)


def _page_specs(n, shape, index_of):
    return [pl.BlockSpec(shape, index_of(u)) for u in range(n)]


def _expand_heads(x8, sd):
    return jnp.concatenate([jnp.broadcast_to(x8[h:h + 1, :], (sd, x8.shape[1])) for h in range(x8.shape[0])], axis=0)


def _transposed_pool(pool):
    d, n, page, kv, hd = pool.shape
    return pool.transpose(0, 1, 3, 4, 2).reshape(d, n, kv * hd, page)


def _fox_decode_kernel(pt_ref, q_ref, kn_ref, vn_ref, sm_ref, lf_ref, *rest, sd, n_steps, pp):
    kt_refs, vt_refs = rest[:pp], rest[pp:2 * pp]
    o_ref, lfo_ref, qbd_ref, m_ref, l_ref, acc_ref, cd_ref = rest[2 * pp:]
    b = pl.program_id(0)
    p = pl.program_id(1)
    n_pages = n_steps * pp
    rows = FOX_HEADS * sd
    uo = _suffix_matrix()

    @pl.when(p == 0)
    def _():
        _blockdiag_queries(q_ref, qbd_ref, FOX_HEADS, FOX_KV)
        head_of_row = lax.shift_right_logical(_iota((rows, LANES), 0), SD_SHIFT)
        i_of_row = jnp.bitwise_and(_iota((rows, LANES), 0), sd - 1)
        col = _iota((rows, LANES), 1)
        lf = _log_sigmoid(sm_ref[...])
        lfo_ref[...] = lf[:, :FOX_HEADS]
        sel_h = jnp.where(col == head_of_row, 1.0, 0.0).astype(BF16)
        a = _dot_nt_exact_rhs(sel_h, _pad_rows(lf, LANES))
        incl = (_iota((LANES, LANES), 0) <= _iota((LANES, LANES), 1)).astype(BF16)
        bias_new = -_dot_exact(a, incl)
        kn = _pad_rows(kn_ref[...], LANES).astype(BF16)
        vn = _pad_rows(vn_ref[...], LANES).astype(BF16)
        s = _dot_nt(qbd_ref[...], kn) + bias_new
        s = jnp.where(col <= i_of_row, s, NEG)
        mx = jnp.max(s, axis=1, keepdims=True)
        pr = jnp.exp(s - mx)
        m_ref[...] = jnp.broadcast_to(mx, m_ref.shape)
        l_ref[...] = jnp.broadcast_to(jnp.sum(pr, axis=1, keepdims=True), l_ref.shape)
        acc_ref[...] = _dot(pr.astype(BF16), vn)
        cd_ref[...] = jnp.zeros_like(cd_ref)

    later = cd_ref[...]
    qbd = qbd_ref[...]
    parts = []
    for u in range(pp):
        lf_page = lf_ref[0, pt_ref[b, n_pages - 1 - (p * pp + u)]]
        r = _dot_exact(lf_page, uo)
        bias = _expand_heads(r[:, :LANES] + later, sd)
        later = later + r[:, LANES:]
        parts.append(_dot(qbd, kt_refs[u][0, 0].astype(BF16)) + bias)
    cd_ref[...] = later
    s = parts[0] if pp == 1 else jnp.concatenate(parts, axis=1)

    def pv(pr):
        out = _dot_nt(pr[:, :LANES], vt_refs[0][0, 0].astype(BF16))
        for u in range(1, pp):
            out = out + _dot_nt(pr[:, u * LANES:(u + 1) * LANES], vt_refs[u][0, 0].astype(BF16))
        return out

    _softmax_update(s, pv, m_ref, l_ref, acc_ref)

    @pl.when(p == n_steps - 1)
    def _():
        acc = acc_ref[...] / _rep(l_ref[...], acc_ref.shape[1] // LANES)
        _blockdiag_output(acc, o_ref, FOX_HEADS, FOX_KV, sd)


def _fox_decode(p32, page_table, pool_k, pool_v, pool_lf, layer, row0, DB, sd, pp):
    n_pages = page_table.shape[1]
    n_steps = n_pages // pp
    page = pool_k.shape[2]
    Wq, Wk = FOX_HEADS * HEAD_DIM, FOX_KV * HEAD_DIM
    rows = FOX_HEADS * sd
    rb = row0 // sd
    pkt, pvt = _transposed_pool(pool_k), _transposed_pool(pool_v)
    plf = pool_lf.transpose(0, 1, 3, 2)
    pg = lambda u: (lambda b, p, pt: (layer, pt[b, n_pages - 1 - (p * pp + u)], 0, 0))
    grid_spec = pltpu.PrefetchScalarGridSpec(
        num_scalar_prefetch=1,
        grid=(DB, n_steps),
        in_specs=[pl.BlockSpec((sd, Wq), lambda b, p, pt: (rb + b, C_FOXQ // Wq)),
                  pl.BlockSpec((sd, Wk), lambda b, p, pt: (rb + b, C_FOXK // Wk)),
                  pl.BlockSpec((sd, Wk), lambda b, p, pt: (rb + b, C_FOXV // Wk)),
                  pl.BlockSpec((sd, LANES), lambda b, p, pt: (rb + b, C_SMALL // LANES)),
                  pl.BlockSpec((1,) + plf.shape[1:], lambda b, p, pt: (layer, 0, 0, 0))]
        + _page_specs(pp, (1, 1, Wk, page), pg) + _page_specs(pp, (1, 1, Wk, page), pg),
        out_specs=[pl.BlockSpec((sd, Wq), lambda b, p, pt: (b, 0)),
                   pl.BlockSpec((sd, FOX_HEADS), lambda b, p, pt: (b, 0))],
        scratch_shapes=[pltpu.VMEM((rows, Wk), BF16), pltpu.VMEM((rows, LANES), F32),
                        pltpu.VMEM((rows, LANES), F32), pltpu.VMEM((rows, Wk), F32),
                        pltpu.VMEM((FOX_HEADS, LANES), F32)])
    return pl.pallas_call(
        functools.partial(_fox_decode_kernel, sd=sd, n_steps=n_steps, pp=pp),
        grid_spec=grid_spec,
        out_shape=[jax.ShapeDtypeStruct((DB * sd, Wq), F32), jax.ShapeDtypeStruct((DB * sd, FOX_HEADS), F32)],
        compiler_params=_cparams(("parallel", "arbitrary"), 56),
        name="fox_decode",
    )(page_table, p32, p32, p32, p32, plf, *([pkt] * pp), *([pvt] * pp))


def _sb_decode_kernel(pt_ref, q_ref, kn_ref, vn_ref, *rest, sd, n_steps, pp):
    del pt_ref
    kt_refs, vt_refs = rest[:pp], rest[pp:2 * pp]
    o_ref, qbd_ref, carry_ref, acc_ref, top_ref = rest[2 * pp:]
    p = pl.program_id(1)
    rows = SB_HEADS * sd
    uo = _suffix_matrix()

    @pl.when(p == 0)
    def _():
        _blockdiag_queries(q_ref, qbd_ref, SB_HEADS, SB_KV)
        i_of_row = jnp.bitwise_and(_iota((rows, LANES), 0), sd - 1)
        strict = _iota((rows, LANES), 1) < i_of_row
        kn = _pad_rows(kn_ref[...], LANES).astype(BF16)
        vn = _pad_rows(vn_ref[...], LANES).astype(BF16)
        ws, c = _sb_blocks([_dot_nt(qbd_ref[...], kn)], jnp.zeros((rows, LANES), F32), uo, [strict])
        acc_ref[...] = _dot(ws[0].astype(BF16), vn)
        carry_ref[...] = c
        top_ref[0] = jnp.max(c)

    @pl.when(top_ref[0] > EXP_ZERO)
    def _():
        qbd = qbd_ref[...]
        zs = [_dot(qbd, kt_refs[u][0, 0].astype(BF16)) for u in range(pp)]
        ws, c = _sb_blocks(zs, carry_ref[...], uo, [None] * pp)
        acc = acc_ref[...]
        for u in range(pp):
            acc = acc + _dot_nt(ws[u].astype(BF16), vt_refs[u][0, 0].astype(BF16))
        acc_ref[...] = acc
        carry_ref[...] = c
        top_ref[0] = jnp.max(c)

    @pl.when(p == n_steps - 1)
    def _():
        _blockdiag_output(acc_ref[...], o_ref, SB_HEADS, SB_KV, sd)


def _sb_decode(p32, page_table, pool_k, pool_v, layer, row0, DB, sd, pp):
    n_pages = page_table.shape[1]
    n_steps = n_pages // pp
    page = pool_k.shape[2]
    Wq, Wk = SB_HEADS * HEAD_DIM, SB_KV * HEAD_DIM
    rows = SB_HEADS * sd
    rb = row0 // sd
    pkt, pvt = _transposed_pool(pool_k), _transposed_pool(pool_v)
    pg = lambda u: (lambda b, p, pt: (layer, pt[b, n_pages - 1 - (p * pp + u)], 0, 0))
    grid_spec = pltpu.PrefetchScalarGridSpec(
        num_scalar_prefetch=1,
        grid=(DB, n_steps),
        in_specs=[pl.BlockSpec((sd, Wq), lambda b, p, pt: (rb + b, C_SBQ // Wq)),
                  pl.BlockSpec((sd, Wk), lambda b, p, pt: (rb + b, C_SBK // Wk)),
                  pl.BlockSpec((sd, Wk), lambda b, p, pt: (rb + b, C_SBV // Wk))]
        + _page_specs(pp, (1, 1, Wk, page), pg) + _page_specs(pp, (1, 1, Wk, page), pg),
        out_specs=pl.BlockSpec((sd, Wq), lambda b, p, pt: (b, 0)),
        scratch_shapes=[pltpu.VMEM((rows, Wk), BF16), pltpu.VMEM((rows, LANES), F32),
                        pltpu.VMEM((rows, Wk), F32), pltpu.SMEM((1,), F32)])
    return pl.pallas_call(
        functools.partial(_sb_decode_kernel, sd=sd, n_steps=n_steps, pp=pp),
        grid_spec=grid_spec,
        out_shape=jax.ShapeDtypeStruct((DB * sd, Wq), F32),
        compiler_params=_cparams(("parallel", "arbitrary")),
        name="sb_decode",
    )(page_table, p32, p32, p32, *([pkt] * pp), *([pvt] * pp))


def _nsa_decode_slopes(rows, g, width=LANES):
    rper = NSA_HEADS // NSA_KV
    rr = lax.shift_right_logical(_iota((rows, width), 0), SD_SHIFT)
    s = jnp.full((rows, width), NSA_SLOPES[rper * g], F32)
    for r in range(1, rper):
        s = jnp.where(rr == r, NSA_SLOPES[rper * g + r], s)
    return s


def _nsa_decode_group_q(qg_ref, g):
    rper = NSA_HEADS // NSA_KV
    return jnp.concatenate([qg_ref[g, r] for r in range(rper)], axis=0).astype(BF16)


def _nsa_decode_pool_kernel(pt_ref, q_ref, sm_ref, wk_ref, wv_ref, *rest, sd, n_steps, pp, past):
    del pt_ref
    kc_refs, vc_refs = rest[:pp], rest[pp:2 * pp]
    tot_ref, selr_ref, kcb_ref, vcb_ref, qg_ref = rest[2 * pp:]
    j = pl.program_id(1)
    rper = NSA_HEADS // NSA_KV
    rows = rper * sd
    nbc = past // NSA_BLOCK
    page = kc_refs[0].shape[3]
    bpp = page // NSA_BLOCK
    i_row = jnp.bitwise_and(_iota((rows, LANES), 0), sd - 1)
    col = _iota((rows, LANES), 1)
    t_row = past + i_row
    slopes = functools.partial(_nsa_decode_slopes, rows)
    group_q = functools.partial(_nsa_decode_group_q, qg_ref)

    @pl.when(j == 0)
    def _():
        kcb_ref[...] = jnp.zeros_like(kcb_ref)
        vcb_ref[...] = jnp.zeros_like(vcb_ref)

    kacc = kcb_ref[...]
    vacc = vcb_ref[...]
    pos_blk = lax.shift_right_logical(_iota((LANES, LANES), 0), BLOCK_SHIFT)
    n_col = _iota((LANES, LANES), 1)
    for u in range(pp):
        ind = (n_col == (j * pp + u) * bpp + pos_blk).astype(BF16)
        pk = kc_refs[u][0, 0] * wk_ref[...]
        pv_ = vc_refs[u][0, 0] * wv_ref[...]
        kh = pk.astype(BF16)
        vh = pv_.astype(BF16)
        stack = jnp.concatenate([kh, (pk - kh.astype(F32)).astype(BF16),
                                 vh, (pv_ - vh.astype(F32)).astype(BF16)], axis=0)
        r = _dot(stack, ind)
        kacc = kacc + r[0:LANES] + r[LANES:2 * LANES]
        vacc = vacc + r[2 * LANES:3 * LANES] + r[3 * LANES:4 * LANES]
    kcb_ref[...] = kacc
    vcb_ref[...] = vacc

    @pl.when(j == n_steps - 1)
    def _():
        _nsa_group_queries(q_ref, qg_ref, sd)
        sig = jax.nn.sigmoid(sm_ref[...])
        blk_end = (col + 1) * NSA_BLOCK - 1
        dist_c = (t_row - blk_end).astype(F32)
        valid_c = jnp.logical_and(blk_end <= t_row, col < nbc)
        kcb = kcb_ref[...].astype(BF16)
        vcb = vcb_ref[...].astype(BF16)
        n_i = _iota((sd, LANES), 1)
        allowed = n_i < nbc
        forced = jnp.logical_or(n_i == 0, n_i == nbc - 1)
        m_idx = _iota((LANES, LANES), 0)
        n_idx = _iota((LANES, LANES), 1)
        for g in range(NSA_KV):
            sc = _dot(group_q(g), kcb) - slopes(g) * dist_c
            sc = jnp.where(valid_c, sc, NEG)
            mx = jnp.max(sc, axis=1, keepdims=True)
            e = jnp.where(valid_c, jnp.exp(sc - mx), 0.0)
            pc = e / jnp.maximum(jnp.sum(e, axis=1, keepdims=True), jnp.finfo(F32).tiny)
            gate_c = jnp.concatenate(
                [jnp.broadcast_to(sig[:, GATE_LANE0 + 3 * (rper * g + r):GATE_LANE0 + 3 * (rper * g + r) + 1],
                                  (sd, LANES)) for r in range(rper)], axis=0)
            tot_ref[0, g] = gate_c * _dot_nt(pc.astype(BF16), vcb)
            imp = pc[0:sd]
            for r in range(1, rper):
                imp = imp + pc[r * sd:(r + 1) * sd]
            score = jnp.where(allowed, jnp.where(forced, BIG_SCORE, imp), -1.0)
            score_t = _pad_rows(score, LANES).T
            sel_rows = []
            for i in range(sd):
                cm = score_t[:, i:i + 1]
                rn = score[i:i + 1, :]
                beats = jnp.where(n_idx > m_idx, jnp.where(cm >= rn, 1.0, 0.0), jnp.where(cm > rn, 1.0, 0.0))
                cnt = jnp.sum(beats, axis=0, keepdims=True)
                sel_rows.append(jnp.where(jnp.logical_and(cnt < NSA_TOPK - 1, allowed[i:i + 1, :]), 1.0, 0.0))
            sel = jnp.concatenate(sel_rows, axis=0)
            selr_ref[0, g] = jnp.concatenate([sel] * rper, axis=0).astype(BF16)


def _nsa_decode_attend_kernel(pt_ref, q_ref, ksn_ref, vsn_ref, kwn_ref, vwn_ref, sm_ref, kws_ref, vws_ref,
                              tot_ref, selr_ref, *rest, sd, n_steps, pp, past):
    del pt_ref
    ks_refs, vs_refs = rest[:pp], rest[pp:2 * pp]
    o_ref, qg_ref, m_ref, l_ref, acc_ref = rest[2 * pp:]
    j = pl.program_id(1)
    rper = NSA_HEADS // NSA_KV
    rows = rper * sd
    page = ks_refs[0].shape[3]
    bpp = page // NSA_BLOCK
    win = kws_ref.shape[3]
    i_row = jnp.bitwise_and(_iota((rows, LANES), 0), sd - 1)
    col = _iota((rows, LANES), 1)
    slopes = functools.partial(_nsa_decode_slopes, rows)
    group_q = functools.partial(_nsa_decode_group_q, qg_ref)

    @pl.when(j == 0)
    def _():
        _nsa_group_queries(q_ref, qg_ref, sd)
        m_ref[...] = jnp.full(m_ref.shape, NEG, F32)
        l_ref[...] = jnp.zeros_like(l_ref)
        acc_ref[...] = jnp.zeros_like(acc_ref)

    def selection_step():
        p0 = j * pp
        wide = pp * LANES
        colw = _iota((rows, wide), 1)
        iw = jnp.bitwise_and(_iota((rows, wide), 0), sd - 1)
        dist = (past + iw - (p0 * page + colw)).astype(F32)
        e_p = (_iota((LANES, wide), 0) ==
               p0 * bpp + lax.shift_right_logical(_iota((LANES, wide), 1), BLOCK_SHIFT)).astype(BF16)
        for g in range(NSA_KV):
            qg = group_q(g)
            qk = [_dot(qg, ks_refs[u][0, 0].astype(BF16)) for u in range(pp)]
            s = qk[0] if pp == 1 else jnp.concatenate(qk, axis=1)
            selb = (_dot(selr_ref[0, g], e_p) - 1.0) * (-NEG)
            s = s - slopes(g, wide) * dist + selb

            def pv(pr):
                out = _dot_nt(pr[:, :LANES], vs_refs[0][0, 0].astype(BF16))
                for u in range(1, pp):
                    out = out + _dot_nt(pr[:, u * LANES:(u + 1) * LANES], vs_refs[u][0, 0].astype(BF16))
                return out

            _softmax_update(s, pv, m_ref, l_ref, acc_ref, g)

    selection_step()

    @pl.when(j == n_steps - 1)
    def _():
        sig = jax.nn.sigmoid(sm_ref[...])

        def gate(g, branch):
            return jnp.concatenate(
                [jnp.broadcast_to(sig[:, GATE_LANE0 + 3 * (rper * g + r) + branch:
                                      GATE_LANE0 + 3 * (rper * g + r) + branch + 1], (sd, LANES))
                 for r in range(rper)], axis=0)

        new_ok = col <= i_row
        dist_new = (i_row - col).astype(F32)
        ksn = _pad_rows(ksn_ref[...], LANES).astype(BF16)
        vsn = _pad_rows(vsn_ref[...], LANES).astype(BF16)
        kwn = _pad_rows(kwn_ref[...], LANES).astype(BF16)
        vwn = _pad_rows(vwn_ref[...], LANES).astype(BF16)
        kws = kws_ref[0, 0].astype(BF16)
        vws = vws_ref[0, 0].astype(BF16)
        colw = _iota((rows, win), 1)
        iw = jnp.bitwise_and(_iota((rows, win), 0), sd - 1)
        dist_w = (win + iw - colw).astype(F32)
        ok_w = colw > iw
        tots = []
        for g in range(NSA_KV):
            qg = group_q(g)
            sl = slopes(g)
            s = jnp.where(new_ok, _dot_nt(qg, ksn) - sl * dist_new, NEG)
            _softmax_update(s, lambda pr: _dot(pr, vsn), m_ref, l_ref, acc_ref, g)
            o_sel = acc_ref[g] / l_ref[g]
            s1 = jnp.where(ok_w, _dot(qg, kws) - slopes(g, win) * dist_w, NEG)
            s2 = jnp.where(new_ok, _dot_nt(qg, kwn) - sl * dist_new, NEG)
            mx = jnp.maximum(jnp.max(s1, axis=1, keepdims=True), jnp.max(s2, axis=1, keepdims=True))
            p1 = jnp.exp(s1 - mx)
            p2 = jnp.exp(s2 - mx)
            den = jnp.sum(p1, axis=1, keepdims=True) + jnp.sum(p2, axis=1, keepdims=True)
            o_win = (_dot_nt(p1.astype(BF16), vws) + _dot(p2.astype(BF16), vwn)) / den
            t = tot_ref[0, g] + gate(g, 1) * o_sel + gate(g, 2) * o_win
            tots.append([t[r * sd:(r + 1) * sd] for r in range(rper)])
        _nsa_output(tots, o_ref)


def _nsa_decode(p32, page_table, pool_kc, pool_vc, pool_ks, pool_vs, st_kw, st_vw, wk_t, wv_t,
                layer, row0, DB, sd, pp):
    n_pages = page_table.shape[1]
    n_steps = n_pages // pp
    page = pool_kc.shape[2]
    past = n_pages * page
    W = NSA_HEADS * HEAD_DIM
    rper = NSA_HEADS // NSA_KV
    rows = rper * sd
    assert past // NSA_BLOCK <= LANES and past % NSA_BLOCK == 0 and sd <= NSA_BLOCK
    rb = row0 // sd
    win = st_kw.shape[2]
    new = lambda c: pl.BlockSpec((sd, LANES), lambda b, j, pt: (rb + b, C_NKV // LANES + c))
    pg = lambda u: (lambda b, j, pt: (layer, pt[b, j * pp + u], 0, 0))
    pages = _page_specs(pp, (1, 1, LANES, page), pg)
    q_spec = pl.BlockSpec((sd, W), lambda b, j, pt: (rb + b, C_NSAQ // W))
    sm_spec = pl.BlockSpec((sd, LANES), lambda b, j, pt: (rb + b, C_SMALL // LANES))
    state = pl.BlockSpec((1, 1, LANES, win), lambda b, j, pt: (layer, b, 0, 0))
    per_seq = pl.BlockSpec((1, NSA_KV, rows, LANES), lambda b, j, pt: (b, 0, 0, 0))
    tp = _transposed_pool
    tot_c, sel_rows = pl.pallas_call(
        functools.partial(_nsa_decode_pool_kernel, sd=sd, n_steps=n_steps, pp=pp, past=past),
        grid_spec=pltpu.PrefetchScalarGridSpec(
            num_scalar_prefetch=1,
            grid=(DB, n_steps),
            in_specs=[q_spec, sm_spec,
                      pl.BlockSpec(wk_t.shape, lambda b, j, pt: (0, 0)),
                      pl.BlockSpec(wv_t.shape, lambda b, j, pt: (0, 0))] + pages + pages,
            out_specs=[per_seq, per_seq],
            scratch_shapes=[pltpu.VMEM((LANES, LANES), F32), pltpu.VMEM((LANES, LANES), F32),
                            pltpu.VMEM((NSA_KV, rper, sd, LANES), F32)]),
        out_shape=[jax.ShapeDtypeStruct((DB, NSA_KV, rows, LANES), F32),
                   jax.ShapeDtypeStruct((DB, NSA_KV, rows, LANES), BF16)],
        compiler_params=_cparams(("parallel", "arbitrary")),
        name="nsa_decode_pool",
    )(page_table, p32, p32, wk_t, wv_t, *([tp(pool_kc)] * pp), *([tp(pool_vc)] * pp))
    return pl.pallas_call(
        functools.partial(_nsa_decode_attend_kernel, sd=sd, n_steps=n_steps, pp=pp, past=past),
        grid_spec=pltpu.PrefetchScalarGridSpec(
            num_scalar_prefetch=1,
            grid=(DB, n_steps),
            in_specs=[q_spec, new(2), new(3), new(4), new(5), sm_spec, state, state, per_seq, per_seq]
            + pages + pages,
            out_specs=pl.BlockSpec((sd, W), lambda b, j, pt: (b, 0)),
            scratch_shapes=[pltpu.VMEM((NSA_KV, rper, sd, LANES), F32),
                            pltpu.VMEM((NSA_KV, rows, LANES), F32), pltpu.VMEM((NSA_KV, rows, LANES), F32),
                            pltpu.VMEM((NSA_KV, rows, LANES), F32)]),
        out_shape=jax.ShapeDtypeStruct((DB * sd, W), F32),
        compiler_params=_cparams(("parallel", "arbitrary")),
        name="nsa_decode_attend",
    )(page_table, p32, p32, p32, p32, p32, p32, tp(st_kw), tp(st_vw), tot_c, sel_rows,
      *([tp(pool_ks)] * pp), *([tp(pool_vs)] * pp))


def _prep_in_weights(w_in, b_in):
    segs = {}
    off = 0
    for name, n in (("fox_q", 512), ("fox_k", 256), ("fox_v", 256), ("fox_f", 8), ("sb_q", 512), ("sb_k", 256),
                    ("sb_v", 256), ("nsa_q", 512), ("nsa_kv", 768), ("nsa_g", 24), ("z", 1536), ("merge", 3072)):
        segs[name] = (off, off + n)
        off += n
    assert off == w_in.shape[-1]
    scale = HEAD_DIM ** -0.5

    def build(a):
        sl = lambda nm: a[..., segs[nm][0]:segs[nm][1]]
        pad = jnp.zeros(a.shape[:-1] + (LANES - 32,), a.dtype)
        return jnp.concatenate([sl("fox_q") * scale, sl("fox_k"), sl("fox_v"), sl("sb_q") * scale, sl("sb_k"),
                                sl("sb_v"), sl("nsa_q") * scale, sl("z"), sl("merge"), sl("nsa_kv"),
                                sl("fox_f"), sl("nsa_g"), pad], axis=-1)
    wp = build(w_in).astype(BF16)
    bp = build(b_in)[:, None, :]
    assert wp.shape[-1] == N_PAD
    return wp, bp


def kernel(x_prompt, x_sample, cache_fox_k, cache_fox_v, cache_fox_logf, cache_sb_k, cache_sb_v, cache_nsa_kc, cache_nsa_vc, cache_nsa_ks, cache_nsa_vs, state_nsa_kw, state_nsa_vw, page_table, norm_g, w_in, b_in, w_cmp_k, w_cmp_v, w_br_fox, w_br_sb, w_br_nsa, w_out, final_norm_g):
    B, S, D = x_prompt.shape
    DB, SD, _ = x_sample.shape
    depth = w_in.shape[0]
    page = cache_fox_k.shape[2]
    n_pages = page_table.shape[1]
    win = state_nsa_kw.shape[2]
    tq = LANES
    tk_fox = _tile(S, 512, LANES)
    tk_sb = _tile(S, 256, LANES)
    pp_sb = _tile(n_pages, 32, 1)
    pp_fox = _tile(n_pages, 32, 1)
    pp_nsa = _tile(n_pages, 64, 1)
    assert cache_fox_k.shape[3:] == (FOX_KV, HEAD_DIM) and cache_fox_logf.shape[3] == FOX_HEADS
    assert cache_sb_k.shape[3:] == (SB_KV, HEAD_DIM) and cache_nsa_kc.shape[3:] == (NSA_KV, HEAD_DIM)
    assert w_br_fox.shape[1] == FOX_HEADS * HEAD_DIM and w_br_sb.shape[1] == SB_HEADS * HEAD_DIM
    assert w_br_nsa.shape[1] == NSA_HEADS * HEAD_DIM and w_cmp_k.shape[1:] == (NSA_BLOCK, NSA_KV)
    assert SD == DEC_ROWS and page == LANES and S % tq == 0 and S >= win and n_pages * page >= win
    TP, TS = B * S, DB * SD
    assert TP % SD == 0

    x = jnp.concatenate([x_prompt.reshape(TP, D), x_sample.reshape(TS, D)], axis=0)
    wp, bp = _prep_in_weights(w_in, b_in)
    wbr = jnp.stack([w_br_fox, w_br_sb, w_br_nsa], axis=1).astype(BF16)
    wout = w_out.astype(BF16)
    wk_exp = jnp.repeat(w_cmp_k, HEAD_DIM, axis=2)
    wv_exp = jnp.repeat(w_cmp_v, HEAD_DIM, axis=2)
    bpp = page // NSA_BLOCK
    key_table = _nsa_key_table(S)
    kv_bufs = [jnp.zeros((depth, B, kv * HEAD_DIM, S), F32) for _, kv in KV_LEAVES]

    h = _rmsnorm(x, norm_g[0], BF16)
    logf_p_all, logf_s_all, proj_s = [], [], []
    for l in range(depth):
        p32, p16 = _inproj(h, wp[l], bp[l])
        logf_p, ft = _foxgate(p32, B, S)
        ft4 = ft.reshape(B, FOX_KV // 2, 4, S // tk_fox, tk_fox).transpose(0, 1, 3, 2, 4)
        of_p = _fox_prompt(p32, p16, ft4, B, S, tq, tk_fox)
        os_p = _sb_prompt(p32, p16, B, S, tq, tk_sb)
        on_p = _nsa_prompt(p32, p16, wk_exp[l], wv_exp[l], key_table, B, S, tq, tk_fox, win)
        kv_bufs = _kv_rows(p32, kv_bufs, l, B, S)
        logf_p_all.append(logf_p.reshape(B, S, FOX_HEADS))
        of_s, logf_s = _fox_decode(p32, page_table, cache_fox_k, cache_fox_v, cache_fox_logf, l, TP, DB, SD, pp_fox)
        os_s = _sb_decode(p32, page_table, cache_sb_k, cache_sb_v, l, TP, DB, SD, pp_sb)
        wk_t = jnp.tile(wk_exp[l].T, (1, bpp))
        wv_t = jnp.tile(wv_exp[l].T, (1, bpp))
        on_s = _nsa_decode(p32, page_table, cache_nsa_kc, cache_nsa_vc, cache_nsa_ks, cache_nsa_vs,
                           state_nsa_kw, state_nsa_vw, wk_t, wv_t, l, TP, DB, SD, pp_nsa)
        last = l + 1 == depth
        g_next = final_norm_g if last else norm_g[l + 1]
        x, h = _merge((of_p, os_p, on_p), (of_s, os_s, on_s), p32, x, wbr[l], wout[l], g_next,
                      F32 if last else BF16)

        proj_s.append(p32[TP:])
        logf_s_all.append(logf_s.reshape(DB, SD, FOX_HEADS))

    y_prompt = h[:TP].reshape(B, S, D)
    y_sample = h[TP:].reshape(DB, SD, D)
    proj_s = jnp.stack(proj_s)
    kv_s = [proj_s[:, :, c0:c0 + kv * HEAD_DIM].reshape(depth, DB, SD, kv, HEAD_DIM) for c0, kv in KV_LEAVES]
    kv_s[8] = jnp.concatenate([state_nsa_kw, kv_s[8]], axis=2)[:, :, SD:]
    kv_s[9] = jnp.concatenate([state_nsa_vw, kv_s[9]], axis=2)[:, :, SD:]
    outs_s = kv_s[:2] + [jnp.stack(logf_s_all)] + kv_s[2:]
    kv_p = [buf.reshape(depth, B, kv, HEAD_DIM, S).transpose(0, 1, 4, 2, 3) for buf, (_, kv) in zip(kv_bufs, KV_LEAVES)]
    kv_p[8], kv_p[9] = kv_p[8][:, :, S - win:], kv_p[9][:, :, S - win:]
    outs_p = kv_p[:2] + [jnp.stack(logf_p_all)] + kv_p[2:]
    return (y_prompt, y_sample, *outs_p, *outs_s)
```
